```python
import math
import jax
import jax.numpy as jnp
from jax import lax
import numpy as np

D_MODEL = 1024
BATCH = 2
SEQ = 8192
DEPTH = 2
DEC_BATCH = 128
DEC_SEQ = 8
PAST_LEN = 8192
PAGE_SIZE = 128

H_A = 4
DK_A = 128
DV_A = 128
CONV_W = 4
CHUNK = 64
CONV_CH = 2 * H_A * DK_A + H_A * DV_A
H_B = 4
DH_B = 64
B_W = H_B * 2 * DH_B
H_C = 16
H_KV_C = 2
G_C = H_C // H_KV_C
DH_C = 64
WINDOW = 128
N_BUCKETS = 32
MAX_DIST = 128
N_BIAS_COLS = max(H_C, 2 * H_B)
QBLK = 128
N_GROUPS = 4
EXP_PER_GROUP = 8
N_EXPERTS = N_GROUPS * EXP_PER_GROUP
TOP_K = 2
D_EXPERT = 256
EPS = 1e-6

OFF_Z = CONV_CH
OFF_A = OFF_Z + H_A * DV_A
OFF_BETA = OFF_A + H_A
OFF_QB = OFF_BETA + H_A
OFF_KB = OFF_QB + B_W
OFF_VB = OFF_KB + B_W
IN0 = OFF_VB + B_W
MIX0 = H_A * DV_A + B_W
Q_C = H_C * DH_C
KV_C = H_KV_C * DH_C
IN1 = Q_C + 2 * KV_C

kernel_name = 'hybrid_gdn_diffattn_swa_hmoe_step'


def rmsnorm(x, w):
    xf = x.astype(jnp.float32)
    y = xf * lax.rsqrt(jnp.mean(xf * xf, axis=-1, keepdims=True) + EPS)
    return (y * w.astype(jnp.float32)).astype(x.dtype)


def l2norm(x):
    xf = x.astype(jnp.float32)
    return xf * lax.rsqrt(jnp.sum(xf * xf, axis=-1, keepdims=True) + EPS)


def t5_bucket(dist):
    d = jnp.maximum(dist, 0)
    max_exact = N_BUCKETS // 2
    ratio = jnp.log(jnp.maximum(d, 1).astype(jnp.float32) / max_exact) / math.log(MAX_DIST / max_exact)
    large = jnp.minimum(max_exact + (ratio * (N_BUCKETS - max_exact)).astype(jnp.int32), N_BUCKETS - 1)
    return jnp.where(d < max_exact, d, large)


def diff_bias(rel_bias, dist):
    cols = rel_bias[:, :2 * H_B].astype(jnp.float32).reshape(N_BUCKETS, H_B, 2)
    return jnp.transpose(cols[t5_bucket(dist)], (2, 3, 0, 1))


def swa_bias(rel_bias, dist):
    cols = rel_bias[:, :H_C].astype(jnp.float32).reshape(N_BUCKETS, H_KV_C, G_C)
    return jnp.transpose(cols[t5_bucket(dist)], (2, 3, 0, 1))


def causal_conv(x, buf, w):
    t = x.shape[1]
    xp = jnp.concatenate([buf.astype(x.dtype), x], axis=1)
    y = sum(xp[:, j:j + t] * w[:, j] for j in range(CONV_W))
    return y, xp[:, t:]


def gated_delta_rule(q, k, v, g, beta, s0):
    f32 = jnp.float32
    out_dtype = v.dtype
    b, t, h, dk = q.shape
    dv = v.shape[-1]
    c = min(CHUNK, t)
    pad = (-t) % c
    q, k, v, g, beta = (z.astype(f32) for z in (q, k, v, g, beta))
    if pad:
        padf = lambda z: jnp.pad(z, [(0, 0), (0, pad)] + [(0, 0)] * (z.ndim - 2))
        q, k, v, g, beta = (padf(z) for z in (q, k, v, g, beta))
    n = (t + pad) // c

    def chunks(z):
        z = z.reshape((b, n, c) + z.shape[2:])
        return jnp.moveaxis(jnp.moveaxis(z, 3, 2), 1, 0)

    qc, kc, vc, gc, bc = (chunks(z) for z in (q * dk ** -0.5, k, v, g, beta))
    gcum = jnp.cumsum(gc, axis=-1)
    idx = jnp.arange(c)
    lower = idx[:, None] >= idx[None, :]
    strict = idx[:, None] > idx[None, :]
    decay = jnp.exp(jnp.where(lower, gcum[..., :, None] - gcum[..., None, :], -jnp.inf))
    kb = kc * bc[..., None]
    amat = jnp.where(strict, jnp.einsum('nbhid,nbhjd->nbhij', kb, kc) * decay, 0.0) + jnp.eye(c, dtype=f32)
    rhs = jnp.concatenate([vc * bc[..., None], kb * jnp.exp(gcum)[..., None]], axis=-1)
    sol = lax.linalg.triangular_solve(amat, rhs, left_side=True, lower=True, unit_diagonal=True)
    u, w = sol[..., :dv], sol[..., dv:]
    qk = jnp.einsum('nbhid,nbhjd->nbhij', qc, kc) * decay

    def step(s, xs):
        q_n, k_n, u_n, w_n, g_n, qk_n = xs
        v_new = u_n - jnp.einsum('bhck,bhkv->bhcv', w_n, s)
        o_n = (jnp.einsum('bhck,bhkv->bhcv', q_n * jnp.exp(g_n)[..., None], s)
               + jnp.einsum('bhij,bhjv->bhiv', qk_n, v_new))
        g_last = g_n[..., -1:]
        s = (s * jnp.exp(g_last)[..., None]
             + jnp.einsum('bhck,bhcv->bhkv', k_n * jnp.exp(g_last - g_n)[..., None], v_new))
        return s, o_n

    s_fin, o = lax.scan(step, s0.astype(f32), (qc, kc, u, w, gcum, qk))
    o = jnp.moveaxis(jnp.moveaxis(o, 0, 1), 2, 3).reshape(b, n * c, h, dv)[:, :t]
    return o.astype(out_dtype), s_fin.astype(s0.dtype)


def gdn_heads(qkv, z, a, bl, conv_buf, s0, conv_w, a_log, dt_bias, norm_w):
    b, t, _ = qkv.shape
    qkv_c, new_buf = causal_conv(qkv, conv_buf, conv_w)
    qkv_c = jax.nn.silu(qkv_c)
    q, k, v = jnp.split(qkv_c, [H_A * DK_A, 2 * H_A * DK_A], axis=-1)
    q = l2norm(q.reshape(b, t, H_A, DK_A))
    k = l2norm(k.reshape(b, t, H_A, DK_A))
    v = v.reshape(b, t, H_A, DV_A)
    g = -jnp.exp(a_log.astype(jnp.float32)) * jax.nn.softplus(a.astype(jnp.float32) + dt_bias.astype(jnp.float32))
    beta = jax.nn.sigmoid(bl.astype(jnp.float32))
    o, s_new = gated_delta_rule(q, k, v, g, beta, s0)
    o = rmsnorm(o, norm_w) * jax.nn.silu(z.reshape(b, t, H_A, DV_A))
    return o.reshape(b, t, H_A * DV_A), new_buf, s_new


def diff_core(q, k, v, bias, valid, lam):
    s = jnp.einsum('bqhmd,bkhmd->bhmqk', q, k).astype(jnp.float32) * DH_B ** -0.5 + bias
    p = jax.nn.softmax(jnp.where(valid, s, -jnp.inf), axis=-1)
    pd = p[:, :, 0] - lam * p[:, :, 1]
    return jnp.einsum('bhqk,bkhe->bqhe', pd.astype(v.dtype), v)


def diff_attn_prompt(q, k, v, lam, rel_bias):
    b, t = q.shape[:2]
    nqb = t // QBLK
    qb = jnp.moveaxis(q.reshape(b, nqb, QBLK, H_B, 2, DH_B), 1, 0)
    kpos = jnp.arange(t)

    def block(args):
        i, q_i = args
        dist = (i * QBLK + jnp.arange(QBLK))[:, None] - kpos[None, :]
        return diff_core(q_i, k, v, diff_bias(rel_bias, dist), dist >= 0, lam)

    o = lax.map(block, (jnp.arange(nqb), qb))
    return jnp.moveaxis(o, 0, 1).reshape(b, t, H_B, 2 * DH_B)


def diff_attn_sample(q, k, v, lam, cache_k, cache_v, page_table, rel_bias):
    s_len = q.shape[1]
    past = page_table.shape[1] * PAGE_SIZE
    dist = (past + jnp.arange(s_len))[:, None] - jnp.arange(past + s_len)[None, :]
    bias = diff_bias(rel_bias, dist)
    valid = dist >= 0

    def one_seq(args):
        q_i, k_i, v_i, pages = args
        k_past = cache_k[pages].reshape(past, H_B, 2, DH_B).astype(k_i.dtype)
        v_past = cache_v[pages].reshape(past, H_B, 2 * DH_B).astype(v_i.dtype)
        keys = jnp.concatenate([k_past, k_i], axis=0)[None]
        vals = jnp.concatenate([v_past, v_i], axis=0)[None]
        return diff_core(q_i[None], keys, vals, bias, valid, lam)[0]

    return lax.map(one_seq, (q, k, v, page_table))


def even_mixer(xn, conv_buf, s0, attend, lam_init, w_in0, conv_w, a_log, dt_bias, gdn_norm_w,
               diff_lambda, diff_subln_w, w_out0):
    b, t, _ = xn.shape
    proj = xn @ w_in0
    qkv_a, z_a, a_a, b_a, q_b, k_b, v_b = jnp.split(proj, [OFF_Z, OFF_A, OFF_BETA, OFF_QB, OFF_KB, OFF_VB], axis=-1)
    o_a, new_buf, s_new = gdn_heads(qkv_a, z_a, a_a, b_a, conv_buf, s0, conv_w, a_log, dt_bias, gdn_norm_w)
    q_b = q_b.reshape(b, t, H_B, 2, DH_B)
    k_b = k_b.reshape(b, t, H_B, 2, DH_B)
    v_b = v_b.reshape(b, t, H_B, 2 * DH_B)
    lp = diff_lambda.astype(jnp.float32)
    lam = jnp.exp(jnp.sum(lp[0] * lp[1])) - jnp.exp(jnp.sum(lp[2] * lp[3])) + lam_init
    o_b = attend(q_b, k_b, v_b, lam)
    o_b = rmsnorm(o_b, diff_subln_w) * (1.0 - lam_init)
    out = jnp.concatenate([o_a, o_b.reshape(b, t, B_W)], axis=-1) @ w_out0
    return out, new_buf, s_new, k_b.reshape(b, t, H_B, 2 * DH_B), v_b


def sink_softmax(s, valid, sinks):
    sk = sinks.astype(jnp.float32).reshape(H_KV_C, G_C, 1, 1)
    s = jnp.where(valid, s, -jnp.inf)
    m = jnp.maximum(jnp.max(s, axis=-1, keepdims=True), sk)
    e = jnp.exp(s - m)
    return e / (jnp.sum(e, axis=-1, keepdims=True) + jnp.exp(sk - m))


def swa_prompt(q, k, v, sinks, rel_bias):
    b, t = q.shape[:2]
    nb = t // WINDOW

    def band(x):
        prev = jnp.concatenate([jnp.zeros_like(x[:, :WINDOW]), x[:, :t - WINDOW]], axis=1)
        return jnp.concatenate([prev.reshape(b, nb, WINDOW, H_KV_C, DH_C),
                                x.reshape(b, nb, WINDOW, H_KV_C, DH_C)], axis=2)

    kb, vb = band(k), band(v)
    qb = q.reshape(b, nb, WINDOW, H_KV_C, G_C, DH_C)
    i = jnp.arange(WINDOW)
    j = jnp.arange(2 * WINDOW)
    dist = WINDOW + i[:, None] - j[None, :]
    valid = ((dist >= 0) & (dist <= WINDOW))[None] & ((jnp.arange(nb) > 0)[:, None, None] | (j >= WINDOW)[None, None, :])
    s = jnp.einsum('bnqkgd,bnjkd->bnkgqj', qb, kb).astype(jnp.float32) * DH_C ** -0.5 + swa_bias(rel_bias, dist)
    p = sink_softmax(s, valid[:, None, None], sinks)
    o = jnp.einsum('bnkgqj,bnjkd->bnqkgd', p.astype(v.dtype), vb)
    return o.reshape(b, t, Q_C), k[:, t - WINDOW:], v[:, t - WINDOW:]


def swa_sample(q, k, v, buf_k, buf_v, sinks, rel_bias):
    s_len = q.shape[1]
    wb = buf_k.shape[1]
    keys = jnp.concatenate([buf_k.astype(k.dtype), k], axis=1)
    vals = jnp.concatenate([buf_v.astype(v.dtype), v], axis=1)
    dist = (wb + jnp.arange(s_len))[:, None] - jnp.arange(wb + s_len)[None, :]
    valid = (dist >= 0) & (dist <= WINDOW)
    s = jnp.einsum('bqkgd,bjkd->bkgqj', q, keys).astype(jnp.float32) * DH_C ** -0.5 + swa_bias(rel_bias, dist)
    p = sink_softmax(s, valid, sinks)
    o = jnp.einsum('bkgqj,bjkd->bqkgd', p.astype(vals.dtype), vals)
    return o.reshape(q.shape[0], s_len, Q_C), keys[:, s_len:], vals[:, s_len:]


def odd_mixer(xn, attend, w_in1, b_in1, w_out1):
    b, t, _ = xn.shape
    proj = xn @ w_in1 + b_in1
    q, k, v = jnp.split(proj, [Q_C, Q_C + KV_C], axis=-1)
    o, k_state, v_state = attend(q.reshape(b, t, H_KV_C, G_C, DH_C),
                                 k.reshape(b, t, H_KV_C, DH_C), v.reshape(b, t, H_KV_C, DH_C))
    return o @ w_out1, k_state, v_state


def hier_moe(x, w_group, b_group, w_router, b_router, w_gate, w_up, w_down):
    f32 = jnp.float32
    n = x.shape[0]
    rows = jnp.arange(n)
    gl = (x @ w_group).astype(f32) + b_group.astype(f32)
    g_sel = jnp.argmax(gl, axis=-1)
    g_w = jax.nn.softmax(gl, axis=-1)[rows, g_sel]
    el = ((x @ w_router).astype(f32) + b_router.astype(f32)).reshape(n, N_GROUPS, EXP_PER_GROUP)[rows, g_sel]
    top_v, top_i = lax.top_k(el, TOP_K)
    top_w = jax.nn.softmax(top_v, axis=-1) * g_w[:, None]
    e_idx = g_sel[:, None] * EXP_PER_GROUP + top_i
    gates = jnp.sum(jax.nn.one_hot(e_idx, N_EXPERTS, dtype=f32) * top_w[..., None], axis=1)
    h = jax.nn.silu(jnp.einsum('nd,edf->nef', x, w_gate)) * jnp.einsum('nd,edf->nef', x, w_up)
    return jnp.einsum('nef,ne,efd->nd', h, gates.astype(x.dtype), w_down)


def setup_inputs(seed: int = 0) -> dict:
    key = jax.random.key(seed)
    ks = jax.random.split(key, 40)
    f32 = jnp.float32
    nrm = lambda k, shape, scale: scale * jax.random.normal(k, shape, f32)
    n_pages = PAST_LEN // PAGE_SIZE
    n_used = DEC_BATCH * n_pages
    n_phys = n_used + max(1, n_used // 4)
    page_table = jax.random.permutation(ks[0], n_phys)[:n_used].reshape(DEC_BATCH, n_pages).astype(jnp.int32)
    dt = jnp.exp(jax.random.uniform(ks[1], (H_A,), f32, math.log(1e-3), math.log(1e-1)))
    return {
        'x_prompt': nrm(ks[2], (BATCH, SEQ, D_MODEL), 1.0),
        'x_sample': nrm(ks[3], (DEC_BATCH, DEC_SEQ, D_MODEL), 1.0),
        'state_a_conv': nrm(ks[4], (DEC_BATCH, CONV_W - 1, CONV_CH), 1.0),
        'state_a_ssm': nrm(ks[5], (DEC_BATCH, H_A, DK_A, DV_A), 0.5),
        'cache_b_k': nrm(ks[6], (n_phys, PAGE_SIZE, H_B, 2 * DH_B), 1.0),
        'cache_b_v': nrm(ks[7], (n_phys, PAGE_SIZE, H_B, 2 * DH_B), 1.0),
        'cache_c_k': nrm(ks[8], (DEC_BATCH, WINDOW, H_KV_C, DH_C), 1.0),
        'cache_c_v': nrm(ks[9], (DEC_BATCH, WINDOW, H_KV_C, DH_C), 1.0),
        'page_table': page_table,
        'norm_mix': 1.0 + nrm(ks[10], (DEPTH, D_MODEL), 0.02),
        'norm_ffn': 1.0 + nrm(ks[11], (DEPTH, D_MODEL), 0.02),
        'norm_final': 1.0 + nrm(ks[12], (D_MODEL,), 0.02),
        'rel_bias': nrm(ks[13], (N_BUCKETS, N_BIAS_COLS), 0.5),
        'w_in0': nrm(ks[14], (D_MODEL, IN0), D_MODEL ** -0.5),
        'conv_w': nrm(ks[15], (CONV_CH, CONV_W), 0.5),
        'a_log': jnp.log(jax.random.uniform(ks[16], (H_A,), f32, 1.0, 16.0)),
        'dt_bias': dt + jnp.log(-jnp.expm1(-dt)),
        'gdn_norm_w': 1.0 + nrm(ks[17], (DV_A,), 0.02),
        'diff_lambda': nrm(ks[18], (4, DH_B), 0.1),
        'diff_subln_w': 1.0 + nrm(ks[19], (2 * DH_B,), 0.02),
        'w_out0': nrm(ks[20], (MIX0, D_MODEL), MIX0 ** -0.5),
        'w_in1': nrm(ks[21], (D_MODEL, IN1), D_MODEL ** -0.5),
        'b_in1': nrm(ks[22], (IN1,), 0.02),
        'sinks': nrm(ks[23], (H_C,), 0.5),
        'w_out1': nrm(ks[24], (Q_C, D_MODEL), Q_C ** -0.5),
        'w_group': nrm(ks[25], (DEPTH, D_MODEL, N_GROUPS), D_MODEL ** -0.5),
        'b_group': nrm(ks[26], (DEPTH, N_GROUPS), 0.01),
        'w_router': nrm(ks[27], (DEPTH, D_MODEL, N_EXPERTS), D_MODEL ** -0.5),
        'b_router': nrm(ks[28], (DEPTH, N_EXPERTS), 0.01),
        'w_gate': nrm(ks[29], (DEPTH, N_EXPERTS, D_MODEL, D_EXPERT), D_MODEL ** -0.5),
        'w_up': nrm(ks[30], (DEPTH, N_EXPERTS, D_MODEL, D_EXPERT), D_MODEL ** -0.5),
        'w_down': nrm(ks[31], (DEPTH, N_EXPERTS, D_EXPERT, D_MODEL), D_EXPERT ** -0.5),
    }


def reference(x_prompt, x_sample, state_a_conv, state_a_ssm, cache_b_k, cache_b_v, cache_c_k, cache_c_v,
              page_table, norm_mix, norm_ffn, norm_final, rel_bias, w_in0, conv_w, a_log, dt_bias, gdn_norm_w,
              diff_lambda, diff_subln_w, w_out0, w_in1, b_in1, sinks, w_out1, w_group, b_group, w_router,
              b_router, w_gate, w_up, w_down):
    bp = x_prompt.shape[0]
    hp, hs = x_prompt, x_sample
    for l in range(DEPTH):
        xp = rmsnorm(hp, norm_mix[l])
        xs = rmsnorm(hs, norm_mix[l])
        if l % 2 == 0:
            lam_init = 0.8 - 0.6 * math.exp(-0.3 * l)
            dp, conv_p, ssm_p, kb_p, vb_p = even_mixer(
                xp, jnp.zeros((bp, CONV_W - 1, CONV_CH), xp.dtype), jnp.zeros((bp, H_A, DK_A, DV_A), xp.dtype),
                lambda q, k, v, lam: diff_attn_prompt(q, k, v, lam, rel_bias), lam_init,
                w_in0, conv_w, a_log, dt_bias, gdn_norm_w, diff_lambda, diff_subln_w, w_out0)
            ds, conv_s, ssm_s, kb_s, vb_s = even_mixer(
                xs, state_a_conv, state_a_ssm,
                lambda q, k, v, lam: diff_attn_sample(q, k, v, lam, cache_b_k, cache_b_v, page_table, rel_bias),
                lam_init, w_in0, conv_w, a_log, dt_bias, gdn_norm_w, diff_lambda, diff_subln_w, w_out0)
        else:
            dp, kc_p, vc_p = odd_mixer(xp, lambda q, k, v: swa_prompt(q, k, v, sinks, rel_bias),
                                       w_in1, b_in1, w_out1)
            ds, kc_s, vc_s = odd_mixer(xs, lambda q, k, v: swa_sample(q, k, v, cache_c_k, cache_c_v, sinks, rel_bias),
                                       w_in1, b_in1, w_out1)
        hp = hp + dp
        hs = hs + ds
        hp = hp + hier_moe(rmsnorm(hp, norm_ffn[l]).reshape(-1, D_MODEL), w_group[l], b_group[l], w_router[l],
                           b_router[l], w_gate[l], w_up[l], w_down[l]).reshape(hp.shape)
        hs = hs + hier_moe(rmsnorm(hs, norm_ffn[l]).reshape(-1, D_MODEL), w_group[l], b_group[l], w_router[l],
                           b_router[l], w_gate[l], w_up[l], w_down[l]).reshape(hs.shape)
    y_prompt = rmsnorm(hp, norm_final)
    y_sample = rmsnorm(hs, norm_final)
    return (y_prompt, y_sample, conv_p, conv_s, ssm_p, ssm_s, kb_p, vb_p, kb_s, vb_s, kc_p, vc_p, kc_s, vc_s)
```

```python
import functools
import math

import jax
import jax.numpy as jnp
import numpy as np
from jax import lax
from jax.experimental import pallas as pl
from jax.experimental.pallas import tpu as pltpu

F32 = jnp.float32
BF16 = jnp.bfloat16
EPS = 1e-6
NEG = -1e30
LANES = 128
VMEM_LIMIT = 56 * 1024 * 1024

D_MODEL = 1024
H_A, DK_A, DV_A, CONV_W = 4, 128, 128, 4
CONV_CH = 2 * H_A * DK_A + H_A * DV_A
GDN_CHUNK = 64
H_B, DH_B = 4, 64
B_W = H_B * 2 * DH_B
PAGE = 128
H_C, H_KV_C, DH_C, WINDOW = 16, 2, 64, 128
G_C = H_C // H_KV_C
Q_C = H_C * DH_C
KV_C = H_KV_C * DH_C
N_BUCKETS, MAX_DIST = 32, 128
N_GROUPS, EXP_PER_GROUP, TOP_K, D_EXPERT = 4, 8, 2, 256
N_EXPERTS = N_GROUPS * EXP_PER_GROUP
LAM_INIT0 = 0.8 - 0.6 * math.exp(-0.3 * 0)


def _params(*sem):
    return pltpu.CompilerParams(dimension_semantics=sem, vmem_limit_bytes=VMEM_LIMIT)


def _mm(a, b, precise=False):
    if precise:
        return jnp.dot(a.astype(F32), b.astype(F32), preferred_element_type=F32, precision=lax.Precision.HIGHEST)
    return jnp.dot(a.astype(BF16), b.astype(BF16), preferred_element_type=F32)


def _mm_nt(a, b, precise=False):
    dn = (((1,), (1,)), ((), ()))
    if precise:
        return lax.dot_general(a.astype(F32), b.astype(F32), dn, preferred_element_type=F32,
                               precision=lax.Precision.HIGHEST)
    return lax.dot_general(a.astype(BF16), b.astype(BF16), dn, preferred_element_type=F32)


def _mm_tn(a, b, precise=False):
    dn = (((0,), (0,)), ((), ()))
    if precise:
        return lax.dot_general(a.astype(F32), b.astype(F32), dn, preferred_element_type=F32,
                               precision=lax.Precision.HIGHEST)
    return lax.dot_general(a.astype(BF16), b.astype(BF16), dn, preferred_element_type=F32)


def _sigmoid(x):
    return 1.0 / (1.0 + jnp.exp(-x))


def _silu(x):
    return x * _sigmoid(x)


def _softplus(x):
    return jnp.maximum(x, 0.0) + jnp.log(1.0 + jnp.exp(-jnp.abs(x)))


def _norm_proj_kernel(n_add, widths, has_bias, emit_h, *refs):
    x_ref = refs[0]
    add_refs = refs[1:1 + n_add]
    nw_ref, w_ref = refs[1 + n_add], refs[2 + n_add]
    pos = 3 + n_add
    b_ref = None
    if has_bias:
        b_ref = refs[pos]
        pos += 1
    outs = refs[pos:]
    h = x_ref[...]
    for a in add_refs:
        h = h + a[...]
    if emit_h:
        outs[0][...] = h
        outs = outs[1:]
    xn = h * lax.rsqrt(jnp.mean(h * h, axis=-1, keepdims=True) + EPS) * nw_ref[...]
    xb = xn.astype(BF16)
    off = 0
    for o_ref, wd in zip(outs, widths):
        y = jnp.dot(xb, w_ref[:, off:off + wd], preferred_element_type=F32)
        if has_bias:
            y = y + b_ref[:, off:off + wd]
        o_ref[...] = y
        off += wd


def norm_proj(x, addends, norm_w, w_bf16, bias, widths, emit_h, tm=256):
    n, d = x.shape
    m = w_bf16.shape[1]
    tm = min(tm, n)
    assert sum(widths) == m and n % tm == 0
    row = lambda i: (i, 0)
    fixed = lambda i: (0, 0)
    in_specs = [pl.BlockSpec((tm, d), row)] * (1 + len(addends))
    in_specs += [pl.BlockSpec((1, d), fixed), pl.BlockSpec((d, m), fixed)]
    args = [x, *addends, norm_w.reshape(1, d), w_bf16]
    if bias is not None:
        in_specs.append(pl.BlockSpec((1, m), fixed))
        args.append(bias.reshape(1, m))
    out_shape, out_specs = [], []
    if emit_h:
        out_shape.append(jax.ShapeDtypeStruct((n, d), F32))
        out_specs.append(pl.BlockSpec((tm, d), row))
    for wd in widths:
        out_shape.append(jax.ShapeDtypeStruct((n, wd), F32))
        out_specs.append(pl.BlockSpec((tm, wd), row))
    return pl.pallas_call(
        functools.partial(_norm_proj_kernel, len(addends), tuple(widths), bias is not None, emit_h),
        grid=(n // tm,), in_specs=in_specs, out_specs=out_specs, out_shape=out_shape,
        compiler_params=_params("parallel"), name="norm_proj",
    )(*args)


def _gdn_kernel(c_len, qkv_ref, z_ref, ab_ref, cbuf_ref, s0_ref, convw_ref, gpar_ref, nw_ref,
                o_ref, sfin_ref, xp_scr, s_scr):
    c = pl.program_id(1)
    n_c = pl.num_programs(1)
    hist = CONV_W - 1
    base = 8 - hist

    @pl.when(c == 0)
    def _():
        xp_scr[base:8, :] = cbuf_ref[0]
        s_scr[...] = s0_ref[0]

    xp_scr[8:8 + c_len, :] = qkv_ref[0]
    y = xp_scr[base:base + c_len, :] * convw_ref[0:1, :]
    for j in range(1, CONV_W):
        y = y + xp_scr[base + j:base + j + c_len, :] * convw_ref[j:j + 1, :]
    xp_scr[base:8, :] = xp_scr[base + c_len:8 + c_len, :]
    y = _silu(y)

    ab = ab_ref[0]
    g_t = gpar_ref[0:1, :] * _softplus(ab + gpar_ref[1:2, :])
    beta_t = _sigmoid(ab)
    ri = lax.broadcasted_iota(jnp.int32, (c_len, c_len), 0)
    ci = lax.broadcasted_iota(jnp.int32, (c_len, c_len), 1)
    lower = ri >= ci
    strict = ri > ci
    eye = (ri == ci).astype(F32)
    gcum_t = _mm(lower.astype(F32), g_t, precise=True)
    gcum_tt = gcum_t.T

    for h in range(H_A):
        q = y[:, h * DK_A:(h + 1) * DK_A]
        k = y[:, (H_A + h) * DK_A:(H_A + h + 1) * DK_A]
        v = y[:, 2 * H_A * DK_A + h * DV_A:2 * H_A * DK_A + (h + 1) * DV_A]
        q = q * lax.rsqrt(jnp.sum(q * q, axis=-1, keepdims=True) + EPS) * (DK_A ** -0.5)
        k = k * lax.rsqrt(jnp.sum(k * k, axis=-1, keepdims=True) + EPS)
        gc = gcum_t[:, h:h + 1]
        gr = gcum_tt[h:h + 1, :]
        beta = beta_t[:, H_A + h:H_A + h + 1]
        decay = jnp.where(lower, jnp.exp(jnp.where(lower, gc - gr, 0.0)), 0.0)
        kb = k * beta
        nmat = jnp.where(strict, _mm_nt(kb, k, True) * decay, 0.0)
        inv = eye - nmat
        pw = _mm(nmat, nmat, True)
        span = 2
        while span < c_len:
            inv = inv + _mm(inv, pw, True)
            span *= 2
            if span < c_len:
                pw = _mm(pw, pw, True)
        rhs = jnp.concatenate([v * beta, kb * jnp.exp(gc)], axis=1)
        sol = _mm(inv, rhs, True)
        u, w = sol[:, :DV_A], sol[:, DV_A:]
        qk = jnp.where(lower, _mm_nt(q, k, True) * decay, 0.0)
        s = s_scr[h]
        v_new = u - _mm(w, s, True)
        o = _mm(q * jnp.exp(gc), s, True) + _mm(qk, v_new, True)
        g_last = gc[c_len - 1:c_len, :]
        s_scr[h] = s * jnp.exp(g_last) + _mm_tn(k * jnp.exp(g_last - gc), v_new, True)
        o = o * lax.rsqrt(jnp.mean(o * o, axis=-1, keepdims=True) + EPS) * nw_ref[...]
        o_ref[0, :, h * DV_A:(h + 1) * DV_A] = o * _silu(z_ref[0, :, h * DV_A:(h + 1) * DV_A])

    @pl.when(c == n_c - 1)
    def _():
        sfin_ref[0] = s_scr[...]


def gdn_heads(qkv, z, ab, conv_buf, s0, conv_w, a_log, dt_bias, norm_w, c_len):
    b, t, _ = qkv.shape
    assert t % c_len == 0
    convw_t = conv_w.T
    gpar = jnp.zeros((2, LANES), F32)
    gpar = gpar.at[0, :H_A].set(-jnp.exp(a_log.astype(F32))).at[1, :H_A].set(dt_bias.astype(F32))
    tok = lambda i, j: (i, j, 0)
    seq3 = lambda i, j: (i, 0, 0)
    fixed = lambda i, j: (0, 0)
    return pl.pallas_call(
        functools.partial(_gdn_kernel, c_len),
        grid=(b, t // c_len),
        in_specs=[
            pl.BlockSpec((1, c_len, CONV_CH), tok),
            pl.BlockSpec((1, c_len, H_A * DV_A), tok),
            pl.BlockSpec((1, c_len, LANES), tok),
            pl.BlockSpec((1, CONV_W - 1, CONV_CH), seq3),
            pl.BlockSpec((1, H_A, DK_A, DV_A), lambda i, j: (i, 0, 0, 0)),
            pl.BlockSpec((CONV_W, CONV_CH), fixed),
            pl.BlockSpec((2, LANES), fixed),
            pl.BlockSpec((1, DV_A), fixed),
        ],
        out_specs=[
            pl.BlockSpec((1, c_len, H_A * DV_A), tok),
            pl.BlockSpec((1, H_A, DK_A, DV_A), lambda i, j: (i, 0, 0, 0)),
        ],
        out_shape=[
            jax.ShapeDtypeStruct((b, t, H_A * DV_A), F32),
            jax.ShapeDtypeStruct((b, H_A, DK_A, DV_A), F32),
        ],
        scratch_shapes=[
            pltpu.VMEM((8 + c_len, CONV_CH), F32),
            pltpu.VMEM((H_A, DK_A, DV_A), F32),
        ],
        compiler_params=_params("parallel", "arbitrary"), name="gdn_heads",
    )(qkv, z, ab, conv_buf, s0, convw_t, gpar, norm_w.reshape(1, DV_A))


def _t5_bucket_np(dist):
    d = np.maximum(dist, 0)
    max_exact = N_BUCKETS // 2
    ratio = (np.log(np.maximum(d, 1).astype(np.float32) / np.float32(max_exact))
             / np.float32(math.log(MAX_DIST / max_exact)))
    large = np.minimum(max_exact + (ratio * (N_BUCKETS - max_exact)).astype(np.int32), N_BUCKETS - 1)
    return np.where(d < max_exact, d, large).astype(np.int32)


FAR_BUCKET = int(_t5_bucket_np(np.array([MAX_DIST]))[0])
assert np.all(_t5_bucket_np(np.arange(MAX_DIST, 4 * MAX_DIST)) == FAR_BUCKET)


def _bias_kernel(n_cols, bucket_ref, rb_ref, out_ref):
    bk = bucket_ref[...]
    for col in range(n_cols):
        acc = jnp.full(bk.shape, NEG, F32)
        for b in range(N_BUCKETS):
            acc = jnp.where(bk == b, rb_ref[b, col], acc)
        out_ref[col] = acc


def bias_tiles(rel_bias, dist, valid, n_cols):
    bucket = np.where(valid, _t5_bucket_np(dist), -1).astype(np.int32)
    r, c = bucket.shape
    return pl.pallas_call(
        functools.partial(_bias_kernel, n_cols),
        in_specs=[pl.BlockSpec(memory_space=pltpu.VMEM), pl.BlockSpec(memory_space=pltpu.SMEM)],
        out_specs=pl.BlockSpec(memory_space=pltpu.VMEM),
        out_shape=jax.ShapeDtypeStruct((n_cols, r, c), F32),
        name="bias_tiles",
    )(jnp.asarray(bucket), rel_bias.astype(F32))


def _diff_lambda(lam_ref):
    lp = lam_ref[...]
    s1 = jnp.sum(lp[0:1] * lp[1:2], axis=-1, keepdims=True)
    s2 = jnp.sum(lp[2:3] * lp[3:4], axis=-1, keepdims=True)
    return jnp.exp(s1) - jnp.exp(s2) + LAM_INIT0


def _diff_finish(o1, o2, lam, nw):
    o = o1 - lam * o2
    return o * lax.rsqrt(jnp.mean(o * o, axis=-1, keepdims=True) + EPS) * nw * (1.0 - LAM_INIT0)


def _softmax_update(s, v_bf, m_ref, l_ref, acc_ref):
    m_prev = m_ref[...]
    m_new = jnp.maximum(m_prev, jnp.max(s, axis=-1, keepdims=True))
    alpha = jnp.exp(m_prev - m_new)
    p = jnp.exp(s - m_new)
    l_ref[...] = alpha * l_ref[...] + jnp.sum(p, axis=-1, keepdims=True)
    acc_ref[...] = alpha * acc_ref[...] + jnp.dot(p.astype(BF16), v_bf, preferred_element_type=F32)
    m_ref[...] = m_new


def _diff_prompt_kernel(blk, q_ref, k_ref, v_ref, t0_ref, t1_ref, far_ref, lam_ref, nw_ref, o_ref,
                        m_scr, l_scr, acc_scr):
    qi = pl.program_id(1)
    ki = pl.program_id(2)
    nsub = blk // LANES

    @pl.when(ki == 0)
    def _():
        m_scr[...] = jnp.full(m_scr.shape, NEG, F32)
        l_scr[...] = jnp.zeros(l_scr.shape, F32)
        acc_scr[...] = jnp.zeros(acc_scr.shape, F32)

    def bias_block(case, hm):
        c = far_ref[hm]
        if case == "far":
            return c
        const = jnp.full((LANES, LANES), c, F32)
        neg = jnp.full((LANES, LANES), NEG, F32)
        rows = []
        for i in range(nsub):
            tiles = []
            for j in range(nsub):
                sub = i - j + (nsub if case == "prev" else 0)
                tiles.append(neg if sub < 0 else t0_ref[hm] if sub == 0 else t1_ref[hm] if sub == 1 else const)
            rows.append(jnp.concatenate(tiles, axis=1))
        return jnp.concatenate(rows, axis=0)

    def step(case):
        q = (q_ref[0] * (DH_B ** -0.5)).astype(BF16)
        k = k_ref[0].astype(BF16)
        v = v_ref[0].astype(BF16)
        first = lax.broadcasted_iota(jnp.int32, (1, 2 * DH_B), 1) < DH_B
        for h in range(H_B):
            sl = slice(h * 2 * DH_B, (h + 1) * 2 * DH_B)
            qh, kh, vh = q[:, sl], k[:, sl], v[:, sl]
            for mp in range(2):
                hm = 2 * h + mp
                km = jnp.where(first if mp == 0 else jnp.logical_not(first), kh, jnp.zeros_like(kh))
                s = lax.dot_general(qh, km, (((1,), (1,)), ((), ())), preferred_element_type=F32)
                s = s + bias_block(case, hm)
                _softmax_update(s, vh, m_scr.at[hm], l_scr.at[hm], acc_scr.at[hm])

    @pl.when(ki < qi - 1)
    def _():
        step("far")

    @pl.when(ki == qi - 1)
    def _():
        step("prev")

    @pl.when(ki == qi)
    def _():
        step("diag")
        lam = _diff_lambda(lam_ref)
        for h in range(H_B):
            o1 = acc_scr[2 * h] / l_scr[2 * h]
            o2 = acc_scr[2 * h + 1] / l_scr[2 * h + 1]
            o_ref[0, :, h * 2 * DH_B:(h + 1) * 2 * DH_B] = _diff_finish(o1, o2, lam, nw_ref[...])


def diff_attn_prompt(q, k, v, rel_bias, diff_lambda, subln_w, blk):
    b, t, _ = q.shape
    assert t % blk == 0 and blk % LANES == 0 and blk >= 2 * LANES
    i = np.arange(LANES)
    d0 = i[:, None] - i[None, :]
    t0 = bias_tiles(rel_bias, d0, d0 >= 0, 2 * H_B)
    t1 = bias_tiles(rel_bias, d0 + LANES, np.ones_like(d0, bool), 2 * H_B)
    far = rel_bias[FAR_BUCKET, :2 * H_B].astype(F32)
    nb = t // blk
    whole = lambda shape: pl.BlockSpec(shape, lambda bi, qi, ki: (0,) * len(shape))
    return pl.pallas_call(
        functools.partial(_diff_prompt_kernel, blk),
        grid=(b, nb, nb),
        in_specs=[
            pl.BlockSpec((1, blk, B_W), lambda bi, qi, ki: (bi, qi, 0)),
            pl.BlockSpec((1, blk, B_W), lambda bi, qi, ki: (bi, jnp.minimum(ki, qi), 0)),
            pl.BlockSpec((1, blk, B_W), lambda bi, qi, ki: (bi, jnp.minimum(ki, qi), 0)),
            whole((2 * H_B, LANES, LANES)),
            whole((2 * H_B, LANES, LANES)),
            pl.BlockSpec(memory_space=pltpu.SMEM),
            whole((4, DH_B)),
            whole((1, 2 * DH_B)),
        ],
        out_specs=pl.BlockSpec((1, blk, B_W), lambda bi, qi, ki: (bi, qi, 0)),
        out_shape=jax.ShapeDtypeStruct((b, t, B_W), F32),
        scratch_shapes=[
            pltpu.VMEM((2 * H_B, blk, 1), F32),
            pltpu.VMEM((2 * H_B, blk, 1), F32),
            pltpu.VMEM((2 * H_B, blk, 2 * DH_B), F32),
        ],
        compiler_params=_params("parallel", "parallel", "arbitrary"), name="diff_attn_prompt",
    )(q, k, v, t0, t1, far, diff_lambda.astype(F32), subln_w.reshape(1, 2 * DH_B).astype(F32))


def _diff_sample_kernel(n_pg, s_len, pt_ref, q_ref, kn_ref, vn_ref, *refs):
    k_pages = refs[:n_pg]
    v_pages = refs[n_pg:2 * n_pg]
    blast_ref, bnew_ref, far_ref, lam_ref, nw_ref, o_ref, m_scr, l_scr, acc_scr = refs[2 * n_pg:]
    j = pl.program_id(1)
    last = pl.num_programs(1) - 1
    n_rows = 2 * H_B * s_len

    @pl.when(j == 0)
    def _():
        m_scr[...] = jnp.full(m_scr.shape, NEG, F32)
        l_scr[...] = jnp.zeros(l_scr.shape, F32)
        acc_scr[...] = jnp.zeros(acc_scr.shape, F32)

    q = q_ref[0] * (DH_B ** -0.5)
    q_rep = jnp.concatenate([q] * (2 * H_B), axis=0)
    rr = lax.broadcasted_iota(jnp.int32, (n_rows, B_W), 0) // s_len
    cc = lax.broadcasted_iota(jnp.int32, (n_rows, B_W), 1) // DH_B
    qbd = jnp.where(rr == cc, q_rep, 0.0).astype(BF16)

    kcat = jnp.concatenate([r[0] for r in k_pages], axis=0).astype(BF16)
    vcat = jnp.concatenate([r[0] for r in v_pages], axis=0).astype(BF16)
    s = lax.dot_general(qbd, kcat, (((1,), (1,)), ((), ())), preferred_element_type=F32)
    s = s + jnp.where(j == last, blast_ref[...], far_ref[...])
    _softmax_update(s, vcat, m_scr, l_scr, acc_scr)

    @pl.when(j == last)
    def _():
        pad = jnp.zeros((PAGE - s_len, B_W), F32)
        kn = jnp.concatenate([kn_ref[0], pad], axis=0).astype(BF16)
        vn = jnp.concatenate([vn_ref[0], pad], axis=0).astype(BF16)
        s2 = lax.dot_general(qbd, kn, (((1,), (1,)), ((), ())), preferred_element_type=F32) + bnew_ref[...]
        _softmax_update(s2, vn, m_scr, l_scr, acc_scr)
        accn = acc_scr[...] / l_scr[...]
        lam = _diff_lambda(lam_ref)
        for h in range(H_B):
            sl = slice(h * 2 * DH_B, (h + 1) * 2 * DH_B)
            o1 = accn[(2 * h) * s_len:(2 * h + 1) * s_len, sl]
            o2 = accn[(2 * h + 1) * s_len:(2 * h + 2) * s_len, sl]
            o_ref[0, :, sl] = _diff_finish(o1, o2, lam, nw_ref[...])


def diff_attn_sample(q, k, v, cache_k, cache_v, page_table, rel_bias, diff_lambda, subln_w, n_pg):
    s_n, s_len, _ = q.shape
    pages_per_seq = page_table.shape[1]
    assert pages_per_seq % n_pg == 0
    past = pages_per_seq * PAGE
    n_phys = cache_k.shape[0]
    ck = cache_k.reshape(n_phys, PAGE, B_W)
    cv = cache_v.reshape(n_phys, PAGE, B_W)
    n_rows = 2 * H_B * s_len
    blk_keys = n_pg * PAGE
    t = np.arange(s_len)
    d_last = (past + t)[:, None] - (past - blk_keys + np.arange(blk_keys))[None, :]
    blast = bias_tiles(rel_bias, d_last, np.ones_like(d_last, bool), 2 * H_B).reshape(n_rows, blk_keys)
    cnew = np.arange(PAGE)
    d_new = t[:, None] - cnew[None, :]
    bnew = bias_tiles(rel_bias, d_new, (d_new >= 0) & (cnew[None, :] < s_len), 2 * H_B).reshape(n_rows, PAGE)
    far = jnp.repeat(rel_bias[FAR_BUCKET, :2 * H_B].astype(F32), s_len).reshape(n_rows, 1)
    assert past - blk_keys + s_len - 1 >= 0 and np.all(_t5_bucket_np(np.array([blk_keys + 1])) == FAR_BUCKET)

    def page_spec(i):
        return pl.BlockSpec((1, PAGE, B_W), lambda s, j, pt: (pt[s * pages_per_seq + j * n_pg + i], 0, 0))

    seq = pl.BlockSpec((1, s_len, B_W), lambda s, j, pt: (s, 0, 0))
    whole = lambda shape: pl.BlockSpec(shape, lambda s, j, pt: (0,) * len(shape))
    grid_spec = pltpu.PrefetchScalarGridSpec(
        num_scalar_prefetch=1,
        grid=(s_n, pages_per_seq // n_pg),
        in_specs=[seq, seq, seq] + [page_spec(i) for i in range(n_pg)] * 2 + [
            whole((n_rows, blk_keys)), whole((n_rows, PAGE)), whole((n_rows, 1)),
            whole((4, DH_B)), whole((1, 2 * DH_B))],
        out_specs=seq,
        scratch_shapes=[
            pltpu.VMEM((n_rows, 1), F32),
            pltpu.VMEM((n_rows, 1), F32),
            pltpu.VMEM((n_rows, B_W), F32),
        ],
    )
    return pl.pallas_call(
        functools.partial(_diff_sample_kernel, n_pg, s_len),
        grid_spec=grid_spec,
        out_shape=jax.ShapeDtypeStruct((s_n, s_len, B_W), F32),
        compiler_params=_params("parallel", "arbitrary"), name="diff_attn_sample",
    )(page_table.reshape(-1).astype(jnp.int32), q, k, v, *([ck] * n_pg), *([cv] * n_pg),
      blast, bnew, far, diff_lambda.astype(F32), subln_w.reshape(1, 2 * DH_B).astype(F32))


def _kv_variants(x, kv):
    lo = lax.broadcasted_iota(jnp.int32, (1, LANES), 1) < DH_C
    rolled = pltpu.roll(x, DH_C, 1)
    a_src, b_src = (x, rolled) if kv == 0 else (rolled, x)
    zero = jnp.zeros_like(x)
    return jnp.where(lo, a_src, zero).astype(BF16), jnp.where(lo, zero, b_src).astype(BF16)


def _sink_attend(q_bf, k_ab, v_ab, bias_ab, sink_ab):
    o = None
    for kx, vx, bias, sink in zip(k_ab, v_ab, bias_ab, sink_ab):
        s = lax.dot_general(q_bf, kx, (((1,), (1,)), ((), ())), preferred_element_type=F32) + bias
        m = jnp.maximum(jnp.max(s, axis=-1, keepdims=True), sink)
        e = jnp.exp(s - m)
        p = e / (jnp.sum(e, axis=-1, keepdims=True) + jnp.exp(sink - m))
        pv = jnp.dot(p.astype(BF16), vx, preferred_element_type=F32)
        o = pv if o is None else o + pv
    return o


def _swa_prompt_kernel(q_ref, kp_ref, kc_ref, vp_ref, vc_ref, bias_ref, sink_ref, o_ref):
    n = pl.program_id(1)
    keys = jnp.concatenate([kp_ref[0], kc_ref[0]], axis=0)
    vals = jnp.concatenate([vp_ref[0], vc_ref[0]], axis=0)
    col = lax.broadcasted_iota(jnp.int32, (1, 2 * WINDOW), 1)
    first_blk = jnp.where((col < WINDOW) & (n == 0), NEG, 0.0)
    n_slot = G_C // 2
    for kv in range(H_KV_C):
        k_ab = _kv_variants(keys, kv)
        v_ab = _kv_variants(vals, kv)
        for ps in range(n_slot):
            h0 = kv * G_C + 2 * ps
            sl = slice(kv * G_C * DH_C + ps * LANES, kv * G_C * DH_C + (ps + 1) * LANES)
            q = (q_ref[0, :, sl] * (DH_C ** -0.5)).astype(BF16)
            bias_ab = (bias_ref[h0] + first_blk, bias_ref[h0 + 1] + first_blk)
            o_ref[0, :, sl] = _sink_attend(q, k_ab, v_ab, bias_ab, (sink_ref[h0], sink_ref[h0 + 1]))


def swa_prompt(q, k, v, rel_bias, sinks):
    b, t, _ = q.shape
    nb = t // WINDOW
    i = np.arange(WINDOW)
    j = np.arange(2 * WINDOW)
    dist = WINDOW + i[:, None] - j[None, :]
    bias = bias_tiles(rel_bias, dist, (dist >= 0) & (dist <= WINDOW), H_C)
    cur = lambda bi, n: (bi, n, 0)
    prev = lambda bi, n: (bi, jnp.maximum(n - 1, 0), 0)
    return pl.pallas_call(
        _swa_prompt_kernel,
        grid=(b, nb),
        in_specs=[
            pl.BlockSpec((1, WINDOW, Q_C), cur),
            pl.BlockSpec((1, WINDOW, KV_C), prev), pl.BlockSpec((1, WINDOW, KV_C), cur),
            pl.BlockSpec((1, WINDOW, KV_C), prev), pl.BlockSpec((1, WINDOW, KV_C), cur),
            pl.BlockSpec((H_C, WINDOW, 2 * WINDOW), lambda bi, n: (0, 0, 0)),
            pl.BlockSpec(memory_space=pltpu.SMEM),
        ],
        out_specs=pl.BlockSpec((1, WINDOW, Q_C), cur),
        out_shape=jax.ShapeDtypeStruct((b, t, Q_C), F32),
        compiler_params=_params("parallel", "parallel"), name="swa_prompt",
    )(q, k, k, v, v, bias, sinks.astype(F32))


def _swa_sample_kernel(gs, s_len, q_ref, kn_ref, vn_ref, ck_ref, cv_ref, bias_ref, sink_ref, o_ref):
    n_slot = G_C // 2
    pad = jnp.zeros((WINDOW - s_len, LANES), F32)
    for si in range(gs):
        keys = jnp.concatenate([ck_ref[si], kn_ref[si], pad], axis=0)
        vals = jnp.concatenate([cv_ref[si], vn_ref[si], pad], axis=0)
        for kv in range(H_KV_C):
            k_ab = _kv_variants(keys, kv)
            v_ab = _kv_variants(vals, kv)
            base = kv * G_C * DH_C
            q = jnp.concatenate([q_ref[si, :, base + ps * LANES:base + (ps + 1) * LANES] for ps in range(n_slot)],
                                axis=0)
            q = (q * (DH_C ** -0.5)).astype(BF16)
            o = _sink_attend(q, k_ab, v_ab, (bias_ref[kv, 0], bias_ref[kv, 1]),
                             (sink_ref[kv, 0], sink_ref[kv, 1]))
            for ps in range(n_slot):
                o_ref[si, :, base + ps * LANES:base + (ps + 1) * LANES] = o[ps * s_len:(ps + 1) * s_len]


def swa_sample(q, k, v, cache_k, cache_v, rel_bias, sinks, gs):
    s_n, s_len, _ = q.shape
    assert s_n % gs == 0 and cache_k.shape[1] == WINDOW
    ck = cache_k.reshape(s_n, WINDOW, KV_C)
    cv = cache_v.reshape(s_n, WINDOW, KV_C)
    tq = np.arange(s_len)
    j = np.arange(2 * WINDOW)
    dist = WINDOW + tq[:, None] - j[None, :]
    valid = (dist >= 0) & (dist <= WINDOW) & (j[None, :] < WINDOW + s_len)
    n_slot = G_C // 2
    bias = bias_tiles(rel_bias, dist, valid, H_C)
    bias = bias.reshape(H_KV_C, n_slot, 2, s_len, 2 * WINDOW).transpose(0, 2, 1, 3, 4)
    bias = bias.reshape(H_KV_C, 2, n_slot * s_len, 2 * WINDOW)
    sk = sinks.astype(F32).reshape(H_KV_C, n_slot, 2).transpose(0, 2, 1)
    sk = jnp.repeat(sk, s_len, axis=-1).reshape(H_KV_C, 2, n_slot * s_len, 1)
    grp = lambda i: (i, 0, 0)
    whole = lambda shape: pl.BlockSpec(shape, lambda i: (0,) * len(shape))
    return pl.pallas_call(
        functools.partial(_swa_sample_kernel, gs, s_len),
        grid=(s_n // gs,),
        in_specs=[
            pl.BlockSpec((gs, s_len, Q_C), grp),
            pl.BlockSpec((gs, s_len, KV_C), grp), pl.BlockSpec((gs, s_len, KV_C), grp),
            pl.BlockSpec((gs, WINDOW, KV_C), grp), pl.BlockSpec((gs, WINDOW, KV_C), grp),
            whole(bias.shape), whole(sk.shape),
        ],
        out_specs=pl.BlockSpec((gs, s_len, Q_C), grp),
        out_shape=jax.ShapeDtypeStruct((s_n, s_len, Q_C), F32),
        compiler_params=_params("parallel"), name="swa_sample",
    )(q, k, v, ck, cv, bias, sk)


ROUTE_BIG = 1 << 20


def _out_route_kernel(n_in, *refs):
    h_ref = refs[0]
    o_refs = refs[1:1 + n_in]
    w_refs = refs[1 + n_in:1 + 2 * n_in]
    nw_ref, wr_ref, br_ref, hn_ref, xn_ref, rt_ref = refs[1 + 2 * n_in:]
    h = h_ref[...]
    for o_ref, w_ref in zip(o_refs, w_refs):
        h = h + jnp.dot(o_ref[...].astype(BF16), w_ref[...], preferred_element_type=F32)
    hn_ref[...] = h
    xn = h * lax.rsqrt(jnp.mean(h * h, axis=-1, keepdims=True) + EPS) * nw_ref[...]
    xn_ref[...] = xn.astype(BF16)
    logit = jnp.dot(xn, wr_ref[...], preferred_element_type=F32, precision=lax.Precision.HIGHEST) + br_ref[...]
    lane = lax.broadcasted_iota(jnp.int32, logit.shape, 1)
    gmask = lane < N_GROUPS
    gl = jnp.where(gmask, logit, NEG)
    gmax = jnp.max(gl, axis=-1, keepdims=True)
    gsel = jnp.min(jnp.where(gl == gmax, lane, ROUTE_BIG), axis=-1, keepdims=True)
    gw = 1.0 / jnp.sum(jnp.where(gmask, jnp.exp(gl - gmax), 0.0), axis=-1, keepdims=True)
    eid = lane - N_GROUPS
    emask = (eid >= 0) & (eid < N_EXPERTS) & (jnp.right_shift(eid, 3) == gsel)
    el = jnp.where(emask, logit, NEG)
    v1 = jnp.max(el, axis=-1, keepdims=True)
    i1 = jnp.min(jnp.where((el == v1) & emask, lane, ROUTE_BIG), axis=-1, keepdims=True)
    emask2 = emask & (lane != i1)
    el2 = jnp.where(emask2, logit, NEG)
    v2 = jnp.max(el2, axis=-1, keepdims=True)
    i2 = jnp.min(jnp.where((el2 == v2) & emask2, lane, ROUTE_BIG), axis=-1, keepdims=True)
    e = jnp.exp(v2 - v1)
    w1 = gw / (1.0 + e)
    w2 = gw * e / (1.0 + e)
    rt = jnp.where(lane == 0, (i1 - N_GROUPS).astype(F32),
                   jnp.where(lane == 1, (i2 - N_GROUPS).astype(F32),
                             jnp.where(lane == 2, w1, jnp.where(lane == 3, w2, 0.0))))
    rt_ref[...] = rt


def out_route(h, outs, weights_bf16, norm_w, w_group, b_group, w_router, b_router, tm=256):
    n, d = h.shape
    tm = min(tm, n)
    assert n % tm == 0 and EXP_PER_GROUP == 8
    wr = jnp.zeros((d, LANES), F32).at[:, :N_GROUPS].set(w_group).at[:, N_GROUPS:N_GROUPS + N_EXPERTS].set(w_router)
    br = jnp.zeros((1, LANES), F32).at[0, :N_GROUPS].set(b_group).at[0, N_GROUPS:N_GROUPS + N_EXPERTS].set(b_router)
    row = lambda i: (i, 0)
    fixed = lambda i: (0, 0)
    in_specs = [pl.BlockSpec((tm, d), row)]
    in_specs += [pl.BlockSpec((tm, o.shape[1]), row) for o in outs]
    in_specs += [pl.BlockSpec(w.shape, fixed) for w in weights_bf16]
    in_specs += [pl.BlockSpec((1, d), fixed), pl.BlockSpec((d, LANES), fixed), pl.BlockSpec((1, LANES), fixed)]
    return pl.pallas_call(
        functools.partial(_out_route_kernel, len(outs)),
        grid=(n // tm,), in_specs=in_specs,
        out_specs=[pl.BlockSpec((tm, d), row), pl.BlockSpec((tm, d), row), pl.BlockSpec((tm, LANES), row)],
        out_shape=[jax.ShapeDtypeStruct((n, d), F32), jax.ShapeDtypeStruct((n, d), BF16),
                   jax.ShapeDtypeStruct((n, LANES), F32)],
        compiler_params=_params("parallel"), name="out_route",
    )(h, *outs, *weights_bf16, norm_w.reshape(1, d), wr, br)


def _experts_kernel(te_ref, nv_ref, x_ref, wrow_ref, wg_ref, wu_ref, wd_ref, y_ref):
    i = pl.program_id(0)

    @pl.when(i < nv_ref[0])
    def _():
        x = x_ref[...]
        g = jnp.dot(x, wg_ref[0].astype(BF16), preferred_element_type=F32)
        u = jnp.dot(x, wu_ref[0].astype(BF16), preferred_element_type=F32)
        hh = (_silu(g) * u).astype(BF16)
        y_ref[...] = jnp.dot(hh, wd_ref[0].astype(BF16), preferred_element_type=F32) * wrow_ref[...]

    @pl.when(i >= nv_ref[0])
    def _():
        y_ref[...] = jnp.zeros(y_ref.shape, F32)


def moe_experts(xs, wrow, tile_expert, n_valid, w_gate, w_up, w_down, tm):
    p, d = xs.shape
    f = w_gate.shape[-1]
    n_tiles = p // tm
    grid_spec = pltpu.PrefetchScalarGridSpec(
        num_scalar_prefetch=2,
        grid=(n_tiles,),
        in_specs=[
            pl.BlockSpec((tm, d), lambda i, te, nv: (i, 0)),
            pl.BlockSpec((tm, 1), lambda i, te, nv: (i, 0)),
            pl.BlockSpec((1, d, f), lambda i, te, nv: (te[i], 0, 0)),
            pl.BlockSpec((1, d, f), lambda i, te, nv: (te[i], 0, 0)),
            pl.BlockSpec((1, f, d), lambda i, te, nv: (te[i], 0, 0)),
        ],
        out_specs=pl.BlockSpec((tm, d), lambda i, te, nv: (i, 0)),
    )
    return pl.pallas_call(
        _experts_kernel, grid_spec=grid_spec,
        out_shape=jax.ShapeDtypeStruct((p, d), F32),
        compiler_params=_params("arbitrary"), name="moe_experts",
    )(tile_expert, n_valid, xs, wrow, w_gate, w_up, w_down)


def hier_moe(xn_bf16, route, w_gate, w_up, w_down, part_sizes, tm=256):
    n, d = xn_bf16.shape
    e_flat = route[:, :TOP_K].astype(jnp.int32).reshape(-1)
    w_flat = route[:, TOP_K:2 * TOP_K].reshape(-1)
    n_slots = n * TOP_K
    order = jnp.argsort(e_flat, stable=True).astype(jnp.int32)
    sorted_e = e_flat[order]
    counts = jnp.zeros((N_EXPERTS,), jnp.int32).at[e_flat].add(1)
    starts = jnp.cumsum(counts) - counts
    padded = ((counts + tm - 1) // tm) * tm
    pad_ends = jnp.cumsum(padded)
    pad_starts = pad_ends - padded
    pos_sorted = pad_starts[sorted_e] + (jnp.arange(n_slots, dtype=jnp.int32) - starts[sorted_e])
    p = n_slots + N_EXPERTS * tm
    n_tiles = p // tm
    row_tok = jnp.zeros((p,), jnp.int32).at[pos_sorted].set(order // TOP_K)
    row_w = jnp.zeros((p,), F32).at[pos_sorted].set(w_flat[order])
    n_valid = (pad_ends[-1] // tm).astype(jnp.int32)
    tile_start = jnp.arange(n_tiles, dtype=jnp.int32) * tm
    tile_e = jnp.searchsorted(pad_ends, tile_start, side="right").astype(jnp.int32)
    last_e = jnp.searchsorted(pad_ends, jnp.maximum(pad_ends[-1] - 1, 0), side="right").astype(jnp.int32)
    tile_e = jnp.where(jnp.arange(n_tiles) < n_valid, tile_e, last_e)
    tile_e = jnp.minimum(tile_e, N_EXPERTS - 1)
    xs = jnp.take(xn_bf16, row_tok, axis=0)
    ys = moe_experts(xs, row_w.reshape(p, 1), tile_e, n_valid.reshape(1), w_gate, w_up, w_down, tm)
    pos_slot = jnp.zeros((n_slots,), jnp.int32).at[order].set(pos_sorted).reshape(n, TOP_K)
    parts, lo = [], 0
    for sz in part_sizes:
        parts.append(tuple(jnp.take(ys, pos_slot[lo:lo + sz, kk], axis=0) for kk in range(TOP_K)))
        lo += sz
    return parts


def _final_norm_kernel(h_ref, a_ref, b_ref, nw_ref, o_ref):
    h = h_ref[...] + a_ref[...] + b_ref[...]
    o_ref[...] = h * lax.rsqrt(jnp.mean(h * h, axis=-1, keepdims=True) + EPS) * nw_ref[...]


def final_norm(h, a, b, norm_w, tm=256):
    n, d = h.shape
    tm = min(tm, n)
    row = pl.BlockSpec((tm, d), lambda i: (i, 0))
    return pl.pallas_call(
        _final_norm_kernel, grid=(n // tm,),
        in_specs=[row, row, row, pl.BlockSpec((1, d), lambda i: (0, 0))],
        out_specs=row, out_shape=jax.ShapeDtypeStruct((n, d), F32),
        compiler_params=_params("parallel"), name="final_norm",
    )(h, a, b, norm_w.reshape(1, d))


def kernel(x_prompt, x_sample, state_a_conv, state_a_ssm, cache_b_k, cache_b_v, cache_c_k, cache_c_v, page_table, norm_mix, norm_ffn, norm_final, rel_bias, w_in0, conv_w, a_log, dt_bias, gdn_norm_w, diff_lambda, diff_subln_w, w_out0, w_in1, b_in1, sinks, w_out1, w_group, b_group, w_router, b_router, w_gate, w_up, w_down):
    bp, tp, d = x_prompt.shape
    sn, sl, _ = x_sample.shape
    n_p, n_s = bp * tp, sn * sl
    parts = ((x_prompt.reshape(n_p, d), bp, tp), (x_sample.reshape(n_s, d), sn, sl))

    off_z = CONV_CH
    off_a = off_z + H_A * DV_A
    off_qb = off_a + 2 * H_A
    w0 = jnp.concatenate([w_in0[:, :off_a], w_in0[:, off_a:off_qb],
                          jnp.zeros((d, LANES - 2 * H_A), w_in0.dtype), w_in0[:, off_qb:]], axis=1).astype(BF16)
    widths0 = (CONV_CH, H_A * DV_A, LANES, B_W, B_W, B_W)
    w_out0_bf = w_out0.astype(BF16)
    w_out0_parts = [w_out0_bf[:H_A * DV_A], w_out0_bf[H_A * DV_A:]]

    def moe_layer(layer, hs, xns, routes):
        xn_all = jnp.concatenate(xns, axis=0)
        route_all = jnp.concatenate(routes, axis=0)
        return hier_moe(xn_all, route_all, w_gate[layer], w_up[layer], w_down[layer], [h.shape[0] for h in hs])

    conv_states = (jnp.zeros((bp, CONV_W - 1, CONV_CH), F32), state_a_conv)
    ssm_states = (jnp.zeros((bp, H_A, DK_A, DV_A), F32), state_a_ssm)
    hs, xns, routes, conv_new, ssm_new, kb_new, vb_new = [], [], [], [], [], [], []
    for idx, (x2, b, t) in enumerate(parts):
        qkv, z, ab, qb, kb, vb = norm_proj(x2, [], norm_mix[0], w0, None, widths0, emit_h=False)
        r3 = lambda a: a.reshape(b, t, a.shape[-1])
        o_a, s_new = gdn_heads(r3(qkv), r3(z), r3(ab), conv_states[idx], ssm_states[idx], conv_w, a_log, dt_bias,
                               gdn_norm_w, min(GDN_CHUNK, t))
        if idx == 0:
            o_b = diff_attn_prompt(r3(qb), r3(kb), r3(vb), rel_bias, diff_lambda, diff_subln_w, 512)
        else:
            o_b = diff_attn_sample(r3(qb), r3(kb), r3(vb), cache_b_k, cache_b_v, page_table, rel_bias, diff_lambda,
                                   diff_subln_w, 8)
        h1, xn, route = out_route(x2, [o_a.reshape(b * t, -1), o_b.reshape(b * t, -1)], w_out0_parts, norm_ffn[0],
                                  w_group[0], b_group[0], w_router[0], b_router[0])
        hs.append(h1), xns.append(xn), routes.append(route)
        conv_new.append(r3(qkv)[:, t - (CONV_W - 1):, :])
        ssm_new.append(s_new)
        kb_new.append(kb.reshape(b, t, H_B, 2 * DH_B))
        vb_new.append(vb.reshape(b, t, H_B, 2 * DH_B))
    ys = moe_layer(0, hs, xns, routes)

    w1 = w_in1.astype(BF16)
    w_out1_bf = w_out1.astype(BF16)
    hs2, xns, routes, kc_new, vc_new = [], [], [], [], []
    for idx, (_, b, t) in enumerate(parts):
        h2, q, k, v = norm_proj(hs[idx], list(ys[idx]), norm_mix[1], w1, b_in1, (Q_C, KV_C, KV_C), emit_h=True)
        r3 = lambda a: a.reshape(b, t, a.shape[-1])
        if idx == 0:
            o_c = swa_prompt(r3(q), r3(k), r3(v), rel_bias, sinks)
            kc_new.append(r3(k)[:, t - WINDOW:].reshape(b, WINDOW, H_KV_C, DH_C))
            vc_new.append(r3(v)[:, t - WINDOW:].reshape(b, WINDOW, H_KV_C, DH_C))
        else:
            o_c = swa_sample(r3(q), r3(k), r3(v), cache_c_k, cache_c_v, rel_bias, sinks, 8)
            kc_new.append(jnp.concatenate([cache_c_k[:, t:], k.reshape(b, t, H_KV_C, DH_C)], axis=1))
            vc_new.append(jnp.concatenate([cache_c_v[:, t:], v.reshape(b, t, H_KV_C, DH_C)], axis=1))
        h3, xn, route = out_route(h2, [o_c.reshape(b * t, -1)], [w_out1_bf], norm_ffn[1],
                                  w_group[1], b_group[1], w_router[1], b_router[1])
        hs2.append(h3), xns.append(xn), routes.append(route)
    ys = moe_layer(1, hs2, xns, routes)

    y_out = [final_norm(hs2[idx], ys[idx][0], ys[idx][1], norm_final).reshape(b, t, d)
             for idx, (_, b, t) in enumerate(parts)]
    return (y_out[0], y_out[1], conv_new[0], conv_new[1], ssm_new[0], ssm_new[1],
            kb_new[0], vb_new[0], kb_new[1], vb_new[1], kc_new[0], vc_new[0], kc_new[1], vc_new[1])
```

```python
import functools
import math

import jax
import jax.numpy as jnp
import numpy as np
from jax import lax
from jax.experimental import pallas as pl
from jax.experimental.pallas import tpu as pltpu

F32 = jnp.float32
BF16 = jnp.bfloat16
EPS = 1e-6
NEG = -1e30
LOG2E = math.log2(math.e)
LANES = 128
VMEM_LIMIT = 56 * 1024 * 1024

D_MODEL = 1024
H_A, DK_A, DV_A, CONV_W = 4, 128, 128, 4
CONV_CH = 2 * H_A * DK_A + H_A * DV_A
GDN_CHUNK = 64
GDN_PRECISE_INV = False
H_B, DH_B = 4, 64
B_W = H_B * 2 * DH_B
PAGE = 128
H_C, H_KV_C, DH_C, WINDOW = 16, 2, 64, 128
G_C = H_C // H_KV_C
Q_C = H_C * DH_C
KV_C = H_KV_C * DH_C
N_BUCKETS, MAX_DIST = 32, 128
N_GROUPS, EXP_PER_GROUP, TOP_K, D_EXPERT = 4, 8, 2, 256
N_EXPERTS = N_GROUPS * EXP_PER_GROUP
LAM_INIT0 = 0.8 - 0.6 * math.exp(-0.3 * 0)


def _params(*sem):
    return pltpu.CompilerParams(dimension_semantics=sem, vmem_limit_bytes=VMEM_LIMIT)


def _mm(a, b, precise=False):
    if precise:
        return jnp.dot(a.astype(F32), b.astype(F32), preferred_element_type=F32, precision=lax.Precision.HIGHEST)
    return jnp.dot(a.astype(BF16), b.astype(BF16), preferred_element_type=F32)


def _mm_nt(a, b, precise=False):
    dn = (((1,), (1,)), ((), ()))
    if precise:
        return lax.dot_general(a.astype(F32), b.astype(F32), dn, preferred_element_type=F32,
                               precision=lax.Precision.HIGHEST)
    return lax.dot_general(a.astype(BF16), b.astype(BF16), dn, preferred_element_type=F32)


def _mm_tn(a, b, precise=False):
    dn = (((0,), (0,)), ((), ()))
    if precise:
        return lax.dot_general(a.astype(F32), b.astype(F32), dn, preferred_element_type=F32,
                               precision=lax.Precision.HIGHEST)
    return lax.dot_general(a.astype(BF16), b.astype(BF16), dn, preferred_element_type=F32)


def _sigmoid(x):
    return 1.0 / (1.0 + jnp.exp(-x))


def _silu(x):
    return x * _sigmoid(x)


def _softplus(x):
    return jnp.maximum(x, 0.0) + jnp.log(1.0 + jnp.exp(-jnp.abs(x)))


def _add_expert_outputs(h, route_ref, y_refs):
    rt = route_ref[...]
    for kk, y_ref in enumerate(y_refs):
        h = h + rt[:, TOP_K + kk:TOP_K + kk + 1] * y_ref[...]
    return h


def _norm_proj_kernel(n_add, widths, has_bias, emit_h, *refs):
    x_ref = refs[0]
    add_refs = refs[1:1 + n_add]
    nw_ref, w_ref = refs[1 + n_add], refs[2 + n_add]
    pos = 3 + n_add
    b_ref = None
    if has_bias:
        b_ref = refs[pos]
        pos += 1
    outs = refs[pos:]
    h = x_ref[...]
    if n_add:
        h = _add_expert_outputs(h, add_refs[0], add_refs[1:])
    if emit_h:
        outs[0][...] = h
        outs = outs[1:]
    xn = h * lax.rsqrt(jnp.mean(h * h, axis=-1, keepdims=True) + EPS) * nw_ref[...]
    xb = xn.astype(BF16)
    off = 0
    for o_ref, wd in zip(outs, widths):
        y = jnp.dot(xb, w_ref[:, off:off + wd], preferred_element_type=F32)
        if has_bias:
            y = y + b_ref[:, off:off + wd]
        o_ref[...] = y
        off += wd


def norm_proj(x, addends, norm_w, w_bf16, bias, widths, emit_h, tm=256):
    n, d = x.shape
    m = w_bf16.shape[1]
    tm = min(tm, n)
    assert sum(widths) == m and n % tm == 0
    row = lambda i: (i, 0)
    fixed = lambda i: (0, 0)
    in_specs = [pl.BlockSpec((tm, d), row)] + [pl.BlockSpec((tm, a.shape[1]), row) for a in addends]
    in_specs += [pl.BlockSpec((1, d), fixed), pl.BlockSpec((d, m), fixed)]
    args = [x, *addends, norm_w.reshape(1, d), w_bf16]
    if bias is not None:
        in_specs.append(pl.BlockSpec((1, m), fixed))
        args.append(bias.reshape(1, m))
    out_shape, out_specs = [], []
    if emit_h:
        out_shape.append(jax.ShapeDtypeStruct((n, d), F32))
        out_specs.append(pl.BlockSpec((tm, d), row))
    for wd in widths:
        out_shape.append(jax.ShapeDtypeStruct((n, wd), F32))
        out_specs.append(pl.BlockSpec((tm, wd), row))
    return pl.pallas_call(
        functools.partial(_norm_proj_kernel, len(addends), tuple(widths), bias is not None, emit_h),
        grid=(n // tm,), in_specs=in_specs, out_specs=out_specs, out_shape=out_shape,
        compiler_params=_params("parallel"), name="norm_proj",
    )(*args)


def _gdn_kernel(c_len, nb, qkv_ref, z_ref, ab_ref, cbuf_ref, s0_ref, convw_ref, gpar_ref, nw_ref,
                o_ref, sfin_ref, xp_scr, s_scr):
    c = pl.program_id(1)
    n_c = pl.num_programs(1)
    hist = CONV_W - 1
    base = 8 - hist

    @pl.when(c == 0)
    def _():
        xp_scr[:, base:8, :] = cbuf_ref[...]
        s_scr[...] = s0_ref[...]

    ri = lax.broadcasted_iota(jnp.int32, (c_len, c_len), 0)
    ci = lax.broadcasted_iota(jnp.int32, (c_len, c_len), 1)
    lower = ri >= ci
    strict = ri > ci
    eye = (ri == ci).astype(F32)
    lower_f = lower.astype(F32)

    units = []
    for bi in range(nb):
        xp_scr[bi, 8:8 + c_len, :] = qkv_ref[bi]
        y = xp_scr[bi, base:base + c_len, :] * convw_ref[0:1, :]
        for j in range(1, CONV_W):
            y = y + xp_scr[bi, base + j:base + j + c_len, :] * convw_ref[j:j + 1, :]
        xp_scr[bi, base:8, :] = xp_scr[bi, base + c_len:8 + c_len, :]
        y = _silu(y)

        ab = ab_ref[bi]
        g_t = gpar_ref[0:1, :] * _softplus(ab + gpar_ref[1:2, :])
        beta_t = _sigmoid(ab)
        gcum_t = _mm(lower_f, g_t, precise=True)
        gcum_tt = gcum_t.T

        for h in range(H_A):
            q = y[:, h * DK_A:(h + 1) * DK_A]
            k = y[:, (H_A + h) * DK_A:(H_A + h + 1) * DK_A]
            v = y[:, 2 * H_A * DK_A + h * DV_A:2 * H_A * DK_A + (h + 1) * DV_A]
            q = q * lax.rsqrt(jnp.sum(q * q, axis=-1, keepdims=True) + EPS) * (DK_A ** -0.5)
            k = k * lax.rsqrt(jnp.sum(k * k, axis=-1, keepdims=True) + EPS)
            gc = gcum_t[:, h:h + 1]
            gr = gcum_tt[h:h + 1, :]
            beta = beta_t[:, H_A + h:H_A + h + 1]
            decay = jnp.where(lower, jnp.exp(jnp.where(lower, gc - gr, 0.0)), 0.0)
            kb = k * beta
            units.append(dict(bi=bi, h=h, q=q, k=k, k_bf=k.astype(BF16), gc=gc, decay=decay, kb=kb,
                              rhs=jnp.concatenate([v * beta, kb * jnp.exp(gc)], axis=1)))

    for un in units:
        un["nmat"] = jnp.where(strict, _mm_nt(un["kb"], un["k_bf"]) * un["decay"], 0.0)
        un["qk"] = jnp.where(lower, _mm_nt(un["q"], un["k_bf"]) * un["decay"], 0.0)
    for un in units:
        un["inv"] = eye - un["nmat"]
        un["pw"] = _mm(un["nmat"], un["nmat"], GDN_PRECISE_INV)
    span = 2
    while span < c_len:
        for un in units:
            un["inv"] = un["inv"] + _mm(un["inv"], un["pw"], GDN_PRECISE_INV)
        span *= 2
        if span < c_len:
            for un in units:
                un["pw"] = _mm(un["pw"], un["pw"], GDN_PRECISE_INV)
    for un in units:
        sol = _mm(un["inv"], un["rhs"], GDN_PRECISE_INV)
        un["u"], un["w"] = sol[:, :DV_A], sol[:, DV_A:]
        un["s"] = s_scr[un["bi"], un["h"]]
        un["s_bf"] = un["s"].astype(BF16)
    for un in units:
        un["v_new"] = un["u"] - _mm(un["w"], un["s_bf"])
        un["o_s"] = _mm(un["q"] * jnp.exp(un["gc"]), un["s_bf"])
    for un in units:
        bi, h, gc = un["bi"], un["h"], un["gc"]
        o = un["o_s"] + _mm(un["qk"], un["v_new"])
        g_last = gc[c_len - 1:c_len, :]
        s_scr[bi, h] = un["s"] * jnp.exp(g_last) + _mm_tn(un["k"] * jnp.exp(g_last - gc), un["v_new"])
        o = o * lax.rsqrt(jnp.mean(o * o, axis=-1, keepdims=True) + EPS) * nw_ref[...]
        o_ref[bi, :, h * DV_A:(h + 1) * DV_A] = o * _silu(z_ref[bi, :, h * DV_A:(h + 1) * DV_A])

    @pl.when(c == n_c - 1)
    def _():
        sfin_ref[...] = s_scr[...]


def gdn_heads(qkv, z, ab, conv_buf, s0, conv_w, a_log, dt_bias, norm_w, c_len, nb):
    b, t, _ = qkv.shape
    assert t % c_len == 0 and b % nb == 0
    convw_t = conv_w.T
    gpar = jnp.zeros((2, LANES), F32)
    gpar = gpar.at[0, :H_A].set(-jnp.exp(a_log.astype(F32))).at[1, :H_A].set(dt_bias.astype(F32))
    tok = lambda i, j: (i, j, 0)
    seq3 = lambda i, j: (i, 0, 0)
    fixed = lambda i, j: (0, 0)
    return pl.pallas_call(
        functools.partial(_gdn_kernel, c_len, nb),
        grid=(b // nb, t // c_len),
        in_specs=[
            pl.BlockSpec((nb, c_len, CONV_CH), tok),
            pl.BlockSpec((nb, c_len, H_A * DV_A), tok),
            pl.BlockSpec((nb, c_len, LANES), tok),
            pl.BlockSpec((nb, CONV_W - 1, CONV_CH), seq3),
            pl.BlockSpec((nb, H_A, DK_A, DV_A), lambda i, j: (i, 0, 0, 0)),
            pl.BlockSpec((CONV_W, CONV_CH), fixed),
            pl.BlockSpec((2, LANES), fixed),
            pl.BlockSpec((1, DV_A), fixed),
        ],
        out_specs=[
            pl.BlockSpec((nb, c_len, H_A * DV_A), tok),
            pl.BlockSpec((nb, H_A, DK_A, DV_A), lambda i, j: (i, 0, 0, 0)),
        ],
        out_shape=[
            jax.ShapeDtypeStruct((b, t, H_A * DV_A), F32),
            jax.ShapeDtypeStruct((b, H_A, DK_A, DV_A), F32),
        ],
        scratch_shapes=[
            pltpu.VMEM((nb, 8 + c_len, CONV_CH), F32),
            pltpu.VMEM((nb, H_A, DK_A, DV_A), F32),
        ],
        compiler_params=_params("parallel", "arbitrary"), name="gdn_heads",
    )(qkv, z, ab, conv_buf, s0, convw_t, gpar, norm_w.reshape(1, DV_A))


def _t5_bucket_np(dist):
    d = np.maximum(dist, 0)
    max_exact = N_BUCKETS // 2
    ratio = (np.log(np.maximum(d, 1).astype(np.float32) / np.float32(max_exact))
             / np.float32(math.log(MAX_DIST / max_exact)))
    large = np.minimum(max_exact + (ratio * (N_BUCKETS - max_exact)).astype(np.int32), N_BUCKETS - 1)
    return np.where(d < max_exact, d, large).astype(np.int32)


FAR_BUCKET = int(_t5_bucket_np(np.array([MAX_DIST]))[0])
assert np.all(_t5_bucket_np(np.arange(MAX_DIST, 4 * MAX_DIST)) == FAR_BUCKET)


def _bias_kernel(n_cols, bucket_ref, rb_ref, out_ref):
    bk = bucket_ref[...]
    for col in range(n_cols):
        acc = jnp.full(bk.shape, NEG, F32)
        for b in range(N_BUCKETS):
            acc = jnp.where(bk == b, rb_ref[b, col], acc)
        out_ref[col] = acc


def bias_tiles(rel_bias, dist, valid, n_cols):
    bucket = np.where(valid, _t5_bucket_np(dist), -1).astype(np.int32)
    r, c = bucket.shape
    return pl.pallas_call(
        functools.partial(_bias_kernel, n_cols),
        in_specs=[pl.BlockSpec(memory_space=pltpu.VMEM), pl.BlockSpec(memory_space=pltpu.SMEM)],
        out_specs=pl.BlockSpec(memory_space=pltpu.VMEM),
        out_shape=jax.ShapeDtypeStruct((n_cols, r, c), F32),
        name="bias_tiles",
    )(jnp.asarray(bucket), rel_bias.astype(F32))


def _diff_lambda(lam_ref):
    lp = lam_ref[...]
    s1 = jnp.sum(lp[0:1] * lp[1:2], axis=-1, keepdims=True)
    s2 = jnp.sum(lp[2:3] * lp[3:4], axis=-1, keepdims=True)
    return jnp.exp(s1) - jnp.exp(s2) + LAM_INIT0


def _diff_finish(o1, o2, lam, nw):
    o = o1 - lam * o2
    return o * lax.rsqrt(jnp.mean(o * o, axis=-1, keepdims=True) + EPS) * nw * (1.0 - LAM_INIT0)


def _softmax_update(s, v_bf, m_ref, l_ref, acc_ref):
    m_prev = m_ref[...]
    m_new = jnp.maximum(m_prev, jnp.max(s, axis=-1, keepdims=True))
    alpha = jnp.exp(m_prev - m_new)
    p = jnp.exp(s - m_new)
    l_ref[...] = alpha * l_ref[...] + jnp.sum(p, axis=-1, keepdims=True)
    acc_ref[...] = alpha * acc_ref[...] + jnp.dot(p.astype(BF16), v_bf, preferred_element_type=F32)
    m_ref[...] = m_new


def _diff_prompt_kernel(blk, q_ref, k_ref, v_ref, t0_ref, t1_ref, far_ref, lam_ref, nw_ref, o_ref,
                        m_scr, acc_scr):
    qi = pl.program_id(1)
    ki = pl.program_id(2)
    nsub = blk // LANES
    hw = 2 * DH_B

    @pl.when(ki == 0)
    def _():
        m_scr[...] = jnp.full(m_scr.shape, NEG, F32)
        acc_scr[...] = jnp.zeros(acc_scr.shape, F32)

    def bias_block(case, hm):
        const = jnp.full((LANES, LANES), far_ref[hm], F32)
        neg = jnp.full((LANES, LANES), NEG, F32)
        rows = []
        for jj in range(nsub):
            tiles = []
            for ii in range(nsub):
                sub = ii - jj + (nsub if case == "prev" else 0)
                tiles.append(neg if sub < 0 else t0_ref[hm] if sub == 0 else t1_ref[hm] if sub == 1 else const)
            rows.append(jnp.concatenate(tiles, axis=1))
        return jnp.concatenate(rows, axis=0)

    def step(case):
        q = (q_ref[0] * (DH_B ** -0.5 * LOG2E)).astype(BF16)
        k = k_ref[0].astype(BF16)
        v_t = v_ref[0].T
        first = lax.broadcasted_iota(jnp.int32, (1, hw), 1) < DH_B
        ones = jnp.ones((hw, blk), F32)
        s_all, v_ext = [], []
        for h in range(H_B):
            sl = slice(h * hw, (h + 1) * hw)
            qh, kh = q[:, sl], k[:, sl]
            v_ext.append(jnp.concatenate([v_t[sl], ones], axis=0).astype(BF16))
            for mp in range(2):
                km = jnp.where(first if mp == 0 else jnp.logical_not(first), kh, jnp.zeros_like(kh))
                s_all.append(lax.dot_general(km, qh, (((1,), (1,)), ((), ())),
                                             preferred_element_type=F32))
        p_all, alpha_all = [], []
        for hm, s in enumerate(s_all):
            m_prev = m_scr[hm]
            if case == "far":
                shift = far_ref[hm]
                m_new = jnp.maximum(m_prev, jnp.max(s, axis=0, keepdims=True) + shift)
                p = jnp.exp2(s + (shift - m_new))
            else:
                s = s + bias_block(case, hm)
                m_new = jnp.maximum(m_prev, jnp.max(s, axis=0, keepdims=True))
                p = jnp.exp2(s - m_new)
            alpha_all.append(jnp.exp2(m_prev - m_new))
            p_all.append(p.astype(BF16))
            m_scr[hm] = m_new
        for hm, (p, alpha) in enumerate(zip(p_all, alpha_all)):
            acc_scr[hm] = alpha * acc_scr[hm] + jnp.dot(v_ext[hm // 2], p, preferred_element_type=F32)

    @pl.when(ki < qi - 1)
    def _():
        step("far")

    @pl.when(ki == qi - 1)
    def _():
        step("prev")

    @pl.when(ki == qi)
    def _():
        step("diag")
        lam = _diff_lambda(lam_ref)
        for h in range(H_B):
            a1, a2 = acc_scr[2 * h], acc_scr[2 * h + 1]
            o = a1[:hw] / a1[hw:hw + 1] - lam * (a2[:hw] / a2[hw:hw + 1])
            o = o * lax.rsqrt(jnp.mean(o * o, axis=0, keepdims=True) + EPS) * nw_ref[...] * (1.0 - LAM_INIT0)
            o_ref[0, :, h * hw:(h + 1) * hw] = o.T


def diff_attn_prompt(q, k, v, rel_bias, diff_lambda, subln_w, blk):
    b, t, _ = q.shape
    assert t % blk == 0 and blk % LANES == 0 and blk >= 2 * LANES
    i = np.arange(LANES)
    d0 = i[None, :] - i[:, None]
    t0 = bias_tiles(rel_bias, d0, d0 >= 0, 2 * H_B) * LOG2E
    t1 = bias_tiles(rel_bias, d0 + LANES, np.ones_like(d0, bool), 2 * H_B) * LOG2E
    far = rel_bias[FAR_BUCKET, :2 * H_B].astype(F32) * LOG2E
    nb = t // blk
    hw = 2 * DH_B
    whole = lambda shape: pl.BlockSpec(shape, lambda bi, qi, ki: (0,) * len(shape))
    return pl.pallas_call(
        functools.partial(_diff_prompt_kernel, blk),
        grid=(b, nb, nb),
        in_specs=[
            pl.BlockSpec((1, blk, B_W), lambda bi, qi, ki: (bi, qi, 0)),
            pl.BlockSpec((1, blk, B_W), lambda bi, qi, ki: (bi, jnp.minimum(ki, qi), 0)),
            pl.BlockSpec((1, blk, B_W), lambda bi, qi, ki: (bi, jnp.minimum(ki, qi), 0)),
            whole((2 * H_B, LANES, LANES)),
            whole((2 * H_B, LANES, LANES)),
            pl.BlockSpec(memory_space=pltpu.SMEM),
            whole((4, DH_B)),
            whole((hw, 1)),
        ],
        out_specs=pl.BlockSpec((1, blk, B_W), lambda bi, qi, ki: (bi, qi, 0)),
        out_shape=jax.ShapeDtypeStruct((b, t, B_W), F32),
        scratch_shapes=[
            pltpu.VMEM((2 * H_B, 1, blk), F32),
            pltpu.VMEM((2 * H_B, 2 * hw, blk), F32),
        ],
        compiler_params=_params("parallel", "parallel", "arbitrary"), name="diff_attn_prompt",
    )(q, k, v, t0, t1, far, diff_lambda.astype(F32), subln_w.reshape(hw, 1).astype(F32))


def _diff_sample_kernel(n_pg, s_len, pt_ref, q_ref, kn_ref, vn_ref, *refs):
    k_pages = refs[:n_pg]
    v_pages = refs[n_pg:2 * n_pg]
    blast_ref, bnew_ref, far_ref, lam_ref, nw_ref, o_ref, m_scr, l_scr, acc_scr = refs[2 * n_pg:]
    j = pl.program_id(1)
    last = pl.num_programs(1) - 1
    hw = 2 * DH_B

    @pl.when(j == 0)
    def _():
        m_scr[...] = jnp.full(m_scr.shape, NEG, F32)
        l_scr[...] = jnp.zeros(l_scr.shape, F32)
        acc_scr[...] = jnp.zeros(acc_scr.shape, F32)

    q = q_ref[0] * (DH_B ** -0.5)
    first = lax.broadcasted_iota(jnp.int32, (1, hw), 1) < DH_B
    pieces = []
    for h in range(H_B):
        qh = q[:, h * hw:(h + 1) * hw]
        pieces += [jnp.where(first, qh, 0.0), jnp.where(first, 0.0, qh)]
    qbd = jnp.concatenate(pieces, axis=0).astype(BF16)

    kcat = jnp.concatenate([r[0] for r in k_pages], axis=0).astype(BF16)
    vcat = jnp.concatenate([r[0] for r in v_pages], axis=0).astype(BF16)
    s = lax.dot_general(qbd, kcat, (((1,), (1,)), ((), ())), preferred_element_type=F32)
    s = s + jnp.where(j == last, blast_ref[...], far_ref[...])
    _softmax_update(s, vcat, m_scr, l_scr, acc_scr)

    @pl.when(j == last)
    def _():
        pad = jnp.zeros((PAGE - H_B * s_len, hw), F32)
        kn = jnp.concatenate([kn_ref[0, :, h * hw:(h + 1) * hw] for h in range(H_B)] + [pad], axis=0)
        vn = jnp.concatenate([vn_ref[0, :, h * hw:(h + 1) * hw] for h in range(H_B)] + [pad], axis=0)
        s2 = lax.dot_general(qbd, kn.astype(BF16), (((1,), (1,)), ((), ())), preferred_element_type=F32)
        _softmax_update(s2 + bnew_ref[...], vn.astype(BF16), m_scr, l_scr, acc_scr)
        accn = acc_scr[...] / l_scr[...]
        lam = _diff_lambda(lam_ref)
        for h in range(H_B):
            o1 = accn[(2 * h) * s_len:(2 * h + 1) * s_len]
            o2 = accn[(2 * h + 1) * s_len:(2 * h + 2) * s_len]
            o_ref[0, :, h * hw:(h + 1) * hw] = _diff_finish(o1, o2, lam, nw_ref[...])


def diff_attn_sample(q, k, v, cache_k, cache_v, page_table, rel_bias, diff_lambda, subln_w, n_pg):
    s_n, s_len, _ = q.shape
    pages_per_seq = page_table.shape[1]
    assert pages_per_seq % n_pg == 0
    past = pages_per_seq * PAGE
    n_phys = cache_k.shape[0]
    hw = 2 * DH_B
    ck = cache_k.reshape(n_phys, PAGE * H_B, hw)
    cv = cache_v.reshape(n_phys, PAGE * H_B, hw)
    n_rows = 2 * H_B * s_len
    blk_keys = n_pg * PAGE
    blk_rows = blk_keys * H_B
    assert PAGE >= H_B * s_len
    t = np.arange(s_len)
    same_head = np.eye(H_B, dtype=bool)
    d_last = (past + t)[:, None] - (past - blk_keys + np.arange(blk_keys))[None, :]
    blast = bias_tiles(rel_bias, d_last, np.ones_like(d_last, bool), 2 * H_B)
    blast = jnp.where(same_head[:, None, None, None, :], blast.reshape(H_B, 2, s_len, blk_keys, 1), NEG)
    blast = blast.reshape(n_rows, blk_rows)
    far = rel_bias[FAR_BUCKET, :2 * H_B].astype(F32).reshape(H_B, 2, 1, 1, 1)
    far = jnp.broadcast_to(jnp.where(same_head[:, None, None, None, :], far, NEG),
                           (H_B, 2, s_len, blk_keys, H_B)).reshape(n_rows, blk_rows)
    cnew = np.arange(PAGE)
    d_new = t[:, None] - cnew[None, :]
    bnew = bias_tiles(rel_bias, d_new, (d_new >= 0) & (cnew[None, :] < s_len), 2 * H_B)[:, :, :s_len]
    bnew = jnp.where(same_head[:, None, None, :, None], bnew.reshape(H_B, 2, s_len, 1, s_len), NEG)
    bnew = jnp.concatenate([bnew.reshape(n_rows, H_B * s_len),
                            jnp.full((n_rows, PAGE - H_B * s_len), NEG, F32)], axis=1)
    assert past - blk_keys + s_len - 1 >= 0 and np.all(_t5_bucket_np(np.array([blk_keys + 1])) == FAR_BUCKET)

    def page_spec(i):
        return pl.BlockSpec((1, PAGE * H_B, hw), lambda s, j, pt: (pt[s * pages_per_seq + j * n_pg + i], 0, 0))

    seq = pl.BlockSpec((1, s_len, B_W), lambda s, j, pt: (s, 0, 0))
    whole = lambda shape: pl.BlockSpec(shape, lambda s, j, pt: (0,) * len(shape))
    grid_spec = pltpu.PrefetchScalarGridSpec(
        num_scalar_prefetch=1,
        grid=(s_n, pages_per_seq // n_pg),
        in_specs=[seq, seq, seq] + [page_spec(i) for i in range(n_pg)] * 2 + [
            whole((n_rows, blk_rows)), whole((n_rows, PAGE)), whole((n_rows, blk_rows)),
            whole((4, DH_B)), whole((1, hw))],
        out_specs=seq,
        scratch_shapes=[
            pltpu.VMEM((n_rows, 1), F32),
            pltpu.VMEM((n_rows, 1), F32),
            pltpu.VMEM((n_rows, hw), F32),
        ],
    )
    return pl.pallas_call(
        functools.partial(_diff_sample_kernel, n_pg, s_len),
        grid_spec=grid_spec,
        out_shape=jax.ShapeDtypeStruct((s_n, s_len, B_W), F32),
        compiler_params=_params("parallel", "arbitrary"), name="diff_attn_sample",
    )(page_table.reshape(-1).astype(jnp.int32), q, k, v, *([ck] * n_pg), *([cv] * n_pg),
      blast, bnew, far, diff_lambda.astype(F32), subln_w.reshape(1, 2 * DH_B).astype(F32))


def _kv_variants(x, kv):
    lo = lax.broadcasted_iota(jnp.int32, (1, LANES), 1) < DH_C
    rolled = pltpu.roll(x, DH_C, 1)
    a_src, b_src = (x, rolled) if kv == 0 else (rolled, x)
    zero = jnp.zeros_like(x)
    return jnp.where(lo, a_src, zero).astype(BF16), jnp.where(lo, zero, b_src).astype(BF16)


def _sink_attend(units):
    scores = [[lax.dot_general(q_bf, kx, (((1,), (1,)), ((), ())), preferred_element_type=F32) + bias
               for kx, bias in zip(k_ab, bias_ab)] for q_bf, k_ab, _, bias_ab, _ in units]
    probs = []
    for s_ab, (_, _, _, _, sink_ab) in zip(scores, units):
        p_ab = []
        for s, sink in zip(s_ab, sink_ab):
            m = jnp.maximum(jnp.max(s, axis=-1, keepdims=True), sink)
            e = jnp.exp(s - m)
            p_ab.append((e / (jnp.sum(e, axis=-1, keepdims=True) + jnp.exp(sink - m))).astype(BF16))
        probs.append(p_ab)
    return [jnp.dot(p_ab[0], v_ab[0], preferred_element_type=F32) + jnp.dot(p_ab[1], v_ab[1], preferred_element_type=F32)
            for p_ab, (_, _, v_ab, _, _) in zip(probs, units)]


def _swa_prompt_kernel(q_ref, kp_ref, kc_ref, vp_ref, vc_ref, bias_ref, sink_ref, o_ref):
    n = pl.program_id(1)
    keys = jnp.concatenate([kp_ref[0], kc_ref[0]], axis=0)
    vals = jnp.concatenate([vp_ref[0], vc_ref[0]], axis=0)
    col = lax.broadcasted_iota(jnp.int32, (1, 2 * WINDOW), 1)
    first_blk = jnp.where((col < WINDOW) & (n == 0), NEG, 0.0)
    n_slot = G_C // 2
    units, slices = [], []
    for kv in range(H_KV_C):
        k_ab = _kv_variants(keys, kv)
        v_ab = _kv_variants(vals, kv)
        for ps in range(n_slot):
            h0 = kv * G_C + 2 * ps
            sl = slice(kv * G_C * DH_C + ps * LANES, kv * G_C * DH_C + (ps + 1) * LANES)
            q = (q_ref[0, :, sl] * (DH_C ** -0.5)).astype(BF16)
            bias_ab = (bias_ref[h0] + first_blk, bias_ref[h0 + 1] + first_blk)
            units.append((q, k_ab, v_ab, bias_ab, (sink_ref[h0], sink_ref[h0 + 1])))
            slices.append(sl)
    for sl, o in zip(slices, _sink_attend(units)):
        o_ref[0, :, sl] = o


def swa_prompt(q, k, v, rel_bias, sinks):
    b, t, _ = q.shape
    nb = t // WINDOW
    i = np.arange(WINDOW)
    j = np.arange(2 * WINDOW)
    dist = WINDOW + i[:, None] - j[None, :]
    bias = bias_tiles(rel_bias, dist, (dist >= 0) & (dist <= WINDOW), H_C)
    cur = lambda bi, n: (bi, n, 0)
    prev = lambda bi, n: (bi, jnp.maximum(n - 1, 0), 0)
    return pl.pallas_call(
        _swa_prompt_kernel,
        grid=(b, nb),
        in_specs=[
            pl.BlockSpec((1, WINDOW, Q_C), cur),
            pl.BlockSpec((1, WINDOW, KV_C), prev), pl.BlockSpec((1, WINDOW, KV_C), cur),
            pl.BlockSpec((1, WINDOW, KV_C), prev), pl.BlockSpec((1, WINDOW, KV_C), cur),
            pl.BlockSpec((H_C, WINDOW, 2 * WINDOW), lambda bi, n: (0, 0, 0)),
            pl.BlockSpec(memory_space=pltpu.SMEM),
        ],
        out_specs=pl.BlockSpec((1, WINDOW, Q_C), cur),
        out_shape=jax.ShapeDtypeStruct((b, t, Q_C), F32),
        compiler_params=_params("parallel", "parallel"), name="swa_prompt",
    )(q, k, k, v, v, bias, sinks.astype(F32))


def _swa_sample_kernel(gs, s_len, q_ref, kn_ref, vn_ref, ck_ref, cv_ref, bias_ref, sink_ref, o_ref):
    n_slot = G_C // 2
    pad = jnp.zeros((WINDOW - s_len, LANES), F32)
    units, where = [], []
    for si in range(gs):
        keys = jnp.concatenate([ck_ref[si], kn_ref[si], pad], axis=0)
        vals = jnp.concatenate([cv_ref[si], vn_ref[si], pad], axis=0)
        for kv in range(H_KV_C):
            k_ab = _kv_variants(keys, kv)
            v_ab = _kv_variants(vals, kv)
            base = kv * G_C * DH_C
            q = jnp.concatenate([q_ref[si, :, base + ps * LANES:base + (ps + 1) * LANES] for ps in range(n_slot)],
                                axis=0)
            q = (q * (DH_C ** -0.5)).astype(BF16)
            units.append((q, k_ab, v_ab, (bias_ref[kv, 0], bias_ref[kv, 1]), (sink_ref[kv, 0], sink_ref[kv, 1])))
            where.append((si, base))
    for (si, base), o in zip(where, _sink_attend(units)):
        for ps in range(n_slot):
            o_ref[si, :, base + ps * LANES:base + (ps + 1) * LANES] = o[ps * s_len:(ps + 1) * s_len]


def swa_sample(q, k, v, cache_k, cache_v, rel_bias, sinks, gs):
    s_n, s_len, _ = q.shape
    assert s_n % gs == 0 and cache_k.shape[1] == WINDOW
    ck = cache_k.reshape(s_n, WINDOW, KV_C)
    cv = cache_v.reshape(s_n, WINDOW, KV_C)
    tq = np.arange(s_len)
    j = np.arange(2 * WINDOW)
    dist = WINDOW + tq[:, None] - j[None, :]
    valid = (dist >= 0) & (dist <= WINDOW) & (j[None, :] < WINDOW + s_len)
    n_slot = G_C // 2
    bias = bias_tiles(rel_bias, dist, valid, H_C)
    bias = bias.reshape(H_KV_C, n_slot, 2, s_len, 2 * WINDOW).transpose(0, 2, 1, 3, 4)
    bias = bias.reshape(H_KV_C, 2, n_slot * s_len, 2 * WINDOW)
    sk = sinks.astype(F32).reshape(H_KV_C, n_slot, 2).transpose(0, 2, 1)
    sk = jnp.repeat(sk, s_len, axis=-1).reshape(H_KV_C, 2, n_slot * s_len, 1)
    grp = lambda i: (i, 0, 0)
    whole = lambda shape: pl.BlockSpec(shape, lambda i: (0,) * len(shape))
    return pl.pallas_call(
        functools.partial(_swa_sample_kernel, gs, s_len),
        grid=(s_n // gs,),
        in_specs=[
            pl.BlockSpec((gs, s_len, Q_C), grp),
            pl.BlockSpec((gs, s_len, KV_C), grp), pl.BlockSpec((gs, s_len, KV_C), grp),
            pl.BlockSpec((gs, WINDOW, KV_C), grp), pl.BlockSpec((gs, WINDOW, KV_C), grp),
            whole(bias.shape), whole(sk.shape),
        ],
        out_specs=pl.BlockSpec((gs, s_len, Q_C), grp),
        out_shape=jax.ShapeDtypeStruct((s_n, s_len, Q_C), F32),
        compiler_params=_params("parallel"), name="swa_sample",
    )(q, k, v, ck, cv, bias, sk)


ROUTE_BIG = 1 << 20


def _out_route_kernel(n_in, *refs):
    h_ref = refs[0]
    o_refs = refs[1:1 + n_in]
    w_refs = refs[1 + n_in:1 + 2 * n_in]
    nw_ref, wr_ref, br_ref, cin_ref, hn_ref, xn_ref, rt_ref, cout_ref, cnt_scr = refs[1 + 2 * n_in:]
    step = pl.program_id(0)

    @pl.when(step == 0)
    def _():
        cnt_scr[...] = cin_ref[...]

    h = h_ref[...]
    for o_ref, w_ref in zip(o_refs, w_refs):
        h = h + jnp.dot(o_ref[...].astype(BF16), w_ref[...], preferred_element_type=F32)
    hn_ref[...] = h
    xn = h * lax.rsqrt(jnp.mean(h * h, axis=-1, keepdims=True) + EPS) * nw_ref[...]
    xn_ref[...] = xn
    logit = jnp.dot(xn, wr_ref[...], preferred_element_type=F32, precision=lax.Precision.HIGHEST) + br_ref[...]
    lane = lax.broadcasted_iota(jnp.int32, logit.shape, 1)
    gmask = lane < N_GROUPS
    gl = jnp.where(gmask, logit, NEG)
    gmax = jnp.max(gl, axis=-1, keepdims=True)
    gsel = jnp.min(jnp.where(gl == gmax, lane, ROUTE_BIG), axis=-1, keepdims=True)
    gw = 1.0 / jnp.sum(jnp.where(gmask, jnp.exp(gl - gmax), 0.0), axis=-1, keepdims=True)
    eid = lane - N_GROUPS
    emask = (eid >= 0) & (eid < N_EXPERTS) & (jnp.right_shift(eid, 3) == gsel)
    el = jnp.where(emask, logit, NEG)
    v1 = jnp.max(el, axis=-1, keepdims=True)
    i1 = jnp.min(jnp.where((el == v1) & emask, lane, ROUTE_BIG), axis=-1, keepdims=True)
    emask2 = emask & (lane != i1)
    el2 = jnp.where(emask2, logit, NEG)
    v2 = jnp.max(el2, axis=-1, keepdims=True)
    i2 = jnp.min(jnp.where((el2 == v2) & emask2, lane, ROUTE_BIG), axis=-1, keepdims=True)
    e = jnp.exp(v2 - v1)
    w1 = gw / (1.0 + e)
    w2 = gw * e / (1.0 + e)
    tm = logit.shape[0]
    onehot = ((lane == i1) | (lane == i2)).astype(BF16)
    ri = lax.broadcasted_iota(jnp.int32, (tm, tm), 0)
    ci = lax.broadcasted_iota(jnp.int32, (tm, tm), 1)
    before = jnp.dot((ri > ci).astype(BF16), onehot, preferred_element_type=F32) + cnt_scr[...]
    r1 = jnp.sum(jnp.where(lane == i1, before, 0.0), axis=-1, keepdims=True)
    r2 = jnp.sum(jnp.where(lane == i2, before, 0.0), axis=-1, keepdims=True)
    cnt_scr[...] = cnt_scr[...] + jnp.sum(onehot.astype(F32), axis=0, keepdims=True)
    rt = jnp.where(lane == 0, (i1 - N_GROUPS).astype(F32),
                   jnp.where(lane == 1, (i2 - N_GROUPS).astype(F32),
                             jnp.where(lane == 2, w1, jnp.where(lane == 3, w2,
                                                                jnp.where(lane == 4, r1,
                                                                          jnp.where(lane == 5, r2, 0.0))))))
    rt_ref[...] = rt

    @pl.when(step == pl.num_programs(0) - 1)
    def _():
        cout_ref[...] = cnt_scr[...]


def out_route(h, outs, weights_bf16, norm_w, w_group, b_group, w_router, b_router, counts_in, tm=256):
    n, d = h.shape
    tm = min(tm, n)
    assert n % tm == 0 and EXP_PER_GROUP == 8
    wr = jnp.zeros((d, LANES), F32).at[:, :N_GROUPS].set(w_group).at[:, N_GROUPS:N_GROUPS + N_EXPERTS].set(w_router)
    br = jnp.zeros((1, LANES), F32).at[0, :N_GROUPS].set(b_group).at[0, N_GROUPS:N_GROUPS + N_EXPERTS].set(b_router)
    row = lambda i: (i, 0)
    fixed = lambda i: (0, 0)
    in_specs = [pl.BlockSpec((tm, d), row)]
    in_specs += [pl.BlockSpec((tm, o.shape[1]), row) for o in outs]
    in_specs += [pl.BlockSpec(w.shape, fixed) for w in weights_bf16]
    in_specs += [pl.BlockSpec((1, d), fixed), pl.BlockSpec((d, LANES), fixed), pl.BlockSpec((1, LANES), fixed),
                 pl.BlockSpec((1, LANES), fixed)]
    return pl.pallas_call(
        functools.partial(_out_route_kernel, len(outs)),
        grid=(n // tm,), in_specs=in_specs,
        out_specs=[pl.BlockSpec((tm, d), row), pl.BlockSpec((tm, d), row), pl.BlockSpec((tm, LANES), row),
                   pl.BlockSpec((1, LANES), fixed)],
        out_shape=[jax.ShapeDtypeStruct((n, d), F32), jax.ShapeDtypeStruct((n, d), F32),
                   jax.ShapeDtypeStruct((n, LANES), F32), jax.ShapeDtypeStruct((1, LANES), F32)],
        scratch_shapes=[pltpu.VMEM((1, LANES), F32)],
        compiler_params=_params("arbitrary"), name="out_route",
    )(h, *outs, *weights_bf16, norm_w.reshape(1, d), wr, br, counts_in)


def _experts_kernel(te_ref, nv_ref, x_ref, wg_ref, wu_ref, wd_ref, y_ref):
    i = pl.program_id(0)

    @pl.when(i < nv_ref[0])
    def _():
        x = x_ref[...].astype(BF16)
        g = jnp.dot(x, wg_ref[0].astype(BF16), preferred_element_type=F32)
        u = jnp.dot(x, wu_ref[0].astype(BF16), preferred_element_type=F32)
        hh = (_silu(g) * u).astype(BF16)
        y_ref[...] = jnp.dot(hh, wd_ref[0].astype(BF16), preferred_element_type=F32)

    @pl.when(i >= nv_ref[0])
    def _():
        y_ref[...] = jnp.zeros(y_ref.shape, F32)


def moe_experts(xs, tile_expert, n_valid, w_gate, w_up, w_down, tm):
    p, d = xs.shape
    f = w_gate.shape[-1]
    n_tiles = p // tm
    grid_spec = pltpu.PrefetchScalarGridSpec(
        num_scalar_prefetch=2,
        grid=(n_tiles,),
        in_specs=[
            pl.BlockSpec((tm, d), lambda i, te, nv: (i, 0)),
            pl.BlockSpec((1, d, f), lambda i, te, nv: (te[i], 0, 0)),
            pl.BlockSpec((1, d, f), lambda i, te, nv: (te[i], 0, 0)),
            pl.BlockSpec((1, f, d), lambda i, te, nv: (te[i], 0, 0)),
        ],
        out_specs=pl.BlockSpec((tm, d), lambda i, te, nv: (i, 0)),
    )
    return pl.pallas_call(
        _experts_kernel, grid_spec=grid_spec,
        out_shape=jax.ShapeDtypeStruct((p, d), F32),
        compiler_params=_params("arbitrary"), name="moe_experts",
    )(tile_expert, n_valid, xs, w_gate, w_up, w_down)


def hier_moe(xn, route, counts, w_gate, w_up, w_down, part_sizes, tm=256):
    n, d = xn.shape
    e_idx = route[:, :TOP_K].astype(jnp.int32)
    rank = route[:, 2 * TOP_K:3 * TOP_K].astype(jnp.int32)
    cnt = counts[0, N_GROUPS:N_GROUPS + N_EXPERTS].astype(jnp.int32)
    padded = ((cnt + tm - 1) // tm) * tm
    pad_ends = jnp.cumsum(padded)
    pad_starts = pad_ends - padded
    experts = jnp.arange(N_EXPERTS, dtype=jnp.int32)
    pos = jnp.sum(jnp.where(e_idx[..., None] == experts, pad_starts, 0), axis=-1) + rank
    p = n * TOP_K + N_EXPERTS * tm
    n_tiles = p // tm
    tok = jnp.broadcast_to(jnp.arange(n, dtype=jnp.int32)[:, None], (n, TOP_K))
    row_tok = jnp.zeros((p,), jnp.int32).at[pos.reshape(-1)].set(tok.reshape(-1), unique_indices=True)
    n_valid = (pad_ends[-1] // tm).astype(jnp.int32)
    tile_start = jnp.minimum(jnp.arange(n_tiles, dtype=jnp.int32), n_valid - 1) * tm
    tile_e = jnp.sum((tile_start[:, None] >= pad_ends[None, :]).astype(jnp.int32), axis=-1)
    tile_e = jnp.minimum(tile_e, N_EXPERTS - 1)
    xs = jnp.take(xn, row_tok, axis=0)
    ys = moe_experts(xs, tile_e, n_valid.reshape(1), w_gate, w_up, w_down, tm)
    parts, lo = [], 0
    for sz in part_sizes:
        parts.append(tuple(jnp.take(ys, pos[lo:lo + sz, kk], axis=0) for kk in range(TOP_K)))
        lo += sz
    return parts


def _final_norm_kernel(h_ref, rt_ref, a_ref, b_ref, nw_ref, o_ref):
    h = _add_expert_outputs(h_ref[...], rt_ref, (a_ref, b_ref))
    o_ref[...] = h * lax.rsqrt(jnp.mean(h * h, axis=-1, keepdims=True) + EPS) * nw_ref[...]


def final_norm(h, route, a, b, norm_w, tm=256):
    n, d = h.shape
    tm = min(tm, n)
    row = pl.BlockSpec((tm, d), lambda i: (i, 0))
    return pl.pallas_call(
        _final_norm_kernel, grid=(n // tm,),
        in_specs=[row, pl.BlockSpec((tm, LANES), lambda i: (i, 0)), row, row, pl.BlockSpec((1, d), lambda i: (0, 0))],
        out_specs=row, out_shape=jax.ShapeDtypeStruct((n, d), F32),
        compiler_params=_params("parallel"), name="final_norm",
    )(h, route, a, b, norm_w.reshape(1, d))


def kernel(x_prompt, x_sample, state_a_conv, state_a_ssm, cache_b_k, cache_b_v, cache_c_k, cache_c_v, page_table, norm_mix, norm_ffn, norm_final, rel_bias, w_in0, conv_w, a_log, dt_bias, gdn_norm_w, diff_lambda, diff_subln_w, w_out0, w_in1, b_in1, sinks, w_out1, w_group, b_group, w_router, b_router, w_gate, w_up, w_down):
    bp, tp, d = x_prompt.shape
    sn, sl, _ = x_sample.shape
    n_p, n_s = bp * tp, sn * sl
    parts = ((x_prompt.reshape(n_p, d), bp, tp), (x_sample.reshape(n_s, d), sn, sl))

    off_z = CONV_CH
    off_a = off_z + H_A * DV_A
    off_qb = off_a + 2 * H_A
    w0 = jnp.concatenate([w_in0[:, :off_a], w_in0[:, off_a:off_qb],
                          jnp.zeros((d, LANES - 2 * H_A), w_in0.dtype), w_in0[:, off_qb:]], axis=1).astype(BF16)
    widths0 = (CONV_CH, H_A * DV_A, LANES, B_W, B_W, B_W)
    w_out0_bf = w_out0.astype(BF16)
    w_out0_parts = [w_out0_bf[:H_A * DV_A], w_out0_bf[H_A * DV_A:]]

    def moe_layer(layer, hs, xns, routes, counts):
        xn_all = jnp.concatenate(xns, axis=0)
        route_all = jnp.concatenate(routes, axis=0)
        return hier_moe(xn_all, route_all, counts, w_gate[layer], w_up[layer], w_down[layer],
                        [h.shape[0] for h in hs])

    conv_states = (jnp.zeros((bp, CONV_W - 1, CONV_CH), F32), state_a_conv)
    ssm_states = (jnp.zeros((bp, H_A, DK_A, DV_A), F32), state_a_ssm)
    hs, xns, routes, conv_new, ssm_new, kb_new, vb_new = [], [], [], [], [], [], []
    counts = jnp.zeros((1, LANES), F32)
    for idx, (x2, b, t) in enumerate(parts):
        qkv, z, ab, qb, kb, vb = norm_proj(x2, [], norm_mix[0], w0, None, widths0, emit_h=False)
        r3 = lambda a: a.reshape(b, t, a.shape[-1])
        o_a, s_new = gdn_heads(r3(qkv), r3(z), r3(ab), conv_states[idx], ssm_states[idx], conv_w, a_log, dt_bias,
                               gdn_norm_w, min(GDN_CHUNK, t), 2 if idx == 0 else 8)
        if idx == 0:
            o_b = diff_attn_prompt(r3(qb), r3(kb), r3(vb), rel_bias, diff_lambda, diff_subln_w, 512)
        else:
            o_b = diff_attn_sample(r3(qb), r3(kb), r3(vb), cache_b_k, cache_b_v, page_table, rel_bias, diff_lambda,
                                   diff_subln_w, 16)
        h1, xn, route, counts = out_route(x2, [o_a.reshape(b * t, -1), o_b.reshape(b * t, -1)], w_out0_parts,
                                          norm_ffn[0], w_group[0], b_group[0], w_router[0], b_router[0], counts)
        hs.append(h1), xns.append(xn), routes.append(route)
        conv_new.append(r3(qkv)[:, t - (CONV_W - 1):, :])
        ssm_new.append(s_new)
        kb_new.append(kb.reshape(b, t, H_B, 2 * DH_B))
        vb_new.append(vb.reshape(b, t, H_B, 2 * DH_B))
    ys = moe_layer(0, hs, xns, routes, counts)

    w1 = w_in1.astype(BF16)
    w_out1_bf = w_out1.astype(BF16)
    routes0 = routes
    hs2, xns, routes, kc_new, vc_new = [], [], [], [], []
    counts = jnp.zeros((1, LANES), F32)
    for idx, (_, b, t) in enumerate(parts):
        h2, q, k, v = norm_proj(hs[idx], [routes0[idx], *ys[idx]], norm_mix[1], w1, b_in1, (Q_C, KV_C, KV_C),
                                emit_h=True)
        r3 = lambda a: a.reshape(b, t, a.shape[-1])
        if idx == 0:
            o_c = swa_prompt(r3(q), r3(k), r3(v), rel_bias, sinks)
            kc_new.append(r3(k)[:, t - WINDOW:].reshape(b, WINDOW, H_KV_C, DH_C))
            vc_new.append(r3(v)[:, t - WINDOW:].reshape(b, WINDOW, H_KV_C, DH_C))
        else:
            o_c = swa_sample(r3(q), r3(k), r3(v), cache_c_k, cache_c_v, rel_bias, sinks, 8)
            kc_new.append(jnp.concatenate([cache_c_k[:, t:], k.reshape(b, t, H_KV_C, DH_C)], axis=1))
            vc_new.append(jnp.concatenate([cache_c_v[:, t:], v.reshape(b, t, H_KV_C, DH_C)], axis=1))
        h3, xn, route, counts = out_route(h2, [o_c.reshape(b * t, -1)], [w_out1_bf], norm_ffn[1],
                                          w_group[1], b_group[1], w_router[1], b_router[1], counts)
        hs2.append(h3), xns.append(xn), routes.append(route)
    ys = moe_layer(1, hs2, xns, routes, counts)

    y_out = [final_norm(hs2[idx], routes[idx], ys[idx][0], ys[idx][1], norm_final).reshape(b, t, d)
             for idx, (_, b, t) in enumerate(parts)]
    return (y_out[0], y_out[1], conv_new[0], conv_new[1], ssm_new[0], ssm_new[1],
            kb_new[0], vb_new[0], kb_new[1], vb_new[1], kc_new[0], vc_new[0], kc_new[1], vc_new[1])
```

```python
import functools
import math

import jax
import jax.numpy as jnp
import numpy as np
from jax import lax
from jax.experimental import pallas as pl
from jax.experimental.pallas import tpu as pltpu

F32 = jnp.float32
BF16 = jnp.bfloat16
EPS = 1e-6
NEG = -1e30
LOG2E = math.log2(math.e)
LANES = 128
VMEM_LIMIT = 56 * 1024 * 1024

D_MODEL = 1024
H_A, DK_A, DV_A, CONV_W = 4, 128, 128, 4
CONV_CH = 2 * H_A * DK_A + H_A * DV_A
GDN_CHUNK = 64
GDN_PRECISE_INV = False
H_B, DH_B = 4, 64
B_W = H_B * 2 * DH_B
PAGE = 128
H_C, H_KV_C, DH_C, WINDOW = 16, 2, 64, 128
G_C = H_C // H_KV_C
Q_C = H_C * DH_C
KV_C = H_KV_C * DH_C
N_BUCKETS, MAX_DIST = 32, 128
N_GROUPS, EXP_PER_GROUP, TOP_K, D_EXPERT = 4, 8, 2, 256
N_EXPERTS = N_GROUPS * EXP_PER_GROUP
LAM_INIT0 = 0.8 - 0.6 * math.exp(-0.3 * 0)


def _params(*sem):
    return pltpu.CompilerParams(dimension_semantics=sem, vmem_limit_bytes=VMEM_LIMIT)


def _mm(a, b, precise=False):
    if precise:
        return jnp.dot(a.astype(F32), b.astype(F32), preferred_element_type=F32, precision=lax.Precision.HIGHEST)
    return jnp.dot(a.astype(BF16), b.astype(BF16), preferred_element_type=F32)


def _mm_nt(a, b, precise=False):
    dn = (((1,), (1,)), ((), ()))
    if precise:
        return lax.dot_general(a.astype(F32), b.astype(F32), dn, preferred_element_type=F32,
                               precision=lax.Precision.HIGHEST)
    return lax.dot_general(a.astype(BF16), b.astype(BF16), dn, preferred_element_type=F32)


def _mm_tn(a, b, precise=False):
    dn = (((0,), (0,)), ((), ()))
    if precise:
        return lax.dot_general(a.astype(F32), b.astype(F32), dn, preferred_element_type=F32,
                               precision=lax.Precision.HIGHEST)
    return lax.dot_general(a.astype(BF16), b.astype(BF16), dn, preferred_element_type=F32)


def _sigmoid(x):
    return 1.0 / (1.0 + jnp.exp(-x))


def _silu(x):
    return x * _sigmoid(x)


def _softplus(x):
    return jnp.maximum(x, 0.0) + jnp.log(1.0 + jnp.exp(-jnp.abs(x)))


def _add_expert_outputs(h, route_ref, y_refs):
    rt = route_ref[...]
    for kk, y_ref in enumerate(y_refs):
        h = h + rt[:, TOP_K + kk:TOP_K + kk + 1] * y_ref[...]
    return h


def _norm_proj_kernel(n_add, widths, has_bias, emit_h, *refs):
    x_ref = refs[0]
    add_refs = refs[1:1 + n_add]
    nw_ref, w_ref = refs[1 + n_add], refs[2 + n_add]
    pos = 3 + n_add
    b_ref = None
    if has_bias:
        b_ref = refs[pos]
        pos += 1
    outs = refs[pos:]
    h = x_ref[...]
    if n_add:
        h = _add_expert_outputs(h, add_refs[0], add_refs[1:])
    if emit_h:
        outs[0][...] = h
        outs = outs[1:]
    xn = h * lax.rsqrt(jnp.mean(h * h, axis=-1, keepdims=True) + EPS) * nw_ref[...]
    xb = xn.astype(BF16)
    off = 0
    for o_ref, wd in zip(outs, widths):
        y = jnp.dot(xb, w_ref[:, off:off + wd], preferred_element_type=F32)
        if has_bias:
            y = y + b_ref[:, off:off + wd]
        o_ref[...] = y
        off += wd


def norm_proj(x, addends, norm_w, w_bf16, bias, widths, emit_h, tm=256):
    n, d = x.shape
    m = w_bf16.shape[1]
    tm = min(tm, n)
    assert sum(widths) == m and n % tm == 0
    row = lambda i: (i, 0)
    fixed = lambda i: (0, 0)
    in_specs = [pl.BlockSpec((tm, d), row)] + [pl.BlockSpec((tm, a.shape[1]), row) for a in addends]
    in_specs += [pl.BlockSpec((1, d), fixed), pl.BlockSpec((d, m), fixed)]
    args = [x, *addends, norm_w.reshape(1, d), w_bf16]
    if bias is not None:
        in_specs.append(pl.BlockSpec((1, m), fixed))
        args.append(bias.reshape(1, m))
    out_shape, out_specs = [], []
    if emit_h:
        out_shape.append(jax.ShapeDtypeStruct((n, d), F32))
        out_specs.append(pl.BlockSpec((tm, d), row))
    for wd in widths:
        out_shape.append(jax.ShapeDtypeStruct((n, wd), F32))
        out_specs.append(pl.BlockSpec((tm, wd), row))
    return pl.pallas_call(
        functools.partial(_norm_proj_kernel, len(addends), tuple(widths), bias is not None, emit_h),
        grid=(n // tm,), in_specs=in_specs, out_specs=out_specs, out_shape=out_shape,
        compiler_params=_params("parallel"), name="norm_proj",
    )(*args)


def _gdn_kernel(c_len, nb, qkv_ref, z_ref, ab_ref, cbuf_ref, s0_ref, convw_ref, gpar_ref, nw_ref,
                o_ref, sfin_ref, xp_scr, s_scr):
    c = pl.program_id(1)
    n_c = pl.num_programs(1)
    hist = CONV_W - 1
    base = 8 - hist

    @pl.when(c == 0)
    def _():
        xp_scr[:, base:8, :] = cbuf_ref[...]
        s_scr[...] = s0_ref[...]

    ri = lax.broadcasted_iota(jnp.int32, (c_len, c_len), 0)
    ci = lax.broadcasted_iota(jnp.int32, (c_len, c_len), 1)
    lower = ri >= ci
    strict = ri > ci
    eye = (ri == ci).astype(F32)
    lower_f = lower.astype(F32)

    units = []
    for bi in range(nb):
        xp_scr[bi, 8:8 + c_len, :] = qkv_ref[bi]
        y = xp_scr[bi, base:base + c_len, :] * convw_ref[0:1, :]
        for j in range(1, CONV_W):
            y = y + xp_scr[bi, base + j:base + j + c_len, :] * convw_ref[j:j + 1, :]
        xp_scr[bi, base:8, :] = xp_scr[bi, base + c_len:8 + c_len, :]
        y = _silu(y)

        ab = ab_ref[bi]
        g_t = gpar_ref[0:1, :] * _softplus(ab + gpar_ref[1:2, :])
        beta_t = _sigmoid(ab)
        gcum_t = _mm(lower_f, g_t, precise=True)
        gcum_tt = gcum_t.T

        for h in range(H_A):
            q = y[:, h * DK_A:(h + 1) * DK_A]
            k = y[:, (H_A + h) * DK_A:(H_A + h + 1) * DK_A]
            v = y[:, 2 * H_A * DK_A + h * DV_A:2 * H_A * DK_A + (h + 1) * DV_A]
            q = q * lax.rsqrt(jnp.sum(q * q, axis=-1, keepdims=True) + EPS) * (DK_A ** -0.5)
            k = k * lax.rsqrt(jnp.sum(k * k, axis=-1, keepdims=True) + EPS)
            gc = gcum_t[:, h:h + 1]
            gr = gcum_tt[h:h + 1, :]
            beta = beta_t[:, H_A + h:H_A + h + 1]
            decay = jnp.where(lower, jnp.exp(jnp.where(lower, gc - gr, 0.0)), 0.0)
            kb = k * beta
            units.append(dict(bi=bi, h=h, q=q, k=k, k_bf=k.astype(BF16), gc=gc, decay=decay, kb=kb,
                              rhs=jnp.concatenate([v * beta, kb * jnp.exp(gc)], axis=1)))

    for un in units:
        un["nmat"] = jnp.where(strict, _mm_nt(un["kb"], un["k_bf"]) * un["decay"], 0.0)
        un["qk"] = jnp.where(lower, _mm_nt(un["q"], un["k_bf"]) * un["decay"], 0.0)
    for un in units:
        un["inv"] = eye - un["nmat"]
        un["pw"] = _mm(un["nmat"], un["nmat"], GDN_PRECISE_INV)
    span = 2
    while span < c_len:
        for un in units:
            un["inv"] = un["inv"] + _mm(un["inv"], un["pw"], GDN_PRECISE_INV)
        span *= 2
        if span < c_len:
            for un in units:
                un["pw"] = _mm(un["pw"], un["pw"], GDN_PRECISE_INV)
    for un in units:
        sol = _mm(un["inv"], un["rhs"], GDN_PRECISE_INV)
        un["u"], un["w"] = sol[:, :DV_A], sol[:, DV_A:]
        un["s"] = s_scr[un["bi"], un["h"]]
        un["s_bf"] = un["s"].astype(BF16)
    for un in units:
        un["v_new"] = un["u"] - _mm(un["w"], un["s_bf"])
        un["o_s"] = _mm(un["q"] * jnp.exp(un["gc"]), un["s_bf"])
    for un in units:
        bi, h, gc = un["bi"], un["h"], un["gc"]
        o = un["o_s"] + _mm(un["qk"], un["v_new"])
        g_last = gc[c_len - 1:c_len, :]
        s_scr[bi, h] = un["s"] * jnp.exp(g_last) + _mm_tn(un["k"] * jnp.exp(g_last - gc), un["v_new"])
        o = o * lax.rsqrt(jnp.mean(o * o, axis=-1, keepdims=True) + EPS) * nw_ref[...]
        o_ref[bi, :, h * DV_A:(h + 1) * DV_A] = o * _silu(z_ref[bi, :, h * DV_A:(h + 1) * DV_A])

    @pl.when(c == n_c - 1)
    def _():
        sfin_ref[...] = s_scr[...]


def gdn_heads(qkv, z, ab, conv_buf, s0, conv_w, a_log, dt_bias, norm_w, c_len, nb):
    b, t, _ = qkv.shape
    assert t % c_len == 0 and b % nb == 0
    convw_t = conv_w.T
    gpar = jnp.zeros((2, LANES), F32)
    gpar = gpar.at[0, :H_A].set(-jnp.exp(a_log.astype(F32))).at[1, :H_A].set(dt_bias.astype(F32))
    tok = lambda i, j: (i, j, 0)
    seq3 = lambda i, j: (i, 0, 0)
    fixed = lambda i, j: (0, 0)
    return pl.pallas_call(
        functools.partial(_gdn_kernel, c_len, nb),
        grid=(b // nb, t // c_len),
        in_specs=[
            pl.BlockSpec((nb, c_len, CONV_CH), tok),
            pl.BlockSpec((nb, c_len, H_A * DV_A), tok),
            pl.BlockSpec((nb, c_len, LANES), tok),
            pl.BlockSpec((nb, CONV_W - 1, CONV_CH), seq3),
            pl.BlockSpec((nb, H_A, DK_A, DV_A), lambda i, j: (i, 0, 0, 0)),
            pl.BlockSpec((CONV_W, CONV_CH), fixed),
            pl.BlockSpec((2, LANES), fixed),
            pl.BlockSpec((1, DV_A), fixed),
        ],
        out_specs=[
            pl.BlockSpec((nb, c_len, H_A * DV_A), tok),
            pl.BlockSpec((nb, H_A, DK_A, DV_A), lambda i, j: (i, 0, 0, 0)),
        ],
        out_shape=[
            jax.ShapeDtypeStruct((b, t, H_A * DV_A), F32),
            jax.ShapeDtypeStruct((b, H_A, DK_A, DV_A), F32),
        ],
        scratch_shapes=[
            pltpu.VMEM((nb, 8 + c_len, CONV_CH), F32),
            pltpu.VMEM((nb, H_A, DK_A, DV_A), F32),
        ],
        compiler_params=_params("parallel", "arbitrary"), name="gdn_heads",
    )(qkv, z, ab, conv_buf, s0, convw_t, gpar, norm_w.reshape(1, DV_A))


def _t5_bucket_np(dist):
    d = np.maximum(dist, 0)
    max_exact = N_BUCKETS // 2
    ratio = (np.log(np.maximum(d, 1).astype(np.float32) / np.float32(max_exact))
             / np.float32(math.log(MAX_DIST / max_exact)))
    large = np.minimum(max_exact + (ratio * (N_BUCKETS - max_exact)).astype(np.int32), N_BUCKETS - 1)
    return np.where(d < max_exact, d, large).astype(np.int32)


FAR_BUCKET = int(_t5_bucket_np(np.array([MAX_DIST]))[0])
assert np.all(_t5_bucket_np(np.arange(MAX_DIST, 4 * MAX_DIST)) == FAR_BUCKET)


def _bias_kernel(col_group, bucket_ref, rb_ref, out_ref):
    for col, grp in enumerate(col_group):
        bk = bucket_ref[grp]
        acc = jnp.full(bk.shape, NEG, F32)
        for b in range(N_BUCKETS):
            acc = jnp.where(bk == b, rb_ref[b, col], acc)
        out_ref[col] = acc


def bias_tiles(rel_bias, dist, valid, n_cols, col_group=None):
    bucket = np.where(valid, _t5_bucket_np(dist), -1).astype(np.int32)
    if bucket.ndim == 2:
        bucket, col_group = bucket[None], (0,) * n_cols
    _, r, c = bucket.shape
    return pl.pallas_call(
        functools.partial(_bias_kernel, tuple(col_group)),
        in_specs=[pl.BlockSpec(memory_space=pltpu.VMEM), pl.BlockSpec(memory_space=pltpu.SMEM)],
        out_specs=pl.BlockSpec(memory_space=pltpu.VMEM),
        out_shape=jax.ShapeDtypeStruct((n_cols, r, c), F32),
        name="bias_tiles",
    )(jnp.asarray(bucket), rel_bias.astype(F32))


def _diff_lambda(lam_ref):
    lp = lam_ref[...]
    s1 = jnp.sum(lp[0:1] * lp[1:2], axis=-1, keepdims=True)
    s2 = jnp.sum(lp[2:3] * lp[3:4], axis=-1, keepdims=True)
    return jnp.exp(s1) - jnp.exp(s2) + LAM_INIT0


def _diff_finish(o1, o2, lam, nw):
    o = o1 - lam * o2
    return o * lax.rsqrt(jnp.mean(o * o, axis=-1, keepdims=True) + EPS) * nw * (1.0 - LAM_INIT0)


def _softmax_update(s, v_bf, m_ref, l_ref, acc_ref):
    m_prev = m_ref[...]
    m_new = jnp.maximum(m_prev, jnp.max(s, axis=-1, keepdims=True))
    alpha = jnp.exp(m_prev - m_new)
    p = jnp.exp(s - m_new)
    l_ref[...] = alpha * l_ref[...] + jnp.sum(p, axis=-1, keepdims=True)
    acc_ref[...] = alpha * acc_ref[...] + jnp.dot(p.astype(BF16), v_bf, preferred_element_type=F32)
    m_ref[...] = m_new


def _diff_prompt_kernel(blk, q_ref, k_ref, v_ref, t0_ref, t1_ref, far_ref, lam_ref, nw_ref, o_ref,
                        m_scr, acc_scr):
    qi = pl.program_id(1)
    ki = pl.program_id(2)
    nsub = blk // LANES
    hw = 2 * DH_B

    @pl.when(ki == 0)
    def _():
        m_scr[...] = jnp.full(m_scr.shape, NEG, F32)
        acc_scr[...] = jnp.zeros(acc_scr.shape, F32)

    def bias_block(case, hm):
        const = jnp.full((LANES, LANES), far_ref[hm], F32)
        neg = jnp.full((LANES, LANES), NEG, F32)
        rows = []
        for jj in range(nsub):
            tiles = []
            for ii in range(nsub):
                sub = ii - jj + (nsub if case == "prev" else 0)
                tiles.append(neg if sub < 0 else t0_ref[hm] if sub == 0 else t1_ref[hm] if sub == 1 else const)
            rows.append(jnp.concatenate(tiles, axis=1))
        return jnp.concatenate(rows, axis=0)

    def step(case):
        q = (q_ref[0] * (DH_B ** -0.5 * LOG2E)).astype(BF16)
        k = k_ref[0].astype(BF16)
        v_t = v_ref[0].T
        first = lax.broadcasted_iota(jnp.int32, (1, hw), 1) < DH_B
        ones = jnp.ones((hw, blk), F32)
        s_all, v_ext = [], []
        for h in range(H_B):
            sl = slice(h * hw, (h + 1) * hw)
            qh, kh = q[:, sl], k[:, sl]
            v_ext.append(jnp.concatenate([v_t[sl], ones], axis=0).astype(BF16))
            for mp in range(2):
                km = jnp.where(first if mp == 0 else jnp.logical_not(first), kh, jnp.zeros_like(kh))
                s_all.append(lax.dot_general(km, qh, (((1,), (1,)), ((), ())),
                                             preferred_element_type=F32))
        p_all, alpha_all = [], []
        for hm, s in enumerate(s_all):
            m_prev = m_scr[hm]
            if case == "far":
                shift = far_ref[hm]
                m_new = jnp.maximum(m_prev, jnp.max(s, axis=0, keepdims=True) + shift)
                p = jnp.exp2(s + (shift - m_new))
            else:
                s = s + bias_block(case, hm)
                m_new = jnp.maximum(m_prev, jnp.max(s, axis=0, keepdims=True))
                p = jnp.exp2(s - m_new)
            alpha_all.append(jnp.exp2(m_prev - m_new))
            p_all.append(p.astype(BF16))
            m_scr[hm] = m_new
        for hm, (p, alpha) in enumerate(zip(p_all, alpha_all)):
            acc_scr[hm] = alpha * acc_scr[hm] + jnp.dot(v_ext[hm // 2], p, preferred_element_type=F32)

    @pl.when(ki < qi - 1)
    def _():
        step("far")

    @pl.when(ki == qi - 1)
    def _():
        step("prev")

    @pl.when(ki == qi)
    def _():
        step("diag")
        lam = _diff_lambda(lam_ref)
        for h in range(H_B):
            a1, a2 = acc_scr[2 * h], acc_scr[2 * h + 1]
            o = a1[:hw] / a1[hw:hw + 1] - lam * (a2[:hw] / a2[hw:hw + 1])
            o = o * lax.rsqrt(jnp.mean(o * o, axis=0, keepdims=True) + EPS) * nw_ref[...] * (1.0 - LAM_INIT0)
            o_ref[0, :, h * hw:(h + 1) * hw] = o.T


def diff_attn_prompt(q, k, v, rel_bias, diff_lambda, subln_w, blk):
    b, t, _ = q.shape
    assert t % blk == 0 and blk % LANES == 0 and blk >= 2 * LANES
    i = np.arange(LANES)
    d0 = i[None, :] - i[:, None]
    t0 = bias_tiles(rel_bias, d0, d0 >= 0, 2 * H_B) * LOG2E
    t1 = bias_tiles(rel_bias, d0 + LANES, np.ones_like(d0, bool), 2 * H_B) * LOG2E
    far = rel_bias[FAR_BUCKET, :2 * H_B].astype(F32) * LOG2E
    nb = t // blk
    hw = 2 * DH_B
    whole = lambda shape: pl.BlockSpec(shape, lambda bi, qi, ki: (0,) * len(shape))
    return pl.pallas_call(
        functools.partial(_diff_prompt_kernel, blk),
        grid=(b, nb, nb),
        in_specs=[
            pl.BlockSpec((1, blk, B_W), lambda bi, qi, ki: (bi, qi, 0)),
            pl.BlockSpec((1, blk, B_W), lambda bi, qi, ki: (bi, jnp.minimum(ki, qi), 0)),
            pl.BlockSpec((1, blk, B_W), lambda bi, qi, ki: (bi, jnp.minimum(ki, qi), 0)),
            whole((2 * H_B, LANES, LANES)),
            whole((2 * H_B, LANES, LANES)),
            pl.BlockSpec(memory_space=pltpu.SMEM),
            whole((4, DH_B)),
            whole((hw, 1)),
        ],
        out_specs=pl.BlockSpec((1, blk, B_W), lambda bi, qi, ki: (bi, qi, 0)),
        out_shape=jax.ShapeDtypeStruct((b, t, B_W), F32),
        scratch_shapes=[
            pltpu.VMEM((2 * H_B, 1, blk), F32),
            pltpu.VMEM((2 * H_B, 2 * hw, blk), F32),
        ],
        compiler_params=_params("parallel", "parallel", "arbitrary"), name="diff_attn_prompt",
    )(q, k, v, t0, t1, far, diff_lambda.astype(F32), subln_w.reshape(hw, 1).astype(F32))


def _diff_sample_kernel(n_pg, s_len, pt_ref, q_ref, kn_ref, vn_ref, *refs):
    k_pages = refs[:n_pg]
    v_pages = refs[n_pg:2 * n_pg]
    blast_ref, bnew_ref, far_ref, lam_ref, nw_ref, o_ref, m_scr, l_scr, acc_scr = refs[2 * n_pg:]
    j = pl.program_id(1)
    last = pl.num_programs(1) - 1
    hw = 2 * DH_B

    @pl.when(j == 0)
    def _():
        m_scr[...] = jnp.full(m_scr.shape, NEG, F32)
        l_scr[...] = jnp.zeros(l_scr.shape, F32)
        acc_scr[...] = jnp.zeros(acc_scr.shape, F32)

    q = q_ref[0] * (DH_B ** -0.5)
    first = lax.broadcasted_iota(jnp.int32, (1, hw), 1) < DH_B
    pieces = []
    for h in range(H_B):
        qh = q[:, h * hw:(h + 1) * hw]
        pieces += [jnp.where(first, qh, 0.0), jnp.where(first, 0.0, qh)]
    qbd = jnp.concatenate(pieces, axis=0).astype(BF16)

    n_grp = 4 if n_pg % 4 == 0 else 1
    per = n_pg // n_grp
    wg = per * PAGE * H_B
    s_grp = []
    for g in range(n_grp):
        kg = jnp.concatenate([r[0] for r in k_pages[g * per:(g + 1) * per]], axis=0).astype(BF16)
        bias = jnp.where(j == last, blast_ref[:, g * wg:(g + 1) * wg], far_ref[:, g * wg:(g + 1) * wg])
        s_grp.append(lax.dot_general(qbd, kg, (((1,), (1,)), ((), ())), preferred_element_type=F32) + bias)
    m_prev = m_scr[...]
    m_new = m_prev
    for s in s_grp:
        m_new = jnp.maximum(m_new, jnp.max(s, axis=-1, keepdims=True))
    alpha = jnp.exp(m_prev - m_new)
    l_new = alpha * l_scr[...]
    acc_new = alpha * acc_scr[...]
    for g, s in enumerate(s_grp):
        p = jnp.exp(s - m_new)
        vg = jnp.concatenate([r[0] for r in v_pages[g * per:(g + 1) * per]], axis=0).astype(BF16)
        l_new = l_new + jnp.sum(p, axis=-1, keepdims=True)
        acc_new = acc_new + jnp.dot(p.astype(BF16), vg, preferred_element_type=F32)
    m_scr[...] = m_new
    l_scr[...] = l_new
    acc_scr[...] = acc_new

    @pl.when(j == last)
    def _():
        pad = jnp.zeros((PAGE - H_B * s_len, hw), F32)
        kn = jnp.concatenate([kn_ref[0, :, h * hw:(h + 1) * hw] for h in range(H_B)] + [pad], axis=0)
        vn = jnp.concatenate([vn_ref[0, :, h * hw:(h + 1) * hw] for h in range(H_B)] + [pad], axis=0)
        s2 = lax.dot_general(qbd, kn.astype(BF16), (((1,), (1,)), ((), ())), preferred_element_type=F32)
        _softmax_update(s2 + bnew_ref[...], vn.astype(BF16), m_scr, l_scr, acc_scr)
        accn = acc_scr[...] / l_scr[...]
        lam = _diff_lambda(lam_ref)
        for h in range(H_B):
            o1 = accn[(2 * h) * s_len:(2 * h + 1) * s_len]
            o2 = accn[(2 * h + 1) * s_len:(2 * h + 2) * s_len]
            o_ref[0, :, h * hw:(h + 1) * hw] = _diff_finish(o1, o2, lam, nw_ref[...])


def diff_attn_sample(q, k, v, cache_k, cache_v, page_table, rel_bias, diff_lambda, subln_w, n_pg):
    s_n, s_len, _ = q.shape
    pages_per_seq = page_table.shape[1]
    assert pages_per_seq % n_pg == 0
    past = pages_per_seq * PAGE
    n_phys = cache_k.shape[0]
    hw = 2 * DH_B
    ck = cache_k.reshape(n_phys, PAGE * H_B, hw)
    cv = cache_v.reshape(n_phys, PAGE * H_B, hw)
    n_rows = 2 * H_B * s_len
    blk_keys = n_pg * PAGE
    blk_rows = blk_keys * H_B
    assert PAGE >= H_B * s_len
    t = np.arange(s_len)
    hh = np.arange(H_B)
    col_group = tuple(c // 2 for c in range(2 * H_B))
    key_of_col = np.repeat(np.arange(blk_keys), H_B)
    head_of_col = np.tile(hh, blk_keys)
    own = np.broadcast_to((head_of_col[None, None, :] == hh[:, None, None]), (H_B, s_len, blk_rows))
    d_last = np.broadcast_to(((past + t)[:, None] - (past - blk_keys + key_of_col)[None, :])[None],
                             (H_B, s_len, blk_rows))
    blast = bias_tiles(rel_bias, d_last, own, 2 * H_B, col_group).reshape(n_rows, blk_rows)
    far = bias_tiles(rel_bias, np.full((H_B, s_len, blk_rows), MAX_DIST), own, 2 * H_B, col_group)
    far = far.reshape(n_rows, blk_rows)
    cnew = np.arange(PAGE)
    d_new = np.broadcast_to((t[:, None] - (cnew % s_len)[None, :])[None], (H_B, s_len, PAGE))
    ok_new = ((cnew // s_len)[None, None, :] == hh[:, None, None]) & (d_new >= 0) & (cnew < H_B * s_len)[None, None, :]
    bnew = bias_tiles(rel_bias, d_new, ok_new, 2 * H_B, col_group).reshape(n_rows, PAGE)
    assert past - blk_keys + s_len - 1 >= 0 and np.all(_t5_bucket_np(np.array([blk_keys + 1])) == FAR_BUCKET)

    def page_spec(i):
        return pl.BlockSpec((1, PAGE * H_B, hw), lambda s, j, pt: (pt[s * pages_per_seq + j * n_pg + i], 0, 0))

    seq = pl.BlockSpec((1, s_len, B_W), lambda s, j, pt: (s, 0, 0))
    whole = lambda shape: pl.BlockSpec(shape, lambda s, j, pt: (0,) * len(shape))
    grid_spec = pltpu.PrefetchScalarGridSpec(
        num_scalar_prefetch=1,
        grid=(s_n, pages_per_seq // n_pg),
        in_specs=[seq, seq, seq] + [page_spec(i) for i in range(n_pg)] * 2 + [
            whole((n_rows, blk_rows)), whole((n_rows, PAGE)), whole((n_rows, blk_rows)),
            whole((4, DH_B)), whole((1, hw))],
        out_specs=seq,
        scratch_shapes=[
            pltpu.VMEM((n_rows, 1), F32),
            pltpu.VMEM((n_rows, 1), F32),
            pltpu.VMEM((n_rows, hw), F32),
        ],
    )
    return pl.pallas_call(
        functools.partial(_diff_sample_kernel, n_pg, s_len),
        grid_spec=grid_spec,
        out_shape=jax.ShapeDtypeStruct((s_n, s_len, B_W), F32),
        compiler_params=_params("parallel", "arbitrary"), name="diff_attn_sample",
    )(page_table.reshape(-1).astype(jnp.int32), q, k, v, *([ck] * n_pg), *([cv] * n_pg),
      blast, bnew, far, diff_lambda.astype(F32), subln_w.reshape(1, 2 * DH_B).astype(F32))


def _kv_variants(x, kv):
    lo = lax.broadcasted_iota(jnp.int32, (1, LANES), 1) < DH_C
    rolled = pltpu.roll(x, DH_C, 1)
    a_src, b_src = (x, rolled) if kv == 0 else (rolled, x)
    zero = jnp.zeros_like(x)
    return jnp.where(lo, a_src, zero).astype(BF16), jnp.where(lo, zero, b_src).astype(BF16)


def _sink_attend(units):
    scores = [[lax.dot_general(q_bf, kx, (((1,), (1,)), ((), ())), preferred_element_type=F32) + bias
               for kx, bias in zip(k_ab, bias_ab)] for q_bf, k_ab, _, bias_ab, _ in units]
    probs = []
    for s_ab, (_, _, _, _, sink_ab) in zip(scores, units):
        p_ab = []
        for s, sink in zip(s_ab, sink_ab):
            m = jnp.maximum(jnp.max(s, axis=-1, keepdims=True), sink)
            e = jnp.exp(s - m)
            p_ab.append((e / (jnp.sum(e, axis=-1, keepdims=True) + jnp.exp(sink - m))).astype(BF16))
        probs.append(p_ab)
    return [jnp.dot(p_ab[0], v_ab[0], preferred_element_type=F32) + jnp.dot(p_ab[1], v_ab[1], preferred_element_type=F32)
            for p_ab, (_, _, v_ab, _, _) in zip(probs, units)]


def _swa_prompt_kernel(q_ref, kp_ref, kc_ref, vp_ref, vc_ref, bias_ref, sink_ref, o_ref):
    n = pl.program_id(1)
    keys = jnp.concatenate([kp_ref[0], kc_ref[0]], axis=0)
    vals = jnp.concatenate([vp_ref[0], vc_ref[0]], axis=0)
    col = lax.broadcasted_iota(jnp.int32, (1, 2 * WINDOW), 1)
    first_blk = jnp.where((col < WINDOW) & (n == 0), NEG, 0.0)
    n_slot = G_C // 2
    units, slices = [], []
    for kv in range(H_KV_C):
        k_ab = _kv_variants(keys, kv)
        v_ab = _kv_variants(vals, kv)
        for ps in range(n_slot):
            h0 = kv * G_C + 2 * ps
            sl = slice(kv * G_C * DH_C + ps * LANES, kv * G_C * DH_C + (ps + 1) * LANES)
            q = (q_ref[0, :, sl] * (DH_C ** -0.5)).astype(BF16)
            bias_ab = (bias_ref[h0] + first_blk, bias_ref[h0 + 1] + first_blk)
            units.append((q, k_ab, v_ab, bias_ab, (sink_ref[h0], sink_ref[h0 + 1])))
            slices.append(sl)
    for sl, o in zip(slices, _sink_attend(units)):
        o_ref[0, :, sl] = o


def swa_prompt(q, k, v, rel_bias, sinks):
    b, t, _ = q.shape
    nb = t // WINDOW
    i = np.arange(WINDOW)
    j = np.arange(2 * WINDOW)
    dist = WINDOW + i[:, None] - j[None, :]
    bias = bias_tiles(rel_bias, dist, (dist >= 0) & (dist <= WINDOW), H_C)
    cur = lambda bi, n: (bi, n, 0)
    prev = lambda bi, n: (bi, jnp.maximum(n - 1, 0), 0)
    return pl.pallas_call(
        _swa_prompt_kernel,
        grid=(b, nb),
        in_specs=[
            pl.BlockSpec((1, WINDOW, Q_C), cur),
            pl.BlockSpec((1, WINDOW, KV_C), prev), pl.BlockSpec((1, WINDOW, KV_C), cur),
            pl.BlockSpec((1, WINDOW, KV_C), prev), pl.BlockSpec((1, WINDOW, KV_C), cur),
            pl.BlockSpec((H_C, WINDOW, 2 * WINDOW), lambda bi, n: (0, 0, 0)),
            pl.BlockSpec(memory_space=pltpu.SMEM),
        ],
        out_specs=pl.BlockSpec((1, WINDOW, Q_C), cur),
        out_shape=jax.ShapeDtypeStruct((b, t, Q_C), F32),
        compiler_params=_params("parallel", "parallel"), name="swa_prompt",
    )(q, k, k, v, v, bias, sinks.astype(F32))


def _swa_sample_kernel(gs, s_len, q_ref, kn_ref, vn_ref, ck_ref, cv_ref, bias_ref, sink_ref, o_ref):
    n_slot = G_C // 2
    pad = jnp.zeros((WINDOW - s_len, LANES), F32)
    units, where = [], []
    for si in range(gs):
        keys = jnp.concatenate([ck_ref[si], kn_ref[si], pad], axis=0)
        vals = jnp.concatenate([cv_ref[si], vn_ref[si], pad], axis=0)
        for kv in range(H_KV_C):
            k_ab = _kv_variants(keys, kv)
            v_ab = _kv_variants(vals, kv)
            base = kv * G_C * DH_C
            q = jnp.concatenate([q_ref[si, :, base + ps * LANES:base + (ps + 1) * LANES] for ps in range(n_slot)],
                                axis=0)
            q = (q * (DH_C ** -0.5)).astype(BF16)
            units.append((q, k_ab, v_ab, (bias_ref[kv, 0], bias_ref[kv, 1]), (sink_ref[kv, 0], sink_ref[kv, 1])))
            where.append((si, base))
    for (si, base), o in zip(where, _sink_attend(units)):
        for ps in range(n_slot):
            o_ref[si, :, base + ps * LANES:base + (ps + 1) * LANES] = o[ps * s_len:(ps + 1) * s_len]


def swa_sample(q, k, v, cache_k, cache_v, rel_bias, sinks, gs):
    s_n, s_len, _ = q.shape
    assert s_n % gs == 0 and cache_k.shape[1] == WINDOW
    ck = cache_k.reshape(s_n, WINDOW, KV_C)
    cv = cache_v.reshape(s_n, WINDOW, KV_C)
    tq = np.arange(s_len)
    j = np.arange(2 * WINDOW)
    dist = WINDOW + tq[:, None] - j[None, :]
    valid = (dist >= 0) & (dist <= WINDOW) & (j[None, :] < WINDOW + s_len)
    n_slot = G_C // 2
    bias = bias_tiles(rel_bias, dist, valid, H_C)
    bias = bias.reshape(H_KV_C, n_slot, 2, s_len, 2 * WINDOW).transpose(0, 2, 1, 3, 4)
    bias = bias.reshape(H_KV_C, 2, n_slot * s_len, 2 * WINDOW)
    sk = sinks.astype(F32).reshape(H_KV_C, n_slot, 2).transpose(0, 2, 1)
    sk = jnp.repeat(sk, s_len, axis=-1).reshape(H_KV_C, 2, n_slot * s_len, 1)
    grp = lambda i: (i, 0, 0)
    whole = lambda shape: pl.BlockSpec(shape, lambda i: (0,) * len(shape))
    return pl.pallas_call(
        functools.partial(_swa_sample_kernel, gs, s_len),
        grid=(s_n // gs,),
        in_specs=[
            pl.BlockSpec((gs, s_len, Q_C), grp),
            pl.BlockSpec((gs, s_len, KV_C), grp), pl.BlockSpec((gs, s_len, KV_C), grp),
            pl.BlockSpec((gs, WINDOW, KV_C), grp), pl.BlockSpec((gs, WINDOW, KV_C), grp),
            whole(bias.shape), whole(sk.shape),
        ],
        out_specs=pl.BlockSpec((gs, s_len, Q_C), grp),
        out_shape=jax.ShapeDtypeStruct((s_n, s_len, Q_C), F32),
        compiler_params=_params("parallel"), name="swa_sample",
    )(q, k, v, ck, cv, bias, sk)


ROUTE_BIG = 1 << 20


def _out_route_kernel(n_in, *refs):
    h_ref = refs[0]
    o_refs = refs[1:1 + n_in]
    w_refs = refs[1 + n_in:1 + 2 * n_in]
    nw_ref, wr_ref, br_ref, cin_ref, hn_ref, xn_ref, rt_ref, cout_ref, cnt_scr = refs[1 + 2 * n_in:]
    step = pl.program_id(0)

    @pl.when(step == 0)
    def _():
        cnt_scr[...] = cin_ref[...]

    h = h_ref[...]
    for o_ref, w_ref in zip(o_refs, w_refs):
        h = h + jnp.dot(o_ref[...].astype(BF16), w_ref[...], preferred_element_type=F32)
    hn_ref[...] = h
    xn = h * lax.rsqrt(jnp.mean(h * h, axis=-1, keepdims=True) + EPS) * nw_ref[...]
    xn_ref[...] = xn
    logit = jnp.dot(xn, wr_ref[...], preferred_element_type=F32, precision=lax.Precision.HIGHEST) + br_ref[...]
    lane = lax.broadcasted_iota(jnp.int32, logit.shape, 1)
    gmask = lane < N_GROUPS
    gl = jnp.where(gmask, logit, NEG)
    gmax = jnp.max(gl, axis=-1, keepdims=True)
    gsel = jnp.min(jnp.where(gl == gmax, lane, ROUTE_BIG), axis=-1, keepdims=True)
    gw = 1.0 / jnp.sum(jnp.where(gmask, jnp.exp(gl - gmax), 0.0), axis=-1, keepdims=True)
    eid = lane - N_GROUPS
    emask = (eid >= 0) & (eid < N_EXPERTS) & (jnp.right_shift(eid, 3) == gsel)
    el = jnp.where(emask, logit, NEG)
    v1 = jnp.max(el, axis=-1, keepdims=True)
    i1 = jnp.min(jnp.where((el == v1) & emask, lane, ROUTE_BIG), axis=-1, keepdims=True)
    emask2 = emask & (lane != i1)
    el2 = jnp.where(emask2, logit, NEG)
    v2 = jnp.max(el2, axis=-1, keepdims=True)
    i2 = jnp.min(jnp.where((el2 == v2) & emask2, lane, ROUTE_BIG), axis=-1, keepdims=True)
    e = jnp.exp(v2 - v1)
    w1 = gw / (1.0 + e)
    w2 = gw * e / (1.0 + e)
    tm = logit.shape[0]
    onehot = ((lane == i1) | (lane == i2)).astype(BF16)
    ri = lax.broadcasted_iota(jnp.int32, (tm, tm), 0)
    ci = lax.broadcasted_iota(jnp.int32, (tm, tm), 1)
    before = jnp.dot((ri > ci).astype(BF16), onehot, preferred_element_type=F32) + cnt_scr[...]
    r1 = jnp.sum(jnp.where(lane == i1, before, 0.0), axis=-1, keepdims=True)
    r2 = jnp.sum(jnp.where(lane == i2, before, 0.0), axis=-1, keepdims=True)
    cnt_scr[...] = cnt_scr[...] + jnp.sum(onehot.astype(F32), axis=0, keepdims=True)
    rt = jnp.where(lane == 0, (i1 - N_GROUPS).astype(F32),
                   jnp.where(lane == 1, (i2 - N_GROUPS).astype(F32),
                             jnp.where(lane == 2, w1, jnp.where(lane == 3, w2,
                                                                jnp.where(lane == 4, r1,
                                                                          jnp.where(lane == 5, r2, 0.0))))))
    rt_ref[...] = rt

    @pl.when(step == pl.num_programs(0) - 1)
    def _():
        cout_ref[...] = cnt_scr[...]


def out_route(h, outs, weights_bf16, norm_w, w_group, b_group, w_router, b_router, counts_in, tm=512):
    n, d = h.shape
    tm = min(tm, n)
    assert n % tm == 0 and EXP_PER_GROUP == 8
    wr = jnp.zeros((d, LANES), F32).at[:, :N_GROUPS].set(w_group).at[:, N_GROUPS:N_GROUPS + N_EXPERTS].set(w_router)
    br = jnp.zeros((1, LANES), F32).at[0, :N_GROUPS].set(b_group).at[0, N_GROUPS:N_GROUPS + N_EXPERTS].set(b_router)
    row = lambda i: (i, 0)
    fixed = lambda i: (0, 0)
    in_specs = [pl.BlockSpec((tm, d), row)]
    in_specs += [pl.BlockSpec((tm, o.shape[1]), row) for o in outs]
    in_specs += [pl.BlockSpec(w.shape, fixed) for w in weights_bf16]
    in_specs += [pl.BlockSpec((1, d), fixed), pl.BlockSpec((d, LANES), fixed), pl.BlockSpec((1, LANES), fixed),
                 pl.BlockSpec((1, LANES), fixed)]
    return pl.pallas_call(
        functools.partial(_out_route_kernel, len(outs)),
        grid=(n // tm,), in_specs=in_specs,
        out_specs=[pl.BlockSpec((tm, d), row), pl.BlockSpec((tm, d), row), pl.BlockSpec((tm, LANES), row),
                   pl.BlockSpec((1, LANES), fixed)],
        out_shape=[jax.ShapeDtypeStruct((n, d), F32), jax.ShapeDtypeStruct((n, d), F32),
                   jax.ShapeDtypeStruct((n, LANES), F32), jax.ShapeDtypeStruct((1, LANES), F32)],
        scratch_shapes=[pltpu.VMEM((1, LANES), F32)],
        compiler_params=_params("arbitrary"), name="out_route",
    )(h, *outs, *weights_bf16, norm_w.reshape(1, d), wr, br, counts_in)


def _experts_kernel(te_ref, nv_ref, x_ref, wg_ref, wu_ref, wd_ref, y_ref):
    i = pl.program_id(0)

    @pl.when(i < nv_ref[0])
    def _():
        x = x_ref[...].astype(BF16)
        g = jnp.dot(x, wg_ref[0, 0].astype(BF16), preferred_element_type=F32)
        u = jnp.dot(x, wu_ref[0, 0].astype(BF16), preferred_element_type=F32)
        hh = (_silu(g) * u).astype(BF16)
        y_ref[...] = jnp.dot(hh, wd_ref[0, 0].astype(BF16), preferred_element_type=F32)

    @pl.when(i >= nv_ref[0])
    def _():
        y_ref[...] = jnp.zeros(y_ref.shape, F32)


def moe_experts(xs, tile_expert, n_valid, w_gate, w_up, w_down, layer, tm):
    p, d = xs.shape
    f = w_gate.shape[-1]
    n_tiles = p // tm
    grid_spec = pltpu.PrefetchScalarGridSpec(
        num_scalar_prefetch=2,
        grid=(n_tiles,),
        in_specs=[
            pl.BlockSpec((tm, d), lambda i, te, nv: (i, 0)),
            pl.BlockSpec((1, 1, d, f), lambda i, te, nv: (layer, te[i], 0, 0)),
            pl.BlockSpec((1, 1, d, f), lambda i, te, nv: (layer, te[i], 0, 0)),
            pl.BlockSpec((1, 1, f, d), lambda i, te, nv: (layer, te[i], 0, 0)),
        ],
        out_specs=pl.BlockSpec((tm, d), lambda i, te, nv: (i, 0)),
    )
    return pl.pallas_call(
        _experts_kernel, grid_spec=grid_spec,
        out_shape=jax.ShapeDtypeStruct((p, d), F32),
        compiler_params=_params("arbitrary"), name="moe_experts",
    )(tile_expert, n_valid, xs, w_gate, w_up, w_down)


def _take_rows(x, idx):
    return x.at[idx].get(mode="promise_in_bounds", unique_indices=True)


def hier_moe(xn, route, counts, w_gate, w_up, w_down, layer, part_sizes, tm=256):
    n, d = xn.shape
    e_idx = route[:, :TOP_K].astype(jnp.int32)
    rank = route[:, 2 * TOP_K:3 * TOP_K].astype(jnp.int32)
    cnt = counts[0, N_GROUPS:N_GROUPS + N_EXPERTS].astype(jnp.int32)
    padded = ((cnt + tm - 1) // tm) * tm
    pad_ends = jnp.cumsum(padded)
    pad_starts = pad_ends - padded
    experts = jnp.arange(N_EXPERTS, dtype=jnp.int32)
    pos = jnp.sum(jnp.where(e_idx[..., None] == experts, pad_starts, 0), axis=-1) + rank
    p = n * TOP_K + N_EXPERTS * tm
    n_tiles = p // tm
    tok = jnp.broadcast_to(jnp.arange(n, dtype=jnp.int32)[:, None], (n, TOP_K))
    row_tok = (jnp.arange(p, dtype=jnp.int32) % n).at[pos.reshape(-1)].set(
        tok.reshape(-1), unique_indices=True, mode="promise_in_bounds")
    n_valid = (pad_ends[-1] // tm).astype(jnp.int32)
    tile_start = jnp.minimum(jnp.arange(n_tiles, dtype=jnp.int32), n_valid - 1) * tm
    tile_e = jnp.sum((tile_start[:, None] >= pad_ends[None, :]).astype(jnp.int32), axis=-1)
    tile_e = jnp.minimum(tile_e, N_EXPERTS - 1)
    xs = xn.at[row_tok].get(mode="promise_in_bounds")
    ys = moe_experts(xs, tile_e, n_valid.reshape(1), w_gate, w_up, w_down, layer, tm)
    parts, lo = [], 0
    for sz in part_sizes:
        parts.append(tuple(_take_rows(ys, pos[lo:lo + sz, kk]) for kk in range(TOP_K)))
        lo += sz
    return parts


def _final_norm_kernel(h_ref, rt_ref, a_ref, b_ref, nw_ref, o_ref):
    h = _add_expert_outputs(h_ref[...], rt_ref, (a_ref, b_ref))
    o_ref[...] = h * lax.rsqrt(jnp.mean(h * h, axis=-1, keepdims=True) + EPS) * nw_ref[...]


def final_norm(h, route, a, b, norm_w, tm=256):
    n, d = h.shape
    tm = min(tm, n)
    row = pl.BlockSpec((tm, d), lambda i: (i, 0))
    return pl.pallas_call(
        _final_norm_kernel, grid=(n // tm,),
        in_specs=[row, pl.BlockSpec((tm, LANES), lambda i: (i, 0)), row, row, pl.BlockSpec((1, d), lambda i: (0, 0))],
        out_specs=row, out_shape=jax.ShapeDtypeStruct((n, d), F32),
        compiler_params=_params("parallel"), name="final_norm",
    )(h, route, a, b, norm_w.reshape(1, d))


def kernel(x_prompt, x_sample, state_a_conv, state_a_ssm, cache_b_k, cache_b_v, cache_c_k, cache_c_v, page_table, norm_mix, norm_ffn, norm_final, rel_bias, w_in0, conv_w, a_log, dt_bias, gdn_norm_w, diff_lambda, diff_subln_w, w_out0, w_in1, b_in1, sinks, w_out1, w_group, b_group, w_router, b_router, w_gate, w_up, w_down):
    bp, tp, d = x_prompt.shape
    sn, sl, _ = x_sample.shape
    n_p, n_s = bp * tp, sn * sl
    parts = ((x_prompt.reshape(n_p, d), bp, tp), (x_sample.reshape(n_s, d), sn, sl))

    off_z = CONV_CH
    off_a = off_z + H_A * DV_A
    off_qb = off_a + 2 * H_A
    w0 = jnp.concatenate([w_in0[:, :off_a], w_in0[:, off_a:off_qb],
                          jnp.zeros((d, LANES - 2 * H_A), w_in0.dtype), w_in0[:, off_qb:]], axis=1).astype(BF16)
    widths0 = (CONV_CH, H_A * DV_A, LANES, B_W, B_W, B_W)
    w_out0_bf = w_out0.astype(BF16)
    w_out0_parts = [w_out0_bf[:H_A * DV_A], w_out0_bf[H_A * DV_A:]]

    def moe_layer(layer, hs, xns, routes, counts):
        xn_all = jnp.concatenate(xns, axis=0)
        route_all = jnp.concatenate(routes, axis=0)
        return hier_moe(xn_all, route_all, counts, w_gate, w_up, w_down, layer, [h.shape[0] for h in hs])

    conv_states = (jnp.zeros((bp, CONV_W - 1, CONV_CH), F32), state_a_conv)
    ssm_states = (jnp.zeros((bp, H_A, DK_A, DV_A), F32), state_a_ssm)
    hs, xns, routes, conv_new, ssm_new, kb_new, vb_new = [], [], [], [], [], [], []
    counts = jnp.zeros((1, LANES), F32)
    for idx, (x2, b, t) in enumerate(parts):
        qkv, z, ab, qb, kb, vb = norm_proj(x2, [], norm_mix[0], w0, None, widths0, emit_h=False)
        r3 = lambda a: a.reshape(b, t, a.shape[-1])
        o_a, s_new = gdn_heads(r3(qkv), r3(z), r3(ab), conv_states[idx], ssm_states[idx], conv_w, a_log, dt_bias,
                               gdn_norm_w, min(GDN_CHUNK, t), 2 if idx == 0 else 8)
        if idx == 0:
            o_b = diff_attn_prompt(r3(qb), r3(kb), r3(vb), rel_bias, diff_lambda, diff_subln_w, 512)
        else:
            o_b = diff_attn_sample(r3(qb), r3(kb), r3(vb), cache_b_k, cache_b_v, page_table, rel_bias, diff_lambda,
                                   diff_subln_w, 16)
        h1, xn, route, counts = out_route(x2, [o_a.reshape(b * t, -1), o_b.reshape(b * t, -1)], w_out0_parts,
                                          norm_ffn[0], w_group[0], b_group[0], w_router[0], b_router[0], counts)
        hs.append(h1), xns.append(xn), routes.append(route)
        conv_new.append(r3(qkv)[:, t - (CONV_W - 1):, :])
        ssm_new.append(s_new)
        kb_new.append(kb.reshape(b, t, H_B, 2 * DH_B))
        vb_new.append(vb.reshape(b, t, H_B, 2 * DH_B))
    ys = moe_layer(0, hs, xns, routes, counts)

    w1 = w_in1.astype(BF16)
    w_out1_bf = w_out1.astype(BF16)
    routes0 = routes
    hs2, xns, routes, kc_new, vc_new = [], [], [], [], []
    counts = jnp.zeros((1, LANES), F32)
    for idx, (_, b, t) in enumerate(parts):
        h2, q, k, v = norm_proj(hs[idx], [routes0[idx], *ys[idx]], norm_mix[1], w1, b_in1, (Q_C, KV_C, KV_C),
                                emit_h=True)
        r3 = lambda a: a.reshape(b, t, a.shape[-1])
        if idx == 0:
            o_c = swa_prompt(r3(q), r3(k), r3(v), rel_bias, sinks)
            kc_new.append(r3(k)[:, t - WINDOW:].reshape(b, WINDOW, H_KV_C, DH_C))
            vc_new.append(r3(v)[:, t - WINDOW:].reshape(b, WINDOW, H_KV_C, DH_C))
        else:
            o_c = swa_sample(r3(q), r3(k), r3(v), cache_c_k, cache_c_v, rel_bias, sinks, 8)
            kc_new.append(jnp.concatenate([cache_c_k[:, t:], k.reshape(b, t, H_KV_C, DH_C)], axis=1))
            vc_new.append(jnp.concatenate([cache_c_v[:, t:], v.reshape(b, t, H_KV_C, DH_C)], axis=1))
        h3, xn, route, counts = out_route(h2, [o_c.reshape(b * t, -1)], [w_out1_bf], norm_ffn[1],
                                          w_group[1], b_group[1], w_router[1], b_router[1], counts)
        hs2.append(h3), xns.append(xn), routes.append(route)
    ys = moe_layer(1, hs2, xns, routes, counts)

    y_out = [final_norm(hs2[idx], routes[idx], ys[idx][0], ys[idx][1], norm_final).reshape(b, t, d)
             for idx, (_, b, t) in enumerate(parts)]
    return (y_out[0], y_out[1], conv_new[0], conv_new[1], ssm_new[0], ssm_new[1],
            kb_new[0], vb_new[0], kb_new[1], vb_new[1], kc_new[0], vc_new[0], kc_new[1], vc_new[1])
```

```python
import functools
import math

import jax
import jax.numpy as jnp
import numpy as np
from jax import lax
from jax.experimental import pallas as pl
from jax.experimental.pallas import tpu as pltpu

F32 = jnp.float32
BF16 = jnp.bfloat16
EPS = 1e-6
NEG = -1e30
LOG2E = math.log2(math.e)
LANES = 128
VMEM_LIMIT = 56 * 1024 * 1024

D_MODEL = 1024
H_A, DK_A, DV_A, CONV_W = 4, 128, 128, 4
CONV_CH = 2 * H_A * DK_A + H_A * DV_A
GDN_CHUNK = 64
GDN_PRECISE_INV = False
H_B, DH_B = 4, 64
B_W = H_B * 2 * DH_B
PAGE = 128
H_C, H_KV_C, DH_C, WINDOW = 16, 2, 64, 128
G_C = H_C // H_KV_C
Q_C = H_C * DH_C
KV_C = H_KV_C * DH_C
N_BUCKETS, MAX_DIST = 32, 128
N_GROUPS, EXP_PER_GROUP, TOP_K, D_EXPERT = 4, 8, 2, 256
N_EXPERTS = N_GROUPS * EXP_PER_GROUP
LAM_INIT0 = 0.8 - 0.6 * math.exp(-0.3 * 0)


def _params(*sem):
    return pltpu.CompilerParams(dimension_semantics=sem, vmem_limit_bytes=VMEM_LIMIT)


def _mm(a, b, precise=False):
    if precise:
        return jnp.dot(a.astype(F32), b.astype(F32), preferred_element_type=F32, precision=lax.Precision.HIGHEST)
    return jnp.dot(a.astype(BF16), b.astype(BF16), preferred_element_type=F32)


def _mm_nt(a, b, precise=False):
    dn = (((1,), (1,)), ((), ()))
    if precise:
        return lax.dot_general(a.astype(F32), b.astype(F32), dn, preferred_element_type=F32,
                               precision=lax.Precision.HIGHEST)
    return lax.dot_general(a.astype(BF16), b.astype(BF16), dn, preferred_element_type=F32)


def _mm_tn(a, b, precise=False):
    dn = (((0,), (0,)), ((), ()))
    if precise:
        return lax.dot_general(a.astype(F32), b.astype(F32), dn, preferred_element_type=F32,
                               precision=lax.Precision.HIGHEST)
    return lax.dot_general(a.astype(BF16), b.astype(BF16), dn, preferred_element_type=F32)


def _sigmoid(x):
    return 1.0 / (1.0 + jnp.exp(-x))


def _silu(x):
    return x * _sigmoid(x)


def _softplus(x):
    return jnp.maximum(x, 0.0) + jnp.log(1.0 + jnp.exp(-jnp.abs(x)))


def _add_expert_outputs(h, route_ref, y_refs):
    rt = route_ref[...]
    for kk, y_ref in enumerate(y_refs):
        h = h + rt[:, TOP_K + kk:TOP_K + kk + 1] * y_ref[...]
    return h


def _norm_proj_kernel(n_add, widths, has_bias, emit_h, head_split, *refs):
    x_ref = refs[0]
    add_refs = refs[1:1 + n_add]
    nw_ref, w_ref = refs[1 + n_add], refs[2 + n_add]
    pos = 3 + n_add
    b_ref = None
    if has_bias:
        b_ref = refs[pos]
        pos += 1
    outs = refs[pos:]
    h = x_ref[...]
    if n_add:
        h = _add_expert_outputs(h, add_refs[0], add_refs[1:])
    if emit_h:
        outs[0][...] = h
        outs = outs[1:]
    split_refs = outs[len(widths):]
    xn = h * lax.rsqrt(jnp.mean(h * h, axis=-1, keepdims=True) + EPS) * nw_ref[...]
    xb = xn.astype(BF16)
    off = 0
    for j, (o_ref, wd) in enumerate(zip(outs, widths)):
        y = jnp.dot(xb, w_ref[:, off:off + wd], preferred_element_type=F32)
        if has_bias:
            y = y + b_ref[:, off:off + wd]
        o_ref[...] = y
        if j in head_split:
            s_ref = split_refs[head_split.index(j)]
            for hh in range(wd // LANES):
                s_ref[:, hh, :] = y[:, hh * LANES:(hh + 1) * LANES]
        off += wd


def norm_proj(x, addends, norm_w, w_bf16, bias, widths, emit_h, head_split=(), add_row0=0, tm=256):
    n, d = x.shape
    m = w_bf16.shape[1]
    tm = min(tm, n)
    assert sum(widths) == m and n % tm == 0
    row = lambda i: (i, 0)
    fixed = lambda i: (0, 0)
    assert add_row0 % tm == 0
    add_row = lambda i: (i + add_row0 // tm, 0)
    in_specs = [pl.BlockSpec((tm, d), row)] + [pl.BlockSpec((tm, a.shape[1]), add_row) for a in addends]
    in_specs += [pl.BlockSpec((1, d), fixed), pl.BlockSpec((d, m), fixed)]
    args = [x, *addends, norm_w.reshape(1, d), w_bf16]
    if bias is not None:
        in_specs.append(pl.BlockSpec((1, m), fixed))
        args.append(bias.reshape(1, m))
    out_shape, out_specs = [], []
    if emit_h:
        out_shape.append(jax.ShapeDtypeStruct((n, d), F32))
        out_specs.append(pl.BlockSpec((tm, d), row))
    for wd in widths:
        out_shape.append(jax.ShapeDtypeStruct((n, wd), F32))
        out_specs.append(pl.BlockSpec((tm, wd), row))
    for j in head_split:
        out_shape.append(jax.ShapeDtypeStruct((n, widths[j] // LANES, LANES), F32))
        out_specs.append(pl.BlockSpec((tm, widths[j] // LANES, LANES), lambda i: (i, 0, 0)))
    return pl.pallas_call(
        functools.partial(_norm_proj_kernel, len(addends), tuple(widths), bias is not None, emit_h,
                          tuple(head_split)),
        grid=(n // tm,), in_specs=in_specs, out_specs=out_specs, out_shape=out_shape,
        compiler_params=_params("parallel"), name="norm_proj",
    )(*args)


def _gdn_kernel(c_len, nb, qkv_ref, z_ref, ab_ref, cbuf_ref, s0_ref, convw_ref, gpar_ref, nw_ref,
                o_ref, sfin_ref, xp_scr, s_scr):
    c = pl.program_id(1)
    n_c = pl.num_programs(1)
    hist = CONV_W - 1
    base = 8 - hist

    @pl.when(c == 0)
    def _():
        xp_scr[:, base:8, :] = cbuf_ref[...]
        s_scr[...] = s0_ref[...]

    ri = lax.broadcasted_iota(jnp.int32, (c_len, c_len), 0)
    ci = lax.broadcasted_iota(jnp.int32, (c_len, c_len), 1)
    lower = ri >= ci
    strict = ri > ci
    eye = (ri == ci).astype(F32)
    lower_f = lower.astype(F32)

    units = []
    for bi in range(nb):
        xp_scr[bi, 8:8 + c_len, :] = qkv_ref[bi]
        y = xp_scr[bi, base:base + c_len, :] * convw_ref[0:1, :]
        for j in range(1, CONV_W):
            y = y + xp_scr[bi, base + j:base + j + c_len, :] * convw_ref[j:j + 1, :]
        xp_scr[bi, base:8, :] = xp_scr[bi, base + c_len:8 + c_len, :]
        y = _silu(y)

        ab = ab_ref[bi]
        g_t = gpar_ref[0:1, :] * _softplus(ab + gpar_ref[1:2, :])
        beta_t = _sigmoid(ab)
        gcum_t = _mm(lower_f, g_t, precise=True)
        gcum_tt = gcum_t.T

        for h in range(H_A):
            q = y[:, h * DK_A:(h + 1) * DK_A]
            k = y[:, (H_A + h) * DK_A:(H_A + h + 1) * DK_A]
            v = y[:, 2 * H_A * DK_A + h * DV_A:2 * H_A * DK_A + (h + 1) * DV_A]
            q = q * lax.rsqrt(jnp.sum(q * q, axis=-1, keepdims=True) + EPS) * (DK_A ** -0.5)
            k = k * lax.rsqrt(jnp.sum(k * k, axis=-1, keepdims=True) + EPS)
            gc = gcum_t[:, h:h + 1]
            gr = gcum_tt[h:h + 1, :]
            beta = beta_t[:, H_A + h:H_A + h + 1]
            decay = jnp.where(lower, jnp.exp(jnp.where(lower, gc - gr, 0.0)), 0.0)
            kb = k * beta
            units.append(dict(bi=bi, h=h, q=q, k=k, k_bf=k.astype(BF16), gc=gc, decay=decay, kb=kb,
                              rhs=jnp.concatenate([v * beta, kb * jnp.exp(gc)], axis=1)))

    for un in units:
        un["nmat"] = jnp.where(strict, _mm_nt(un["kb"], un["k_bf"]) * un["decay"], 0.0)
        un["qk"] = jnp.where(lower, _mm_nt(un["q"], un["k_bf"]) * un["decay"], 0.0)
    for un in units:
        un["inv"] = eye - un["nmat"]
        un["pw"] = _mm(un["nmat"], un["nmat"], GDN_PRECISE_INV)
    span = 2
    while span < c_len:
        for un in units:
            un["inv"] = un["inv"] + _mm(un["inv"], un["pw"], GDN_PRECISE_INV)
        span *= 2
        if span < c_len:
            for un in units:
                un["pw"] = _mm(un["pw"], un["pw"], GDN_PRECISE_INV)
    for un in units:
        sol = _mm(un["inv"], un["rhs"], GDN_PRECISE_INV)
        un["u"], un["w"] = sol[:, :DV_A], sol[:, DV_A:]
        un["s"] = s_scr[un["bi"], un["h"]]
        un["s_bf"] = un["s"].astype(BF16)
    for un in units:
        un["v_new"] = un["u"] - _mm(un["w"], un["s_bf"])
        un["o_s"] = _mm(un["q"] * jnp.exp(un["gc"]), un["s_bf"])
    for un in units:
        bi, h, gc = un["bi"], un["h"], un["gc"]
        o = un["o_s"] + _mm(un["qk"], un["v_new"])
        g_last = gc[c_len - 1:c_len, :]
        s_scr[bi, h] = un["s"] * jnp.exp(g_last) + _mm_tn(un["k"] * jnp.exp(g_last - gc), un["v_new"])
        o = o * lax.rsqrt(jnp.mean(o * o, axis=-1, keepdims=True) + EPS) * nw_ref[...]
        o_ref[bi, :, h * DV_A:(h + 1) * DV_A] = o * _silu(z_ref[bi, :, h * DV_A:(h + 1) * DV_A])

    @pl.when(c == n_c - 1)
    def _():
        sfin_ref[...] = s_scr[...]


def gdn_heads(qkv, z, ab, conv_buf, s0, conv_w, a_log, dt_bias, norm_w, c_len, nb):
    b, t, _ = qkv.shape
    assert t % c_len == 0 and b % nb == 0
    convw_t = conv_w.T
    gpar = jnp.zeros((2, LANES), F32)
    gpar = gpar.at[0, :H_A].set(-jnp.exp(a_log.astype(F32))).at[1, :H_A].set(dt_bias.astype(F32))
    tok = lambda i, j: (i, j, 0)
    seq3 = lambda i, j: (i, 0, 0)
    fixed = lambda i, j: (0, 0)
    return pl.pallas_call(
        functools.partial(_gdn_kernel, c_len, nb),
        grid=(b // nb, t // c_len),
        in_specs=[
            pl.BlockSpec((nb, c_len, CONV_CH), tok),
            pl.BlockSpec((nb, c_len, H_A * DV_A), tok),
            pl.BlockSpec((nb, c_len, LANES), tok),
            pl.BlockSpec((nb, CONV_W - 1, CONV_CH), seq3),
            pl.BlockSpec((nb, H_A, DK_A, DV_A), lambda i, j: (i, 0, 0, 0)),
            pl.BlockSpec((CONV_W, CONV_CH), fixed),
            pl.BlockSpec((2, LANES), fixed),
            pl.BlockSpec((1, DV_A), fixed),
        ],
        out_specs=[
            pl.BlockSpec((nb, c_len, H_A * DV_A), tok),
            pl.BlockSpec((nb, H_A, DK_A, DV_A), lambda i, j: (i, 0, 0, 0)),
        ],
        out_shape=[
            jax.ShapeDtypeStruct((b, t, H_A * DV_A), F32),
            jax.ShapeDtypeStruct((b, H_A, DK_A, DV_A), F32),
        ],
        scratch_shapes=[
            pltpu.VMEM((nb, 8 + c_len, CONV_CH), F32),
            pltpu.VMEM((nb, H_A, DK_A, DV_A), F32),
        ],
        compiler_params=_params("parallel", "arbitrary"), name="gdn_heads",
    )(qkv, z, ab, conv_buf, s0, convw_t, gpar, norm_w.reshape(1, DV_A))


def _t5_bucket_np(dist):
    d = np.maximum(dist, 0)
    max_exact = N_BUCKETS // 2
    ratio = (np.log(np.maximum(d, 1).astype(np.float32) / np.float32(max_exact))
             / np.float32(math.log(MAX_DIST / max_exact)))
    large = np.minimum(max_exact + (ratio * (N_BUCKETS - max_exact)).astype(np.int32), N_BUCKETS - 1)
    return np.where(d < max_exact, d, large).astype(np.int32)


FAR_BUCKET = int(_t5_bucket_np(np.array([MAX_DIST]))[0])
assert np.all(_t5_bucket_np(np.arange(MAX_DIST, 4 * MAX_DIST)) == FAR_BUCKET)


def _bias_kernel(col_group, bucket_ref, rb_ref, out_ref):
    for col, grp in enumerate(col_group):
        bk = bucket_ref[grp]
        acc = jnp.full(bk.shape, NEG, F32)
        for b in range(N_BUCKETS):
            acc = jnp.where(bk == b, rb_ref[b, col], acc)
        out_ref[col] = acc


def bias_tiles(rel_bias, dist, valid, n_cols, col_group=None):
    bucket = np.where(valid, _t5_bucket_np(dist), -1).astype(np.int32)
    if bucket.ndim == 2:
        bucket, col_group = bucket[None], (0,) * n_cols
    _, r, c = bucket.shape
    return pl.pallas_call(
        functools.partial(_bias_kernel, tuple(col_group)),
        in_specs=[pl.BlockSpec(memory_space=pltpu.VMEM), pl.BlockSpec(memory_space=pltpu.SMEM)],
        out_specs=pl.BlockSpec(memory_space=pltpu.VMEM),
        out_shape=jax.ShapeDtypeStruct((n_cols, r, c), F32),
        name="bias_tiles",
    )(jnp.asarray(bucket), rel_bias.astype(F32))


def _diff_lambda(lam_ref):
    lp = lam_ref[...]
    s1 = jnp.sum(lp[0:1] * lp[1:2], axis=-1, keepdims=True)
    s2 = jnp.sum(lp[2:3] * lp[3:4], axis=-1, keepdims=True)
    return jnp.exp(s1) - jnp.exp(s2) + LAM_INIT0


def _diff_finish(o1, o2, lam, nw):
    o = o1 - lam * o2
    return o * lax.rsqrt(jnp.mean(o * o, axis=-1, keepdims=True) + EPS) * nw * (1.0 - LAM_INIT0)


def _softmax_update(s, v_bf, m_ref, l_ref, acc_ref):
    m_prev = m_ref[...]
    m_new = jnp.maximum(m_prev, jnp.max(s, axis=-1, keepdims=True))
    alpha = jnp.exp(m_prev - m_new)
    p = jnp.exp(s - m_new)
    l_ref[...] = alpha * l_ref[...] + jnp.sum(p, axis=-1, keepdims=True)
    acc_ref[...] = alpha * acc_ref[...] + jnp.dot(p.astype(BF16), v_bf, preferred_element_type=F32)
    m_ref[...] = m_new


def _diff_prompt_kernel(blk, qt_ref, kt_ref, q_ref, k_ref, v_ref, t0_ref, t1_ref, far_ref, lam_ref, nw_ref, o_ref,
                        m_scr, acc_scr):
    qi = qt_ref[pl.program_id(1)]
    ki = kt_ref[pl.program_id(1)]
    nsub = blk // LANES
    hw = 2 * DH_B

    @pl.when(ki == 0)
    def _():
        m_scr[...] = jnp.full(m_scr.shape, NEG, F32)
        acc_scr[...] = jnp.zeros(acc_scr.shape, F32)

    def bias_block(case, hm):
        const = jnp.full((LANES, LANES), far_ref[hm], F32)
        neg = jnp.full((LANES, LANES), NEG, F32)
        rows = []
        for jj in range(nsub):
            tiles = []
            for ii in range(nsub):
                sub = ii - jj + (nsub if case == "prev" else 0)
                tiles.append(neg if sub < 0 else t0_ref[hm] if sub == 0 else t1_ref[hm] if sub == 1 else const)
            rows.append(jnp.concatenate(tiles, axis=1))
        return jnp.concatenate(rows, axis=0)

    def step(case):
        q = (q_ref[0] * (DH_B ** -0.5 * LOG2E)).astype(BF16)
        k = k_ref[0].astype(BF16)
        v_t = v_ref[0].T
        first = lax.broadcasted_iota(jnp.int32, (1, hw), 1) < DH_B
        ones = jnp.ones((hw, blk), F32)
        s_all, v_ext = [], []
        for h in range(H_B):
            sl = slice(h * hw, (h + 1) * hw)
            qh, kh = q[:, sl], k[:, sl]
            v_ext.append(jnp.concatenate([v_t[sl], ones], axis=0).astype(BF16))
            for mp in range(2):
                km = jnp.where(first if mp == 0 else jnp.logical_not(first), kh, jnp.zeros_like(kh))
                s_all.append(lax.dot_general(km, qh, (((1,), (1,)), ((), ())),
                                             preferred_element_type=F32))
        p_all, alpha_all = [], []
        for hm, s in enumerate(s_all):
            m_prev = m_scr[hm]
            if case == "far":
                shift = far_ref[hm]
                m_new = jnp.maximum(m_prev, jnp.max(s, axis=0, keepdims=True) + shift)
                p = jnp.exp2(s + (shift - m_new))
            else:
                s = s + bias_block(case, hm)
                m_new = jnp.maximum(m_prev, jnp.max(s, axis=0, keepdims=True))
                p = jnp.exp2(s - m_new)
            alpha_all.append(jnp.exp2(m_prev - m_new))
            p_all.append(p.astype(BF16))
            m_scr[hm] = m_new
        for hm, (p, alpha) in enumerate(zip(p_all, alpha_all)):
            acc_scr[hm] = alpha * acc_scr[hm] + jnp.dot(v_ext[hm // 2], p, preferred_element_type=F32)

    @pl.when(ki < qi - 1)
    def _():
        step("far")

    @pl.when(ki == qi - 1)
    def _():
        step("prev")

    @pl.when(ki == qi)
    def _():
        step("diag")
        lam = _diff_lambda(lam_ref)
        for h in range(H_B):
            a1, a2 = acc_scr[2 * h], acc_scr[2 * h + 1]
            o = a1[:hw] / a1[hw:hw + 1] - lam * (a2[:hw] / a2[hw:hw + 1])
            o = o * lax.rsqrt(jnp.mean(o * o, axis=0, keepdims=True) + EPS) * nw_ref[...] * (1.0 - LAM_INIT0)
            o_ref[0, :, h * hw:(h + 1) * hw] = o.T


def diff_attn_prompt(q, k, v, rel_bias, diff_lambda, subln_w, blk):
    b, t, _ = q.shape
    assert t % blk == 0 and blk % LANES == 0 and blk >= 2 * LANES
    i = np.arange(LANES)
    d0 = i[None, :] - i[:, None]
    t0 = bias_tiles(rel_bias, d0, d0 >= 0, 2 * H_B) * LOG2E
    t1 = bias_tiles(rel_bias, d0 + LANES, np.ones_like(d0, bool), 2 * H_B) * LOG2E
    far = rel_bias[FAR_BUCKET, :2 * H_B].astype(F32) * LOG2E
    nb = t // blk
    hw = 2 * DH_B
    pairs = [(qi, ki) for qi in range(nb) for ki in range(qi + 1)]
    q_tab = jnp.asarray(np.array([p[0] for p in pairs], np.int32))
    k_tab = jnp.asarray(np.array([p[1] for p in pairs], np.int32))
    whole = lambda shape: pl.BlockSpec(shape, lambda bi, pi, qt, kt: (0,) * len(shape))
    grid_spec = pltpu.PrefetchScalarGridSpec(
        num_scalar_prefetch=2,
        grid=(b, len(pairs)),
        in_specs=[
            pl.BlockSpec((1, blk, B_W), lambda bi, pi, qt, kt: (bi, qt[pi], 0)),
            pl.BlockSpec((1, blk, B_W), lambda bi, pi, qt, kt: (bi, kt[pi], 0)),
            pl.BlockSpec((1, blk, B_W), lambda bi, pi, qt, kt: (bi, kt[pi], 0)),
            whole((2 * H_B, LANES, LANES)),
            whole((2 * H_B, LANES, LANES)),
            pl.BlockSpec(memory_space=pltpu.SMEM),
            whole((4, DH_B)),
            whole((hw, 1)),
        ],
        out_specs=pl.BlockSpec((1, blk, B_W), lambda bi, pi, qt, kt: (bi, qt[pi], 0)),
        scratch_shapes=[
            pltpu.VMEM((2 * H_B, 1, blk), F32),
            pltpu.VMEM((2 * H_B, 2 * hw, blk), F32),
        ],
    )
    return pl.pallas_call(
        functools.partial(_diff_prompt_kernel, blk),
        grid_spec=grid_spec,
        out_shape=jax.ShapeDtypeStruct((b, t, B_W), F32),
        compiler_params=_params("parallel", "arbitrary"), name="diff_attn_prompt",
    )(q_tab, k_tab, q, k, v, t0, t1, far, diff_lambda.astype(F32), subln_w.reshape(hw, 1).astype(F32))


def _diff_sample_kernel(n_pg, s_len, pt_ref, q_ref, kn_ref, vn_ref, *refs):
    k_pages = refs[:n_pg]
    v_pages = refs[n_pg:2 * n_pg]
    blast_ref, bnew_ref, far_ref, lam_ref, nw_ref, o_ref, m_scr, l_scr, acc_scr = refs[2 * n_pg:]
    j = pl.program_id(1)
    last = pl.num_programs(1) - 1
    hw = 2 * DH_B

    @pl.when(j == 0)
    def _():
        m_scr[...] = jnp.full(m_scr.shape, NEG, F32)
        l_scr[...] = jnp.zeros(l_scr.shape, F32)
        acc_scr[...] = jnp.zeros(acc_scr.shape, F32)

    q = q_ref[0] * (DH_B ** -0.5)
    first = lax.broadcasted_iota(jnp.int32, (1, hw), 1) < DH_B
    pieces = []
    for h in range(H_B):
        qh = q[:, h * hw:(h + 1) * hw]
        pieces += [jnp.where(first, qh, 0.0), jnp.where(first, 0.0, qh)]
    qbd = jnp.concatenate(pieces, axis=0).astype(BF16)

    n_grp = 4 if n_pg % 4 == 0 else 1
    per = n_pg // n_grp
    wg = per * PAGE * H_B
    s_grp = []
    for g in range(n_grp):
        kg = jnp.concatenate([r[0] for r in k_pages[g * per:(g + 1) * per]], axis=0).astype(BF16)
        bias = jnp.where(j == last, blast_ref[:, g * wg:(g + 1) * wg], far_ref[:, g * wg:(g + 1) * wg])
        s_grp.append(lax.dot_general(qbd, kg, (((1,), (1,)), ((), ())), preferred_element_type=F32) + bias)
    m_prev = m_scr[...]
    m_new = m_prev
    for s in s_grp:
        m_new = jnp.maximum(m_new, jnp.max(s, axis=-1, keepdims=True))
    alpha = jnp.exp(m_prev - m_new)
    l_new = alpha * l_scr[...]
    acc_new = alpha * acc_scr[...]
    for g, s in enumerate(s_grp):
        p = jnp.exp(s - m_new)
        vg = jnp.concatenate([r[0] for r in v_pages[g * per:(g + 1) * per]], axis=0).astype(BF16)
        l_new = l_new + jnp.sum(p, axis=-1, keepdims=True)
        acc_new = acc_new + jnp.dot(p.astype(BF16), vg, preferred_element_type=F32)
    m_scr[...] = m_new
    l_scr[...] = l_new
    acc_scr[...] = acc_new

    @pl.when(j == last)
    def _():
        pad = jnp.zeros((PAGE - H_B * s_len, hw), F32)
        kn = jnp.concatenate([kn_ref[0, :, h * hw:(h + 1) * hw] for h in range(H_B)] + [pad], axis=0)
        vn = jnp.concatenate([vn_ref[0, :, h * hw:(h + 1) * hw] for h in range(H_B)] + [pad], axis=0)
        s2 = lax.dot_general(qbd, kn.astype(BF16), (((1,), (1,)), ((), ())), preferred_element_type=F32)
        _softmax_update(s2 + bnew_ref[...], vn.astype(BF16), m_scr, l_scr, acc_scr)
        accn = acc_scr[...] / l_scr[...]
        lam = _diff_lambda(lam_ref)
        for h in range(H_B):
            o1 = accn[(2 * h) * s_len:(2 * h + 1) * s_len]
            o2 = accn[(2 * h + 1) * s_len:(2 * h + 2) * s_len]
            o_ref[0, :, h * hw:(h + 1) * hw] = _diff_finish(o1, o2, lam, nw_ref[...])


def diff_attn_sample(q, k, v, cache_k, cache_v, page_table, rel_bias, diff_lambda, subln_w, n_pg):
    s_n, s_len, _ = q.shape
    pages_per_seq = page_table.shape[1]
    assert pages_per_seq % n_pg == 0
    past = pages_per_seq * PAGE
    n_phys = cache_k.shape[0]
    hw = 2 * DH_B
    ck = cache_k.reshape(n_phys, PAGE * H_B, hw)
    cv = cache_v.reshape(n_phys, PAGE * H_B, hw)
    n_rows = 2 * H_B * s_len
    blk_keys = n_pg * PAGE
    blk_rows = blk_keys * H_B
    assert PAGE >= H_B * s_len
    t = np.arange(s_len)
    hh = np.arange(H_B)
    col_group = tuple(c // 2 for c in range(2 * H_B))
    key_of_col = np.repeat(np.arange(blk_keys), H_B)
    head_of_col = np.tile(hh, blk_keys)
    own = np.broadcast_to((head_of_col[None, None, :] == hh[:, None, None]), (H_B, s_len, blk_rows))
    d_last = np.broadcast_to(((past + t)[:, None] - (past - blk_keys + key_of_col)[None, :])[None],
                             (H_B, s_len, blk_rows))
    blast = bias_tiles(rel_bias, d_last, own, 2 * H_B, col_group).reshape(n_rows, blk_rows)
    far = bias_tiles(rel_bias, np.full((H_B, s_len, blk_rows), MAX_DIST), own, 2 * H_B, col_group)
    far = far.reshape(n_rows, blk_rows)
    cnew = np.arange(PAGE)
    d_new = np.broadcast_to((t[:, None] - (cnew % s_len)[None, :])[None], (H_B, s_len, PAGE))
    ok_new = ((cnew // s_len)[None, None, :] == hh[:, None, None]) & (d_new >= 0) & (cnew < H_B * s_len)[None, None, :]
    bnew = bias_tiles(rel_bias, d_new, ok_new, 2 * H_B, col_group).reshape(n_rows, PAGE)
    assert past - blk_keys + s_len - 1 >= 0 and np.all(_t5_bucket_np(np.array([blk_keys + 1])) == FAR_BUCKET)

    def page_spec(i):
        return pl.BlockSpec((1, PAGE * H_B, hw), lambda s, j, pt: (pt[s * pages_per_seq + j * n_pg + i], 0, 0))

    seq = pl.BlockSpec((1, s_len, B_W), lambda s, j, pt: (s, 0, 0))
    whole = lambda shape: pl.BlockSpec(shape, lambda s, j, pt: (0,) * len(shape))
    grid_spec = pltpu.PrefetchScalarGridSpec(
        num_scalar_prefetch=1,
        grid=(s_n, pages_per_seq // n_pg),
        in_specs=[seq, seq, seq] + [page_spec(i) for i in range(n_pg)] * 2 + [
            whole((n_rows, blk_rows)), whole((n_rows, PAGE)), whole((n_rows, blk_rows)),
            whole((4, DH_B)), whole((1, hw))],
        out_specs=seq,
        scratch_shapes=[
            pltpu.VMEM((n_rows, 1), F32),
            pltpu.VMEM((n_rows, 1), F32),
            pltpu.VMEM((n_rows, hw), F32),
        ],
    )
    return pl.pallas_call(
        functools.partial(_diff_sample_kernel, n_pg, s_len),
        grid_spec=grid_spec,
        out_shape=jax.ShapeDtypeStruct((s_n, s_len, B_W), F32),
        compiler_params=_params("parallel", "arbitrary"), name="diff_attn_sample",
    )(page_table.reshape(-1).astype(jnp.int32), q, k, v, *([ck] * n_pg), *([cv] * n_pg),
      blast, bnew, far, diff_lambda.astype(F32), subln_w.reshape(1, 2 * DH_B).astype(F32))


def _kv_variants(x, kv):
    lo = lax.broadcasted_iota(jnp.int32, (1, LANES), 1) < DH_C
    rolled = pltpu.roll(x, DH_C, 1)
    a_src, b_src = (x, rolled) if kv == 0 else (rolled, x)
    zero = jnp.zeros_like(x)
    return jnp.where(lo, a_src, zero).astype(BF16), jnp.where(lo, zero, b_src).astype(BF16)


def _sink_attend(units):
    scores = [[lax.dot_general(q_bf, kx, (((1,), (1,)), ((), ())), preferred_element_type=F32) + bias
               for kx, bias in zip(k_ab, bias_ab)] for q_bf, k_ab, _, bias_ab, _ in units]
    probs = []
    for s_ab, (_, _, _, _, sink_ab) in zip(scores, units):
        p_ab = []
        for s, sink in zip(s_ab, sink_ab):
            m = jnp.maximum(jnp.max(s, axis=-1, keepdims=True), sink)
            e = jnp.exp(s - m)
            p_ab.append((e / (jnp.sum(e, axis=-1, keepdims=True) + jnp.exp(sink - m))).astype(BF16))
        probs.append(p_ab)
    return [jnp.dot(p_ab[0], v_ab[0], preferred_element_type=F32) + jnp.dot(p_ab[1], v_ab[1], preferred_element_type=F32)
            for p_ab, (_, _, v_ab, _, _) in zip(probs, units)]


def _swa_prompt_kernel(q_ref, kp_ref, kc_ref, vp_ref, vc_ref, bias_ref, sink_ref, o_ref):
    n = pl.program_id(1)
    keys = jnp.concatenate([kp_ref[0], kc_ref[0]], axis=0)
    vals = jnp.concatenate([vp_ref[0], vc_ref[0]], axis=0)
    col = lax.broadcasted_iota(jnp.int32, (1, 2 * WINDOW), 1)
    first_blk = jnp.where((col < WINDOW) & (n == 0), NEG, 0.0)
    n_slot = G_C // 2
    units, slices = [], []
    for kv in range(H_KV_C):
        k_ab = _kv_variants(keys, kv)
        v_ab = _kv_variants(vals, kv)
        for ps in range(n_slot):
            h0 = kv * G_C + 2 * ps
            sl = slice(kv * G_C * DH_C + ps * LANES, kv * G_C * DH_C + (ps + 1) * LANES)
            q = (q_ref[0, :, sl] * (DH_C ** -0.5)).astype(BF16)
            bias_ab = (bias_ref[h0] + first_blk, bias_ref[h0 + 1] + first_blk)
            units.append((q, k_ab, v_ab, bias_ab, (sink_ref[h0], sink_ref[h0 + 1])))
            slices.append(sl)
    for sl, o in zip(slices, _sink_attend(units)):
        o_ref[0, :, sl] = o


def swa_prompt(q, k, v, rel_bias, sinks):
    b, t, _ = q.shape
    nb = t // WINDOW
    i = np.arange(WINDOW)
    j = np.arange(2 * WINDOW)
    dist = WINDOW + i[:, None] - j[None, :]
    bias = bias_tiles(rel_bias, dist, (dist >= 0) & (dist <= WINDOW), H_C)
    cur = lambda bi, n: (bi, n, 0)
    prev = lambda bi, n: (bi, jnp.maximum(n - 1, 0), 0)
    return pl.pallas_call(
        _swa_prompt_kernel,
        grid=(b, nb),
        in_specs=[
            pl.BlockSpec((1, WINDOW, Q_C), cur),
            pl.BlockSpec((1, WINDOW, KV_C), prev), pl.BlockSpec((1, WINDOW, KV_C), cur),
            pl.BlockSpec((1, WINDOW, KV_C), prev), pl.BlockSpec((1, WINDOW, KV_C), cur),
            pl.BlockSpec((H_C, WINDOW, 2 * WINDOW), lambda bi, n: (0, 0, 0)),
            pl.BlockSpec(memory_space=pltpu.SMEM),
        ],
        out_specs=pl.BlockSpec((1, WINDOW, Q_C), cur),
        out_shape=jax.ShapeDtypeStruct((b, t, Q_C), F32),
        compiler_params=_params("parallel", "parallel"), name="swa_prompt",
    )(q, k, k, v, v, bias, sinks.astype(F32))


def _swa_sample_kernel(gs, s_len, q_ref, kn_ref, vn_ref, ck_ref, cv_ref, bias_ref, sink_ref, o_ref):
    n_slot = G_C // 2
    pad = jnp.zeros((WINDOW - s_len, LANES), F32)
    units, where = [], []
    for si in range(gs):
        keys = jnp.concatenate([ck_ref[si], kn_ref[si], pad], axis=0)
        vals = jnp.concatenate([cv_ref[si], vn_ref[si], pad], axis=0)
        for kv in range(H_KV_C):
            k_ab = _kv_variants(keys, kv)
            v_ab = _kv_variants(vals, kv)
            base = kv * G_C * DH_C
            q = jnp.concatenate([q_ref[si, :, base + ps * LANES:base + (ps + 1) * LANES] for ps in range(n_slot)],
                                axis=0)
            q = (q * (DH_C ** -0.5)).astype(BF16)
            units.append((q, k_ab, v_ab, (bias_ref[kv, 0], bias_ref[kv, 1]), (sink_ref[kv, 0], sink_ref[kv, 1])))
            where.append((si, base))
    for (si, base), o in zip(where, _sink_attend(units)):
        for ps in range(n_slot):
            o_ref[si, :, base + ps * LANES:base + (ps + 1) * LANES] = o[ps * s_len:(ps + 1) * s_len]


def swa_sample(q, k, v, cache_k, cache_v, rel_bias, sinks, gs):
    s_n, s_len, _ = q.shape
    assert s_n % gs == 0 and cache_k.shape[1] == WINDOW
    ck = cache_k.reshape(s_n, WINDOW, KV_C)
    cv = cache_v.reshape(s_n, WINDOW, KV_C)
    tq = np.arange(s_len)
    j = np.arange(2 * WINDOW)
    dist = WINDOW + tq[:, None] - j[None, :]
    valid = (dist >= 0) & (dist <= WINDOW) & (j[None, :] < WINDOW + s_len)
    n_slot = G_C // 2
    bias = bias_tiles(rel_bias, dist, valid, H_C)
    bias = bias.reshape(H_KV_C, n_slot, 2, s_len, 2 * WINDOW).transpose(0, 2, 1, 3, 4)
    bias = bias.reshape(H_KV_C, 2, n_slot * s_len, 2 * WINDOW)
    sk = sinks.astype(F32).reshape(H_KV_C, n_slot, 2).transpose(0, 2, 1)
    sk = jnp.repeat(sk, s_len, axis=-1).reshape(H_KV_C, 2, n_slot * s_len, 1)
    grp = lambda i: (i, 0, 0)
    whole = lambda shape: pl.BlockSpec(shape, lambda i: (0,) * len(shape))
    return pl.pallas_call(
        functools.partial(_swa_sample_kernel, gs, s_len),
        grid=(s_n // gs,),
        in_specs=[
            pl.BlockSpec((gs, s_len, Q_C), grp),
            pl.BlockSpec((gs, s_len, KV_C), grp), pl.BlockSpec((gs, s_len, KV_C), grp),
            pl.BlockSpec((gs, WINDOW, KV_C), grp), pl.BlockSpec((gs, WINDOW, KV_C), grp),
            whole(bias.shape), whole(sk.shape),
        ],
        out_specs=pl.BlockSpec((gs, s_len, Q_C), grp),
        out_shape=jax.ShapeDtypeStruct((s_n, s_len, Q_C), F32),
        compiler_params=_params("parallel"), name="swa_sample",
    )(q, k, v, ck, cv, bias, sk)


ROUTE_BIG = 1 << 20


def _out_route_kernel(n_in, *refs):
    h_ref = refs[0]
    o_refs = refs[1:1 + n_in]
    w_refs = refs[1 + n_in:1 + 2 * n_in]
    nw_ref, wr_ref, br_ref, cin_ref, hn_ref, xn_ref, rt_ref, cout_ref, cnt_scr = refs[1 + 2 * n_in:]
    step = pl.program_id(0)

    @pl.when(step == 0)
    def _():
        cnt_scr[...] = cin_ref[...]

    h = h_ref[...]
    for o_ref, w_ref in zip(o_refs, w_refs):
        h = h + jnp.dot(o_ref[...].astype(BF16), w_ref[...], preferred_element_type=F32)
    hn_ref[...] = h
    xn = h * lax.rsqrt(jnp.mean(h * h, axis=-1, keepdims=True) + EPS) * nw_ref[...]
    xn_ref[...] = xn
    logit = jnp.dot(xn, wr_ref[...], preferred_element_type=F32, precision=lax.Precision.HIGHEST) + br_ref[...]
    lane = lax.broadcasted_iota(jnp.int32, logit.shape, 1)
    gmask = lane < N_GROUPS
    gl = jnp.where(gmask, logit, NEG)
    gmax = jnp.max(gl, axis=-1, keepdims=True)
    gsel = jnp.min(jnp.where(gl == gmax, lane, ROUTE_BIG), axis=-1, keepdims=True)
    gw = 1.0 / jnp.sum(jnp.where(gmask, jnp.exp(gl - gmax), 0.0), axis=-1, keepdims=True)
    eid = lane - N_GROUPS
    emask = (eid >= 0) & (eid < N_EXPERTS) & (jnp.right_shift(eid, 3) == gsel)
    el = jnp.where(emask, logit, NEG)
    v1 = jnp.max(el, axis=-1, keepdims=True)
    i1 = jnp.min(jnp.where((el == v1) & emask, lane, ROUTE_BIG), axis=-1, keepdims=True)
    emask2 = emask & (lane != i1)
    el2 = jnp.where(emask2, logit, NEG)
    v2 = jnp.max(el2, axis=-1, keepdims=True)
    i2 = jnp.min(jnp.where((el2 == v2) & emask2, lane, ROUTE_BIG), axis=-1, keepdims=True)
    e = jnp.exp(v2 - v1)
    w1 = gw / (1.0 + e)
    w2 = gw * e / (1.0 + e)
    tm = logit.shape[0]
    onehot = ((lane == i1) | (lane == i2)).astype(BF16)
    ri = lax.broadcasted_iota(jnp.int32, (tm, tm), 0)
    ci = lax.broadcasted_iota(jnp.int32, (tm, tm), 1)
    before = jnp.dot((ri > ci).astype(BF16), onehot, preferred_element_type=F32) + cnt_scr[...]
    r1 = jnp.sum(jnp.where(lane == i1, before, 0.0), axis=-1, keepdims=True)
    r2 = jnp.sum(jnp.where(lane == i2, before, 0.0), axis=-1, keepdims=True)
    cnt_scr[...] = cnt_scr[...] + jnp.sum(onehot.astype(F32), axis=0, keepdims=True)
    rt = jnp.where(lane == 0, (i1 - N_GROUPS).astype(F32),
                   jnp.where(lane == 1, (i2 - N_GROUPS).astype(F32),
                             jnp.where(lane == 2, w1, jnp.where(lane == 3, w2,
                                                                jnp.where(lane == 4, r1,
                                                                          jnp.where(lane == 5, r2, 0.0))))))
    rt_ref[...] = rt

    @pl.when(step == pl.num_programs(0) - 1)
    def _():
        cout_ref[...] = cnt_scr[...]


def out_route(h, outs, weights_bf16, norm_w, w_group, b_group, w_router, b_router, counts_in, tm=512):
    n, d = h.shape
    tm = min(tm, n)
    assert n % tm == 0 and EXP_PER_GROUP == 8
    wr = jnp.zeros((d, LANES), F32).at[:, :N_GROUPS].set(w_group).at[:, N_GROUPS:N_GROUPS + N_EXPERTS].set(w_router)
    br = jnp.zeros((1, LANES), F32).at[0, :N_GROUPS].set(b_group).at[0, N_GROUPS:N_GROUPS + N_EXPERTS].set(b_router)
    row = lambda i: (i, 0)
    fixed = lambda i: (0, 0)
    in_specs = [pl.BlockSpec((tm, d), row)]
    in_specs += [pl.BlockSpec((tm, o.shape[1]), row) for o in outs]
    in_specs += [pl.BlockSpec(w.shape, fixed) for w in weights_bf16]
    in_specs += [pl.BlockSpec((1, d), fixed), pl.BlockSpec((d, LANES), fixed), pl.BlockSpec((1, LANES), fixed),
                 pl.BlockSpec((1, LANES), fixed)]
    return pl.pallas_call(
        functools.partial(_out_route_kernel, len(outs)),
        grid=(n // tm,), in_specs=in_specs,
        out_specs=[pl.BlockSpec((tm, d), row), pl.BlockSpec((tm, d), row), pl.BlockSpec((tm, LANES), row),
                   pl.BlockSpec((1, LANES), fixed)],
        out_shape=[jax.ShapeDtypeStruct((n, d), F32), jax.ShapeDtypeStruct((n, d), F32),
                   jax.ShapeDtypeStruct((n, LANES), F32), jax.ShapeDtypeStruct((1, LANES), F32)],
        scratch_shapes=[pltpu.VMEM((1, LANES), F32)],
        compiler_params=_params("arbitrary"), name="out_route",
    )(h, *outs, *weights_bf16, norm_w.reshape(1, d), wr, br, counts_in)


def _experts_kernel(te_ref, nv_ref, x_ref, wg_ref, wu_ref, wd_ref, y_ref):
    i = pl.program_id(0)

    @pl.when(i < nv_ref[0])
    def _():
        x = x_ref[...].astype(BF16)
        g = jnp.dot(x, wg_ref[0, 0].astype(BF16), preferred_element_type=F32)
        u = jnp.dot(x, wu_ref[0, 0].astype(BF16), preferred_element_type=F32)
        hh = (_silu(g) * u).astype(BF16)
        y_ref[...] = jnp.dot(hh, wd_ref[0, 0].astype(BF16), preferred_element_type=F32)

    @pl.when(i >= nv_ref[0])
    def _():
        y_ref[...] = jnp.zeros(y_ref.shape, F32)


def moe_experts(xs, tile_expert, n_valid, w_gate, w_up, w_down, layer, tm):
    p, d = xs.shape
    f = w_gate.shape[-1]
    n_tiles = p // tm
    grid_spec = pltpu.PrefetchScalarGridSpec(
        num_scalar_prefetch=2,
        grid=(n_tiles,),
        in_specs=[
            pl.BlockSpec((tm, d), lambda i, te, nv: (i, 0)),
            pl.BlockSpec((1, 1, d, f), lambda i, te, nv: (layer, te[i], 0, 0)),
            pl.BlockSpec((1, 1, d, f), lambda i, te, nv: (layer, te[i], 0, 0)),
            pl.BlockSpec((1, 1, f, d), lambda i, te, nv: (layer, te[i], 0, 0)),
        ],
        out_specs=pl.BlockSpec((tm, d), lambda i, te, nv: (i, 0)),
    )
    return pl.pallas_call(
        _experts_kernel, grid_spec=grid_spec,
        out_shape=jax.ShapeDtypeStruct((p, d), F32),
        compiler_params=_params("arbitrary"), name="moe_experts",
    )(tile_expert, n_valid, xs, w_gate, w_up, w_down)


def _take_rows(x, idx):
    return x.at[idx].get(mode="promise_in_bounds", unique_indices=True)


def hier_moe(xn, route, counts, w_gate, w_up, w_down, layer, tm=512):
    n, d = xn.shape
    e_idx = route[:, :TOP_K].astype(jnp.int32)
    rank = route[:, 2 * TOP_K:3 * TOP_K].astype(jnp.int32)
    cnt = counts[0, N_GROUPS:N_GROUPS + N_EXPERTS].astype(jnp.int32)
    padded = ((cnt + tm - 1) // tm) * tm
    pad_ends = jnp.cumsum(padded)
    pad_starts = pad_ends - padded
    experts = jnp.arange(N_EXPERTS, dtype=jnp.int32)
    pos = jnp.sum(jnp.where(e_idx[..., None] == experts, pad_starts, 0), axis=-1) + rank
    p = n * TOP_K + N_EXPERTS * tm
    n_tiles = p // tm
    tok = jnp.broadcast_to(jnp.arange(n, dtype=jnp.int32)[:, None], (n, TOP_K))
    row_tok = (jnp.arange(p, dtype=jnp.int32) % n).at[pos.reshape(-1)].set(
        tok.reshape(-1), unique_indices=True, mode="promise_in_bounds")
    n_valid = (pad_ends[-1] // tm).astype(jnp.int32)
    tile_start = jnp.minimum(jnp.arange(n_tiles, dtype=jnp.int32), n_valid - 1) * tm
    tile_e = jnp.sum((tile_start[:, None] >= pad_ends[None, :]).astype(jnp.int32), axis=-1)
    tile_e = jnp.minimum(tile_e, N_EXPERTS - 1)
    xs = xn.at[row_tok].get(mode="promise_in_bounds")
    ys = moe_experts(xs, tile_e, n_valid.reshape(1), w_gate, w_up, w_down, layer, tm)
    return tuple(_take_rows(ys, pos[:, kk]) for kk in range(TOP_K))


def _final_norm_kernel(h_ref, rt_ref, a_ref, b_ref, nw_ref, o_ref):
    h = _add_expert_outputs(h_ref[...], rt_ref, (a_ref, b_ref))
    o_ref[...] = h * lax.rsqrt(jnp.mean(h * h, axis=-1, keepdims=True) + EPS) * nw_ref[...]


def final_norm(h, route, a, b, norm_w, add_row0=0, tm=256):
    n, d = h.shape
    tm = min(tm, n)
    assert add_row0 % tm == 0
    row = pl.BlockSpec((tm, d), lambda i: (i, 0))
    add_row = pl.BlockSpec((tm, d), lambda i: (i + add_row0 // tm, 0))
    return pl.pallas_call(
        _final_norm_kernel, grid=(n // tm,),
        in_specs=[row, pl.BlockSpec((tm, LANES), lambda i: (i + add_row0 // tm, 0)), add_row, add_row,
                  pl.BlockSpec((1, d), lambda i: (0, 0))],
        out_specs=row, out_shape=jax.ShapeDtypeStruct((n, d), F32),
        compiler_params=_params("parallel"), name="final_norm",
    )(h, route, a, b, norm_w.reshape(1, d))


def kernel(x_prompt, x_sample, state_a_conv, state_a_ssm, cache_b_k, cache_b_v, cache_c_k, cache_c_v, page_table, norm_mix, norm_ffn, norm_final, rel_bias, w_in0, conv_w, a_log, dt_bias, gdn_norm_w, diff_lambda, diff_subln_w, w_out0, w_in1, b_in1, sinks, w_out1, w_group, b_group, w_router, b_router, w_gate, w_up, w_down):
    bp, tp, d = x_prompt.shape
    sn, sl, _ = x_sample.shape
    n_p, n_s = bp * tp, sn * sl
    parts = ((x_prompt.reshape(n_p, d), bp, tp), (x_sample.reshape(n_s, d), sn, sl))

    off_z = CONV_CH
    off_a = off_z + H_A * DV_A
    off_qb = off_a + 2 * H_A
    w0 = jnp.concatenate([w_in0[:, :off_a], w_in0[:, off_a:off_qb],
                          jnp.zeros((d, LANES - 2 * H_A), w_in0.dtype), w_in0[:, off_qb:]], axis=1).astype(BF16)
    widths0 = (CONV_CH, H_A * DV_A, LANES, B_W, B_W, B_W)
    w_out0_bf = w_out0.astype(BF16)
    w_out0_parts = [w_out0_bf[:H_A * DV_A], w_out0_bf[H_A * DV_A:]]

    row0 = (0, n_p)

    def moe_layer(layer, xns, routes, counts):
        xn_all = jnp.concatenate(xns, axis=0)
        route_all = jnp.concatenate(routes, axis=0)
        return (route_all,) + hier_moe(xn_all, route_all, counts, w_gate, w_up, w_down, layer)

    conv_states = (jnp.zeros((bp, CONV_W - 1, CONV_CH), F32), state_a_conv)
    ssm_states = (jnp.zeros((bp, H_A, DK_A, DV_A), F32), state_a_ssm)
    hs, xns, routes, conv_new, ssm_new, kb_new, vb_new = [], [], [], [], [], [], []
    counts = jnp.zeros((1, LANES), F32)
    for idx, (x2, b, t) in enumerate(parts):
        qkv, z, ab, qb, kb, vb, kb4, vb4 = norm_proj(x2, [], norm_mix[0], w0, None, widths0, emit_h=False,
                                                     head_split=(4, 5))
        r3 = lambda a: a.reshape(b, t, a.shape[-1])
        o_a, s_new = gdn_heads(r3(qkv), r3(z), r3(ab), conv_states[idx], ssm_states[idx], conv_w, a_log, dt_bias,
                               gdn_norm_w, min(GDN_CHUNK, t), 2 if idx == 0 else 8)
        if idx == 0:
            o_b = diff_attn_prompt(r3(qb), r3(kb), r3(vb), rel_bias, diff_lambda, diff_subln_w, 512)
        else:
            o_b = diff_attn_sample(r3(qb), r3(kb), r3(vb), cache_b_k, cache_b_v, page_table, rel_bias, diff_lambda,
                                   diff_subln_w, 16)
        h1, xn, route, counts = out_route(x2, [o_a.reshape(b * t, -1), o_b.reshape(b * t, -1)], w_out0_parts,
                                          norm_ffn[0], w_group[0], b_group[0], w_router[0], b_router[0], counts)
        hs.append(h1), xns.append(xn), routes.append(route)
        conv_new.append(r3(qkv)[:, t - (CONV_W - 1):, :])
        ssm_new.append(s_new)
        kb_new.append(kb4.reshape(b, t, H_B, 2 * DH_B))
        vb_new.append(vb4.reshape(b, t, H_B, 2 * DH_B))
    moe_out = moe_layer(0, xns, routes, counts)

    w1 = w_in1.astype(BF16)
    w_out1_bf = w_out1.astype(BF16)
    hs2, xns, routes, kc_new, vc_new = [], [], [], [], []
    counts = jnp.zeros((1, LANES), F32)
    for idx, (_, b, t) in enumerate(parts):
        h2, q, k, v = norm_proj(hs[idx], moe_out, norm_mix[1], w1, b_in1, (Q_C, KV_C, KV_C),
                                emit_h=True, add_row0=row0[idx])
        r3 = lambda a: a.reshape(b, t, a.shape[-1])
        if idx == 0:
            o_c = swa_prompt(r3(q), r3(k), r3(v), rel_bias, sinks)
            kc_new.append(r3(k)[:, t - WINDOW:].reshape(b, WINDOW, H_KV_C, DH_C))
            vc_new.append(r3(v)[:, t - WINDOW:].reshape(b, WINDOW, H_KV_C, DH_C))
        else:
            o_c = swa_sample(r3(q), r3(k), r3(v), cache_c_k, cache_c_v, rel_bias, sinks, 8)
            kc_new.append(jnp.concatenate([cache_c_k[:, t:], k.reshape(b, t, H_KV_C, DH_C)], axis=1))
            vc_new.append(jnp.concatenate([cache_c_v[:, t:], v.reshape(b, t, H_KV_C, DH_C)], axis=1))
        h3, xn, route, counts = out_route(h2, [o_c.reshape(b * t, -1)], [w_out1_bf], norm_ffn[1],
                                          w_group[1], b_group[1], w_router[1], b_router[1], counts)
        hs2.append(h3), xns.append(xn), routes.append(route)
    moe_out = moe_layer(1, xns, routes, counts)

    y_out = [final_norm(hs2[idx], *moe_out, norm_final, add_row0=row0[idx]).reshape(b, t, d)
             for idx, (_, b, t) in enumerate(parts)]
    return (y_out[0], y_out[1], conv_new[0], conv_new[1], ssm_new[0], ssm_new[1],
            kb_new[0], vb_new[0], kb_new[1], vb_new[1], kc_new[0], vc_new[0], kc_new[1], vc_new[1])
```

```python
import functools
import math

import jax
import jax.numpy as jnp
import numpy as np
from jax import lax
from jax.experimental import pallas as pl
from jax.experimental.pallas import tpu as pltpu

F32 = jnp.float32
BF16 = jnp.bfloat16
EPS = 1e-6
NEG = -1e30
LOG2E = math.log2(math.e)
LANES = 128
VMEM_LIMIT = 56 * 1024 * 1024

D_MODEL = 1024
H_A, DK_A, DV_A, CONV_W = 4, 128, 128, 4
CONV_CH = 2 * H_A * DK_A + H_A * DV_A
GDN_CHUNK = 64
GDN_PRECISE_INV = False
H_B, DH_B = 4, 64
B_W = H_B * 2 * DH_B
PAGE = 128
H_C, H_KV_C, DH_C, WINDOW = 16, 2, 64, 128
G_C = H_C // H_KV_C
Q_C = H_C * DH_C
KV_C = H_KV_C * DH_C
N_BUCKETS, MAX_DIST = 32, 128
N_GROUPS, EXP_PER_GROUP, TOP_K, D_EXPERT = 4, 8, 2, 256
N_EXPERTS = N_GROUPS * EXP_PER_GROUP
LAM_INIT0 = 0.8 - 0.6 * math.exp(-0.3 * 0)


def _params(*sem):
    return pltpu.CompilerParams(dimension_semantics=sem, vmem_limit_bytes=VMEM_LIMIT)


def _mm(a, b, precise=False):
    if precise:
        return jnp.dot(a.astype(F32), b.astype(F32), preferred_element_type=F32, precision=lax.Precision.HIGHEST)
    return jnp.dot(a.astype(BF16), b.astype(BF16), preferred_element_type=F32)


def _mm_nt(a, b, precise=False):
    dn = (((1,), (1,)), ((), ()))
    if precise:
        return lax.dot_general(a.astype(F32), b.astype(F32), dn, preferred_element_type=F32,
                               precision=lax.Precision.HIGHEST)
    return lax.dot_general(a.astype(BF16), b.astype(BF16), dn, preferred_element_type=F32)


def _mm_tn(a, b, precise=False):
    dn = (((0,), (0,)), ((), ()))
    if precise:
        return lax.dot_general(a.astype(F32), b.astype(F32), dn, preferred_element_type=F32,
                               precision=lax.Precision.HIGHEST)
    return lax.dot_general(a.astype(BF16), b.astype(BF16), dn, preferred_element_type=F32)


def _sigmoid(x):
    return 1.0 / (1.0 + jnp.exp(-x))


def _silu(x):
    return x * _sigmoid(x)


def _softplus(x):
    return jnp.maximum(x, 0.0) + jnp.log(1.0 + jnp.exp(-jnp.abs(x)))


def _add_expert_outputs(h, route_ref, y_refs):
    rt = route_ref[...]
    for kk, y_ref in enumerate(y_refs):
        h = h + rt[:, TOP_K + kk:TOP_K + kk + 1] * y_ref[...]
    return h


def _norm_proj_kernel(n_add, widths, has_bias, emit_h, head_split, *refs):
    x_ref = refs[0]
    add_refs = refs[1:1 + n_add]
    nw_ref, w_ref = refs[1 + n_add], refs[2 + n_add]
    pos = 3 + n_add
    b_ref = None
    if has_bias:
        b_ref = refs[pos]
        pos += 1
    outs = refs[pos:]
    h = x_ref[...]
    if n_add:
        h = _add_expert_outputs(h, add_refs[0], add_refs[1:])
    if emit_h:
        outs[0][...] = h
        outs = outs[1:]
    split_refs = outs[len(widths):]
    xn = h * lax.rsqrt(jnp.mean(h * h, axis=-1, keepdims=True) + EPS) * nw_ref[...]
    xb = xn.astype(BF16)
    off = 0
    for j, (o_ref, wd) in enumerate(zip(outs, widths)):
        y = jnp.dot(xb, w_ref[:, off:off + wd], preferred_element_type=F32)
        if has_bias:
            y = y + b_ref[:, off:off + wd]
        o_ref[...] = y
        if j in head_split:
            s_ref = split_refs[head_split.index(j)]
            for hh in range(wd // LANES):
                s_ref[:, hh, :] = y[:, hh * LANES:(hh + 1) * LANES]
        off += wd


def norm_proj(x, addends, norm_w, w_bf16, bias, widths, emit_h, head_split=(), add_row0=0, tm=256):
    n, d = x.shape
    m = w_bf16.shape[1]
    tm = min(tm, n)
    assert sum(widths) == m and n % tm == 0
    row = lambda i: (i, 0)
    fixed = lambda i: (0, 0)
    assert add_row0 % tm == 0
    add_row = lambda i: (i + add_row0 // tm, 0)
    in_specs = [pl.BlockSpec((tm, d), row)] + [pl.BlockSpec((tm, a.shape[1]), add_row) for a in addends]
    in_specs += [pl.BlockSpec((1, d), fixed), pl.BlockSpec((d, m), fixed)]
    args = [x, *addends, norm_w.reshape(1, d), w_bf16]
    if bias is not None:
        in_specs.append(pl.BlockSpec((1, m), fixed))
        args.append(bias.reshape(1, m))
    out_shape, out_specs = [], []
    if emit_h:
        out_shape.append(jax.ShapeDtypeStruct((n, d), F32))
        out_specs.append(pl.BlockSpec((tm, d), row))
    for wd in widths:
        out_shape.append(jax.ShapeDtypeStruct((n, wd), F32))
        out_specs.append(pl.BlockSpec((tm, wd), row))
    for j in head_split:
        out_shape.append(jax.ShapeDtypeStruct((n, widths[j] // LANES, LANES), F32))
        out_specs.append(pl.BlockSpec((tm, widths[j] // LANES, LANES), lambda i: (i, 0, 0)))
    return pl.pallas_call(
        functools.partial(_norm_proj_kernel, len(addends), tuple(widths), bias is not None, emit_h,
                          tuple(head_split)),
        grid=(n // tm,), in_specs=in_specs, out_specs=out_specs, out_shape=out_shape,
        compiler_params=_params("parallel"), name="norm_proj",
    )(*args)


def _gdn_kernel(c_len, nb, qkv_ref, z_ref, ab_ref, cbuf_ref, s0_ref, convw_ref, gpar_ref, nw_ref,
                o_ref, sfin_ref, xp_scr, s_scr):
    c = pl.program_id(1)
    n_c = pl.num_programs(1)
    hist = CONV_W - 1
    base = 8 - hist

    @pl.when(c == 0)
    def _():
        xp_scr[:, base:8, :] = cbuf_ref[...]
        s_scr[...] = s0_ref[...]

    ri = lax.broadcasted_iota(jnp.int32, (c_len, c_len), 0)
    ci = lax.broadcasted_iota(jnp.int32, (c_len, c_len), 1)
    lower = ri >= ci
    strict = ri > ci
    eye = (ri == ci).astype(F32)
    lower_f = lower.astype(F32)

    units = []
    for bi in range(nb):
        xp_scr[bi, 8:8 + c_len, :] = qkv_ref[bi]
        y = xp_scr[bi, base:base + c_len, :] * convw_ref[0:1, :]
        for j in range(1, CONV_W):
            y = y + xp_scr[bi, base + j:base + j + c_len, :] * convw_ref[j:j + 1, :]
        xp_scr[bi, base:8, :] = xp_scr[bi, base + c_len:8 + c_len, :]
        y = _silu(y)

        ab = ab_ref[bi]
        g_t = gpar_ref[0:1, :] * _softplus(ab + gpar_ref[1:2, :])
        beta_t = _sigmoid(ab)
        gcum_t = _mm(lower_f, g_t, precise=True)
        gcum_tt = gcum_t.T

        for h in range(H_A):
            q = y[:, h * DK_A:(h + 1) * DK_A]
            k = y[:, (H_A + h) * DK_A:(H_A + h + 1) * DK_A]
            v = y[:, 2 * H_A * DK_A + h * DV_A:2 * H_A * DK_A + (h + 1) * DV_A]
            q = q * lax.rsqrt(jnp.sum(q * q, axis=-1, keepdims=True) + EPS) * (DK_A ** -0.5)
            k = k * lax.rsqrt(jnp.sum(k * k, axis=-1, keepdims=True) + EPS)
            gc = gcum_t[:, h:h + 1]
            gr = gcum_tt[h:h + 1, :]
            beta = beta_t[:, H_A + h:H_A + h + 1]
            decay = jnp.where(lower, jnp.exp(jnp.where(lower, gc - gr, 0.0)), 0.0)
            kb = k * beta
            units.append(dict(bi=bi, h=h, q=q, k=k, k_bf=k.astype(BF16), gc=gc, decay=decay, kb=kb,
                              rhs=jnp.concatenate([v * beta, kb * jnp.exp(gc)], axis=1)))

    for un in units:
        un["nmat"] = jnp.where(strict, _mm_nt(un["kb"], un["k_bf"]) * un["decay"], 0.0)
        un["qk"] = jnp.where(lower, _mm_nt(un["q"], un["k_bf"]) * un["decay"], 0.0)
    for un in units:
        un["inv"] = eye - un["nmat"]
        un["pw"] = _mm(un["nmat"], un["nmat"], GDN_PRECISE_INV)
    span = 2
    while span < c_len:
        for un in units:
            un["inv"] = un["inv"] + _mm(un["inv"], un["pw"], GDN_PRECISE_INV)
        span *= 2
        if span < c_len:
            for un in units:
                un["pw"] = _mm(un["pw"], un["pw"], GDN_PRECISE_INV)
    for un in units:
        sol = _mm(un["inv"], un["rhs"], GDN_PRECISE_INV)
        un["u"], un["w"] = sol[:, :DV_A], sol[:, DV_A:]
        un["s"] = s_scr[un["bi"], un["h"]]
        un["s_bf"] = un["s"].astype(BF16)
    for un in units:
        un["v_new"] = un["u"] - _mm(un["w"], un["s_bf"])
        un["o_s"] = _mm(un["q"] * jnp.exp(un["gc"]), un["s_bf"])
    for un in units:
        bi, h, gc = un["bi"], un["h"], un["gc"]
        o = un["o_s"] + _mm(un["qk"], un["v_new"])
        g_last = gc[c_len - 1:c_len, :]
        s_scr[bi, h] = un["s"] * jnp.exp(g_last) + _mm_tn(un["k"] * jnp.exp(g_last - gc), un["v_new"])
        o = o * lax.rsqrt(jnp.mean(o * o, axis=-1, keepdims=True) + EPS) * nw_ref[...]
        o_ref[bi, :, h * DV_A:(h + 1) * DV_A] = o * _silu(z_ref[bi, :, h * DV_A:(h + 1) * DV_A])

    @pl.when(c == n_c - 1)
    def _():
        sfin_ref[...] = s_scr[...]


def gdn_heads(qkv, z, ab, conv_buf, s0, conv_w, a_log, dt_bias, norm_w, c_len, nb):
    b, t, _ = qkv.shape
    assert t % c_len == 0 and b % nb == 0
    convw_t = conv_w.T
    gpar = jnp.zeros((2, LANES), F32)
    gpar = gpar.at[0, :H_A].set(-jnp.exp(a_log.astype(F32))).at[1, :H_A].set(dt_bias.astype(F32))
    tok = lambda i, j: (i, j, 0)
    seq3 = lambda i, j: (i, 0, 0)
    fixed = lambda i, j: (0, 0)
    return pl.pallas_call(
        functools.partial(_gdn_kernel, c_len, nb),
        grid=(b // nb, t // c_len),
        in_specs=[
            pl.BlockSpec((nb, c_len, CONV_CH), tok),
            pl.BlockSpec((nb, c_len, H_A * DV_A), tok),
            pl.BlockSpec((nb, c_len, LANES), tok),
            pl.BlockSpec((nb, CONV_W - 1, CONV_CH), seq3),
            pl.BlockSpec((nb, H_A, DK_A, DV_A), lambda i, j: (i, 0, 0, 0)),
            pl.BlockSpec((CONV_W, CONV_CH), fixed),
            pl.BlockSpec((2, LANES), fixed),
            pl.BlockSpec((1, DV_A), fixed),
        ],
        out_specs=[
            pl.BlockSpec((nb, c_len, H_A * DV_A), tok),
            pl.BlockSpec((nb, H_A, DK_A, DV_A), lambda i, j: (i, 0, 0, 0)),
        ],
        out_shape=[
            jax.ShapeDtypeStruct((b, t, H_A * DV_A), F32),
            jax.ShapeDtypeStruct((b, H_A, DK_A, DV_A), F32),
        ],
        scratch_shapes=[
            pltpu.VMEM((nb, 8 + c_len, CONV_CH), F32),
            pltpu.VMEM((nb, H_A, DK_A, DV_A), F32),
        ],
        compiler_params=_params("parallel", "arbitrary"), name="gdn_heads",
    )(qkv, z, ab, conv_buf, s0, convw_t, gpar, norm_w.reshape(1, DV_A))


def _t5_bucket_np(dist):
    d = np.maximum(dist, 0)
    max_exact = N_BUCKETS // 2
    ratio = (np.log(np.maximum(d, 1).astype(np.float32) / np.float32(max_exact))
             / np.float32(math.log(MAX_DIST / max_exact)))
    large = np.minimum(max_exact + (ratio * (N_BUCKETS - max_exact)).astype(np.int32), N_BUCKETS - 1)
    return np.where(d < max_exact, d, large).astype(np.int32)


FAR_BUCKET = int(_t5_bucket_np(np.array([MAX_DIST]))[0])
assert np.all(_t5_bucket_np(np.arange(MAX_DIST, 4 * MAX_DIST)) == FAR_BUCKET)


def _bias_kernel(col_group, bucket_ref, rb_ref, out_ref):
    for col, grp in enumerate(col_group):
        bk = bucket_ref[grp]
        acc = jnp.full(bk.shape, NEG, F32)
        for b in range(N_BUCKETS):
            acc = jnp.where(bk == b, rb_ref[b, col], acc)
        out_ref[col] = acc


def bias_tiles(rel_bias, dist, valid, n_cols, col_group=None):
    bucket = np.where(valid, _t5_bucket_np(dist), -1).astype(np.int32)
    if bucket.ndim == 2:
        bucket, col_group = bucket[None], (0,) * n_cols
    _, r, c = bucket.shape
    return pl.pallas_call(
        functools.partial(_bias_kernel, tuple(col_group)),
        in_specs=[pl.BlockSpec(memory_space=pltpu.VMEM), pl.BlockSpec(memory_space=pltpu.SMEM)],
        out_specs=pl.BlockSpec(memory_space=pltpu.VMEM),
        out_shape=jax.ShapeDtypeStruct((n_cols, r, c), F32),
        name="bias_tiles",
    )(jnp.asarray(bucket), rel_bias.astype(F32))


def _diff_lambda(lam_ref):
    lp = lam_ref[...]
    s1 = jnp.sum(lp[0:1] * lp[1:2], axis=-1, keepdims=True)
    s2 = jnp.sum(lp[2:3] * lp[3:4], axis=-1, keepdims=True)
    return jnp.exp(s1) - jnp.exp(s2) + LAM_INIT0


def _diff_finish(o1, o2, lam, nw):
    o = o1 - lam * o2
    return o * lax.rsqrt(jnp.mean(o * o, axis=-1, keepdims=True) + EPS) * nw * (1.0 - LAM_INIT0)


def _softmax_update(s, v_bf, m_ref, l_ref, acc_ref):
    m_prev = m_ref[...]
    m_new = jnp.maximum(m_prev, jnp.max(s, axis=-1, keepdims=True))
    alpha = jnp.exp(m_prev - m_new)
    p = jnp.exp(s - m_new)
    l_ref[...] = alpha * l_ref[...] + jnp.sum(p, axis=-1, keepdims=True)
    acc_ref[...] = alpha * acc_ref[...] + jnp.dot(p.astype(BF16), v_bf, preferred_element_type=F32)
    m_ref[...] = m_new


def _diff_prompt_kernel(blk, qt_ref, kt_ref, q_ref, k_ref, v_ref, t0_ref, t1_ref, far_ref, lam_ref, nw_ref, o_ref,
                        m_scr, acc_scr):
    qi = qt_ref[pl.program_id(1)]
    ki = kt_ref[pl.program_id(1)]
    nsub = blk // LANES
    hw = 2 * DH_B

    @pl.when(ki == 0)
    def _():
        m_scr[...] = jnp.full(m_scr.shape, NEG, F32)
        acc_scr[...] = jnp.zeros(acc_scr.shape, F32)

    def bias_block(case, hm):
        const = jnp.full((LANES, LANES), far_ref[hm], F32)
        neg = jnp.full((LANES, LANES), NEG, F32)
        rows = []
        for jj in range(nsub):
            tiles = []
            for ii in range(nsub):
                sub = ii - jj + (nsub if case == "prev" else 0)
                tiles.append(neg if sub < 0 else t0_ref[hm] if sub == 0 else t1_ref[hm] if sub == 1 else const)
            rows.append(jnp.concatenate(tiles, axis=1))
        return jnp.concatenate(rows, axis=0)

    def step(case):
        q = (q_ref[0] * (DH_B ** -0.5 * LOG2E)).astype(BF16)
        k = k_ref[0].astype(BF16)
        v_t = v_ref[0].T
        first = lax.broadcasted_iota(jnp.int32, (1, hw), 1) < DH_B
        ones = jnp.ones((hw, blk), F32)
        s_all, v_ext = [], []
        for h in range(H_B):
            sl = slice(h * hw, (h + 1) * hw)
            qh, kh = q[:, sl], k[:, sl]
            v_ext.append(jnp.concatenate([v_t[sl], ones], axis=0).astype(BF16))
            for mp in range(2):
                km = jnp.where(first if mp == 0 else jnp.logical_not(first), kh, jnp.zeros_like(kh))
                s_all.append(lax.dot_general(km, qh, (((1,), (1,)), ((), ())),
                                             preferred_element_type=F32))
        p_all, alpha_all = [], []
        for hm, s in enumerate(s_all):
            m_prev = m_scr[hm]
            if case == "far":
                shift = far_ref[hm]
                m_new = jnp.maximum(m_prev, jnp.max(s, axis=0, keepdims=True) + shift)
                p = jnp.exp2(s + (shift - m_new))
            else:
                s = s + bias_block(case, hm)
                m_new = jnp.maximum(m_prev, jnp.max(s, axis=0, keepdims=True))
                p = jnp.exp2(s - m_new)
            alpha_all.append(jnp.exp2(m_prev - m_new))
            p_all.append(p.astype(BF16))
            m_scr[hm] = m_new
        for hm, (p, alpha) in enumerate(zip(p_all, alpha_all)):
            acc_scr[hm] = alpha * acc_scr[hm] + jnp.dot(v_ext[hm // 2], p, preferred_element_type=F32)

    @pl.when(ki < qi - 1)
    def _():
        step("far")

    @pl.when(ki == qi - 1)
    def _():
        step("prev")

    @pl.when(ki == qi)
    def _():
        step("diag")
        lam = _diff_lambda(lam_ref)
        for h in range(H_B):
            a1, a2 = acc_scr[2 * h], acc_scr[2 * h + 1]
            o = a1[:hw] / a1[hw:hw + 1] - lam * (a2[:hw] / a2[hw:hw + 1])
            o = o * lax.rsqrt(jnp.mean(o * o, axis=0, keepdims=True) + EPS) * nw_ref[...] * (1.0 - LAM_INIT0)
            o_ref[0, :, h * hw:(h + 1) * hw] = o.T


def diff_attn_prompt(q, k, v, rel_bias, diff_lambda, subln_w, blk):
    b, t, _ = q.shape
    assert t % blk == 0 and blk % LANES == 0 and blk >= 2 * LANES
    i = np.arange(LANES)
    d0 = i[None, :] - i[:, None]
    t0 = bias_tiles(rel_bias, d0, d0 >= 0, 2 * H_B) * LOG2E
    t1 = bias_tiles(rel_bias, d0 + LANES, np.ones_like(d0, bool), 2 * H_B) * LOG2E
    far = rel_bias[FAR_BUCKET, :2 * H_B].astype(F32) * LOG2E
    nb = t // blk
    hw = 2 * DH_B
    pairs = [(qi, ki) for qi in range(nb) for ki in range(qi + 1)]
    q_tab = jnp.asarray(np.array([p[0] for p in pairs], np.int32))
    k_tab = jnp.asarray(np.array([p[1] for p in pairs], np.int32))
    whole = lambda shape: pl.BlockSpec(shape, lambda bi, pi, qt, kt: (0,) * len(shape))
    grid_spec = pltpu.PrefetchScalarGridSpec(
        num_scalar_prefetch=2,
        grid=(b, len(pairs)),
        in_specs=[
            pl.BlockSpec((1, blk, B_W), lambda bi, pi, qt, kt: (bi, qt[pi], 0)),
            pl.BlockSpec((1, blk, B_W), lambda bi, pi, qt, kt: (bi, kt[pi], 0)),
            pl.BlockSpec((1, blk, B_W), lambda bi, pi, qt, kt: (bi, kt[pi], 0)),
            whole((2 * H_B, LANES, LANES)),
            whole((2 * H_B, LANES, LANES)),
            pl.BlockSpec(memory_space=pltpu.SMEM),
            whole((4, DH_B)),
            whole((hw, 1)),
        ],
        out_specs=pl.BlockSpec((1, blk, B_W), lambda bi, pi, qt, kt: (bi, qt[pi], 0)),
        scratch_shapes=[
            pltpu.VMEM((2 * H_B, 1, blk), F32),
            pltpu.VMEM((2 * H_B, 2 * hw, blk), F32),
        ],
    )
    return pl.pallas_call(
        functools.partial(_diff_prompt_kernel, blk),
        grid_spec=grid_spec,
        out_shape=jax.ShapeDtypeStruct((b, t, B_W), F32),
        compiler_params=_params("parallel", "arbitrary"), name="diff_attn_prompt",
    )(q_tab, k_tab, q, k, v, t0, t1, far, diff_lambda.astype(F32), subln_w.reshape(hw, 1).astype(F32))


def _diff_sample_kernel(n_pg, s_len, pt_ref, q_ref, kn_ref, vn_ref, *refs):
    k_pages = refs[:n_pg]
    v_pages = refs[n_pg:2 * n_pg]
    blast_ref, bnew_ref, far_ref, lam_ref, nw_ref, o_ref, m_scr, l_scr, acc_scr = refs[2 * n_pg:]
    j = pl.program_id(1)
    last = pl.num_programs(1) - 1
    hw = 2 * DH_B

    @pl.when(j == 0)
    def _():
        m_scr[...] = jnp.full(m_scr.shape, NEG, F32)
        l_scr[...] = jnp.zeros(l_scr.shape, F32)
        acc_scr[...] = jnp.zeros(acc_scr.shape, F32)

    q = q_ref[0] * (DH_B ** -0.5)
    first = lax.broadcasted_iota(jnp.int32, (1, hw), 1) < DH_B
    rh = 2 * s_len
    q_heads = []
    for h in range(H_B):
        qh = q[:, h * hw:(h + 1) * hw]
        q_heads.append(jnp.concatenate([jnp.where(first, qh, 0.0), jnp.where(first, 0.0, qh)], axis=0).astype(BF16))

    def head_rows(ref, pages, h):
        return jnp.concatenate([r[0, pl.ds(h, PAGE, stride=H_B), :] for r in pages], axis=0).astype(BF16)

    n_grp = 4 if n_pg % 4 == 0 else 1
    per = n_pg // n_grp
    wg = per * PAGE
    s_units = [[None] * n_grp for _ in range(H_B)]
    for g in range(n_grp):
        for h in range(H_B):
            rows = slice(h * rh, (h + 1) * rh)
            bias = jnp.where(j == last, blast_ref[rows, g * wg:(g + 1) * wg], far_ref[rows, :])
            kgh = head_rows(None, k_pages[g * per:(g + 1) * per], h)
            s_units[h][g] = lax.dot_general(q_heads[h], kgh, (((1,), (1,)), ((), ())),
                                            preferred_element_type=F32) + bias
    m_news, alphas = [], []
    for h in range(H_B):
        m_prev = m_scr[h * rh:(h + 1) * rh]
        m_new = m_prev
        for s in s_units[h]:
            m_new = jnp.maximum(m_new, jnp.max(s, axis=-1, keepdims=True))
        m_news.append(m_new)
        alphas.append(jnp.exp(m_prev - m_new))
    l_new = [alphas[h] * l_scr[h * rh:(h + 1) * rh] for h in range(H_B)]
    acc_new = [alphas[h] * acc_scr[h * rh:(h + 1) * rh] for h in range(H_B)]
    for g in range(n_grp):
        for h in range(H_B):
            p = jnp.exp(s_units[h][g] - m_news[h])
            vgh = head_rows(None, v_pages[g * per:(g + 1) * per], h)
            l_new[h] = l_new[h] + jnp.sum(p, axis=-1, keepdims=True)
            acc_new[h] = acc_new[h] + jnp.dot(p.astype(BF16), vgh, preferred_element_type=F32)
    for h in range(H_B):
        m_scr[h * rh:(h + 1) * rh] = m_news[h]
        l_scr[h * rh:(h + 1) * rh] = l_new[h]
        acc_scr[h * rh:(h + 1) * rh] = acc_new[h]

    @pl.when(j == last)
    def _():
        pad = jnp.zeros((PAGE - s_len, hw), F32)
        for h in range(H_B):
            rows = slice(h * rh, (h + 1) * rh)
            kn = jnp.concatenate([kn_ref[0, :, h * hw:(h + 1) * hw], pad], axis=0).astype(BF16)
            vn = jnp.concatenate([vn_ref[0, :, h * hw:(h + 1) * hw], pad], axis=0).astype(BF16)
            s2 = lax.dot_general(q_heads[h], kn, (((1,), (1,)), ((), ())), preferred_element_type=F32)
            _softmax_update(s2 + bnew_ref[rows], vn, m_scr.at[rows], l_scr.at[rows], acc_scr.at[rows])
        accn = acc_scr[...] / l_scr[...]
        lam = _diff_lambda(lam_ref)
        for h in range(H_B):
            o1 = accn[(2 * h) * s_len:(2 * h + 1) * s_len]
            o2 = accn[(2 * h + 1) * s_len:(2 * h + 2) * s_len]
            o_ref[0, :, h * hw:(h + 1) * hw] = _diff_finish(o1, o2, lam, nw_ref[...])


def diff_attn_sample(q, k, v, cache_k, cache_v, page_table, rel_bias, diff_lambda, subln_w, n_pg):
    s_n, s_len, _ = q.shape
    pages_per_seq = page_table.shape[1]
    assert pages_per_seq % n_pg == 0
    past = pages_per_seq * PAGE
    n_phys = cache_k.shape[0]
    hw = 2 * DH_B
    ck = cache_k.reshape(n_phys, PAGE * H_B, hw)
    cv = cache_v.reshape(n_phys, PAGE * H_B, hw)
    n_rows = 2 * H_B * s_len
    blk_keys = n_pg * PAGE
    assert PAGE >= s_len
    t = np.arange(s_len)
    d_last = (past + t)[:, None] - (past - blk_keys + np.arange(blk_keys))[None, :]
    blast = bias_tiles(rel_bias, d_last, np.ones_like(d_last, bool), 2 * H_B).reshape(n_rows, blk_keys)
    far = bias_tiles(rel_bias, np.full((s_len, LANES), MAX_DIST), np.ones((s_len, LANES), bool), 2 * H_B)
    far = far.reshape(n_rows, LANES)[:, :1]
    cnew = np.arange(PAGE)
    d_new = t[:, None] - cnew[None, :]
    bnew = bias_tiles(rel_bias, d_new, (d_new >= 0) & (cnew[None, :] < s_len), 2 * H_B).reshape(n_rows, PAGE)
    assert past - blk_keys + s_len - 1 >= 0 and np.all(_t5_bucket_np(np.array([blk_keys + 1])) == FAR_BUCKET)

    def page_spec(i):
        return pl.BlockSpec((1, PAGE * H_B, hw), lambda s, j, pt: (pt[s * pages_per_seq + j * n_pg + i], 0, 0))

    seq = pl.BlockSpec((1, s_len, B_W), lambda s, j, pt: (s, 0, 0))
    whole = lambda shape: pl.BlockSpec(shape, lambda s, j, pt: (0,) * len(shape))
    grid_spec = pltpu.PrefetchScalarGridSpec(
        num_scalar_prefetch=1,
        grid=(s_n, pages_per_seq // n_pg),
        in_specs=[seq, seq, seq] + [page_spec(i) for i in range(n_pg)] * 2 + [
            whole((n_rows, blk_keys)), whole((n_rows, PAGE)), whole((n_rows, 1)),
            whole((4, DH_B)), whole((1, hw))],
        out_specs=seq,
        scratch_shapes=[
            pltpu.VMEM((n_rows, 1), F32),
            pltpu.VMEM((n_rows, 1), F32),
            pltpu.VMEM((n_rows, hw), F32),
        ],
    )
    return pl.pallas_call(
        functools.partial(_diff_sample_kernel, n_pg, s_len),
        grid_spec=grid_spec,
        out_shape=jax.ShapeDtypeStruct((s_n, s_len, B_W), F32),
        compiler_params=_params("parallel", "arbitrary"), name="diff_attn_sample",
    )(page_table.reshape(-1).astype(jnp.int32), q, k, v, *([ck] * n_pg), *([cv] * n_pg),
      blast, bnew, far, diff_lambda.astype(F32), subln_w.reshape(1, 2 * DH_B).astype(F32))


def _kv_variants(x, kv):
    lo = lax.broadcasted_iota(jnp.int32, (1, LANES), 1) < DH_C
    rolled = pltpu.roll(x, DH_C, 1)
    a_src, b_src = (x, rolled) if kv == 0 else (rolled, x)
    zero = jnp.zeros_like(x)
    return jnp.where(lo, a_src, zero).astype(BF16), jnp.where(lo, zero, b_src).astype(BF16)


def _sink_attend(units):
    scores = [[lax.dot_general(q_bf, kx, (((1,), (1,)), ((), ())), preferred_element_type=F32) + bias
               for kx, bias in zip(k_ab, bias_ab)] for q_bf, k_ab, _, bias_ab, _ in units]
    probs = []
    for s_ab, (_, _, _, _, sink_ab) in zip(scores, units):
        p_ab = []
        for s, sink in zip(s_ab, sink_ab):
            m = jnp.maximum(jnp.max(s, axis=-1, keepdims=True), sink)
            e = jnp.exp(s - m)
            p_ab.append((e / (jnp.sum(e, axis=-1, keepdims=True) + jnp.exp(sink - m))).astype(BF16))
        probs.append(p_ab)
    return [jnp.dot(p_ab[0], v_ab[0], preferred_element_type=F32) + jnp.dot(p_ab[1], v_ab[1], preferred_element_type=F32)
            for p_ab, (_, _, v_ab, _, _) in zip(probs, units)]


def _swa_prompt_kernel(q_ref, kp_ref, kc_ref, vp_ref, vc_ref, bias_ref, sink_ref, o_ref):
    n = pl.program_id(1)
    keys = jnp.concatenate([kp_ref[0], kc_ref[0]], axis=0)
    vals = jnp.concatenate([vp_ref[0], vc_ref[0]], axis=0)
    col = lax.broadcasted_iota(jnp.int32, (1, 2 * WINDOW), 1)
    first_blk = jnp.where((col < WINDOW) & (n == 0), NEG, 0.0)
    n_slot = G_C // 2
    units, slices = [], []
    for kv in range(H_KV_C):
        k_ab = _kv_variants(keys, kv)
        v_ab = _kv_variants(vals, kv)
        for ps in range(n_slot):
            h0 = kv * G_C + 2 * ps
            sl = slice(kv * G_C * DH_C + ps * LANES, kv * G_C * DH_C + (ps + 1) * LANES)
            q = (q_ref[0, :, sl] * (DH_C ** -0.5)).astype(BF16)
            bias_ab = (bias_ref[h0] + first_blk, bias_ref[h0 + 1] + first_blk)
            units.append((q, k_ab, v_ab, bias_ab, (sink_ref[h0], sink_ref[h0 + 1])))
            slices.append(sl)
    for sl, o in zip(slices, _sink_attend(units)):
        o_ref[0, :, sl] = o


def swa_prompt(q, k, v, rel_bias, sinks):
    b, t, _ = q.shape
    nb = t // WINDOW
    i = np.arange(WINDOW)
    j = np.arange(2 * WINDOW)
    dist = WINDOW + i[:, None] - j[None, :]
    bias = bias_tiles(rel_bias, dist, (dist >= 0) & (dist <= WINDOW), H_C)
    cur = lambda bi, n: (bi, n, 0)
    prev = lambda bi, n: (bi, jnp.maximum(n - 1, 0), 0)
    return pl.pallas_call(
        _swa_prompt_kernel,
        grid=(b, nb),
        in_specs=[
            pl.BlockSpec((1, WINDOW, Q_C), cur),
            pl.BlockSpec((1, WINDOW, KV_C), prev), pl.BlockSpec((1, WINDOW, KV_C), cur),
            pl.BlockSpec((1, WINDOW, KV_C), prev), pl.BlockSpec((1, WINDOW, KV_C), cur),
            pl.BlockSpec((H_C, WINDOW, 2 * WINDOW), lambda bi, n: (0, 0, 0)),
            pl.BlockSpec(memory_space=pltpu.SMEM),
        ],
        out_specs=pl.BlockSpec((1, WINDOW, Q_C), cur),
        out_shape=jax.ShapeDtypeStruct((b, t, Q_C), F32),
        compiler_params=_params("parallel", "parallel"), name="swa_prompt",
    )(q, k, k, v, v, bias, sinks.astype(F32))


def _swa_sample_kernel(gs, s_len, q_ref, kn_ref, vn_ref, ck_ref, cv_ref, bias_ref, sink_ref, o_ref):
    n_slot = G_C // 2
    pad = jnp.zeros((WINDOW - s_len, LANES), F32)
    units, where = [], []
    for si in range(gs):
        keys = jnp.concatenate([ck_ref[si], kn_ref[si], pad], axis=0)
        vals = jnp.concatenate([cv_ref[si], vn_ref[si], pad], axis=0)
        for kv in range(H_KV_C):
            k_ab = _kv_variants(keys, kv)
            v_ab = _kv_variants(vals, kv)
            base = kv * G_C * DH_C
            q = jnp.concatenate([q_ref[si, :, base + ps * LANES:base + (ps + 1) * LANES] for ps in range(n_slot)],
                                axis=0)
            q = (q * (DH_C ** -0.5)).astype(BF16)
            units.append((q, k_ab, v_ab, (bias_ref[kv, 0], bias_ref[kv, 1]), (sink_ref[kv, 0], sink_ref[kv, 1])))
            where.append((si, base))
    for (si, base), o in zip(where, _sink_attend(units)):
        for ps in range(n_slot):
            o_ref[si, :, base + ps * LANES:base + (ps + 1) * LANES] = o[ps * s_len:(ps + 1) * s_len]


def swa_sample(q, k, v, cache_k, cache_v, rel_bias, sinks, gs):
    s_n, s_len, _ = q.shape
    assert s_n % gs == 0 and cache_k.shape[1] == WINDOW
    ck = cache_k.reshape(s_n, WINDOW, KV_C)
    cv = cache_v.reshape(s_n, WINDOW, KV_C)
    tq = np.arange(s_len)
    j = np.arange(2 * WINDOW)
    dist = WINDOW + tq[:, None] - j[None, :]
    valid = (dist >= 0) & (dist <= WINDOW) & (j[None, :] < WINDOW + s_len)
    n_slot = G_C // 2
    bias = bias_tiles(rel_bias, dist, valid, H_C)
    bias = bias.reshape(H_KV_C, n_slot, 2, s_len, 2 * WINDOW).transpose(0, 2, 1, 3, 4)
    bias = bias.reshape(H_KV_C, 2, n_slot * s_len, 2 * WINDOW)
    sk = sinks.astype(F32).reshape(H_KV_C, n_slot, 2).transpose(0, 2, 1)
    sk = jnp.repeat(sk, s_len, axis=-1).reshape(H_KV_C, 2, n_slot * s_len, 1)
    grp = lambda i: (i, 0, 0)
    whole = lambda shape: pl.BlockSpec(shape, lambda i: (0,) * len(shape))
    return pl.pallas_call(
        functools.partial(_swa_sample_kernel, gs, s_len),
        grid=(s_n // gs,),
        in_specs=[
            pl.BlockSpec((gs, s_len, Q_C), grp),
            pl.BlockSpec((gs, s_len, KV_C), grp), pl.BlockSpec((gs, s_len, KV_C), grp),
            pl.BlockSpec((gs, WINDOW, KV_C), grp), pl.BlockSpec((gs, WINDOW, KV_C), grp),
            whole(bias.shape), whole(sk.shape),
        ],
        out_specs=pl.BlockSpec((gs, s_len, Q_C), grp),
        out_shape=jax.ShapeDtypeStruct((s_n, s_len, Q_C), F32),
        compiler_params=_params("parallel"), name="swa_sample",
    )(q, k, v, ck, cv, bias, sk)


ROUTE_BIG = 1 << 20


def _out_route_kernel(n_in, *refs):
    h_ref = refs[0]
    o_refs = refs[1:1 + n_in]
    w_refs = refs[1 + n_in:1 + 2 * n_in]
    nw_ref, wr_ref, br_ref, cin_ref, hn_ref, xn_ref, rt_ref, cout_ref, cnt_scr = refs[1 + 2 * n_in:]
    step = pl.program_id(0)

    @pl.when(step == 0)
    def _():
        cnt_scr[...] = cin_ref[...]

    h = h_ref[...]
    for o_ref, w_ref in zip(o_refs, w_refs):
        h = h + jnp.dot(o_ref[...].astype(BF16), w_ref[...], preferred_element_type=F32)
    hn_ref[...] = h
    xn = h * lax.rsqrt(jnp.mean(h * h, axis=-1, keepdims=True) + EPS) * nw_ref[...]
    xn_ref[...] = xn
    logit = jnp.dot(xn, wr_ref[...], preferred_element_type=F32, precision=lax.Precision.HIGHEST) + br_ref[...]
    lane = lax.broadcasted_iota(jnp.int32, logit.shape, 1)
    gmask = lane < N_GROUPS
    gl = jnp.where(gmask, logit, NEG)
    gmax = jnp.max(gl, axis=-1, keepdims=True)
    gsel = jnp.min(jnp.where(gl == gmax, lane, ROUTE_BIG), axis=-1, keepdims=True)
    gw = 1.0 / jnp.sum(jnp.where(gmask, jnp.exp(gl - gmax), 0.0), axis=-1, keepdims=True)
    eid = lane - N_GROUPS
    emask = (eid >= 0) & (eid < N_EXPERTS) & (jnp.right_shift(eid, 3) == gsel)
    el = jnp.where(emask, logit, NEG)
    v1 = jnp.max(el, axis=-1, keepdims=True)
    i1 = jnp.min(jnp.where((el == v1) & emask, lane, ROUTE_BIG), axis=-1, keepdims=True)
    emask2 = emask & (lane != i1)
    el2 = jnp.where(emask2, logit, NEG)
    v2 = jnp.max(el2, axis=-1, keepdims=True)
    i2 = jnp.min(jnp.where((el2 == v2) & emask2, lane, ROUTE_BIG), axis=-1, keepdims=True)
    e = jnp.exp(v2 - v1)
    w1 = gw / (1.0 + e)
    w2 = gw * e / (1.0 + e)
    tm = logit.shape[0]
    onehot = ((lane == i1) | (lane == i2)).astype(BF16)
    ri = lax.broadcasted_iota(jnp.int32, (tm, tm), 0)
    ci = lax.broadcasted_iota(jnp.int32, (tm, tm), 1)
    before = jnp.dot((ri > ci).astype(BF16), onehot, preferred_element_type=F32) + cnt_scr[...]
    r1 = jnp.sum(jnp.where(lane == i1, before, 0.0), axis=-1, keepdims=True)
    r2 = jnp.sum(jnp.where(lane == i2, before, 0.0), axis=-1, keepdims=True)
    cnt_scr[...] = cnt_scr[...] + jnp.sum(onehot.astype(F32), axis=0, keepdims=True)
    rt = jnp.where(lane == 0, (i1 - N_GROUPS).astype(F32),
                   jnp.where(lane == 1, (i2 - N_GROUPS).astype(F32),
                             jnp.where(lane == 2, w1, jnp.where(lane == 3, w2,
                                                                jnp.where(lane == 4, r1,
                                                                          jnp.where(lane == 5, r2, 0.0))))))
    rt_ref[...] = rt

    @pl.when(step == pl.num_programs(0) - 1)
    def _():
        cout_ref[...] = cnt_scr[...]


def out_route(h, outs, weights_bf16, norm_w, w_group, b_group, w_router, b_router, counts_in, tm=512):
    n, d = h.shape
    tm = min(tm, n)
    assert n % tm == 0 and EXP_PER_GROUP == 8
    wr = jnp.zeros((d, LANES), F32).at[:, :N_GROUPS].set(w_group).at[:, N_GROUPS:N_GROUPS + N_EXPERTS].set(w_router)
    br = jnp.zeros((1, LANES), F32).at[0, :N_GROUPS].set(b_group).at[0, N_GROUPS:N_GROUPS + N_EXPERTS].set(b_router)
    row = lambda i: (i, 0)
    fixed = lambda i: (0, 0)
    in_specs = [pl.BlockSpec((tm, d), row)]
    in_specs += [pl.BlockSpec((tm, o.shape[1]), row) for o in outs]
    in_specs += [pl.BlockSpec(w.shape, fixed) for w in weights_bf16]
    in_specs += [pl.BlockSpec((1, d), fixed), pl.BlockSpec((d, LANES), fixed), pl.BlockSpec((1, LANES), fixed),
                 pl.BlockSpec((1, LANES), fixed)]
    return pl.pallas_call(
        functools.partial(_out_route_kernel, len(outs)),
        grid=(n // tm,), in_specs=in_specs,
        out_specs=[pl.BlockSpec((tm, d), row), pl.BlockSpec((tm, d), row), pl.BlockSpec((tm, LANES), row),
                   pl.BlockSpec((1, LANES), fixed)],
        out_shape=[jax.ShapeDtypeStruct((n, d), F32), jax.ShapeDtypeStruct((n, d), F32),
                   jax.ShapeDtypeStruct((n, LANES), F32), jax.ShapeDtypeStruct((1, LANES), F32)],
        scratch_shapes=[pltpu.VMEM((1, LANES), F32)],
        compiler_params=_params("arbitrary"), name="out_route",
    )(h, *outs, *weights_bf16, norm_w.reshape(1, d), wr, br, counts_in)


def _experts_kernel(te_ref, nv_ref, x_ref, wg_ref, wu_ref, wd_ref, y_ref):
    i = pl.program_id(0)

    @pl.when(i < nv_ref[0])
    def _():
        x = x_ref[...].astype(BF16)
        g = jnp.dot(x, wg_ref[0, 0].astype(BF16), preferred_element_type=F32)
        u = jnp.dot(x, wu_ref[0, 0].astype(BF16), preferred_element_type=F32)
        hh = (_silu(g) * u).astype(BF16)
        y_ref[...] = jnp.dot(hh, wd_ref[0, 0].astype(BF16), preferred_element_type=F32)

    @pl.when(i >= nv_ref[0])
    def _():
        y_ref[...] = jnp.zeros(y_ref.shape, F32)


def moe_experts(xs, tile_expert, n_valid, w_gate, w_up, w_down, layer, tm):
    p, d = xs.shape
    f = w_gate.shape[-1]
    n_tiles = p // tm
    grid_spec = pltpu.PrefetchScalarGridSpec(
        num_scalar_prefetch=2,
        grid=(n_tiles,),
        in_specs=[
            pl.BlockSpec((tm, d), lambda i, te, nv: (i, 0)),
            pl.BlockSpec((1, 1, d, f), lambda i, te, nv: (layer, te[i], 0, 0)),
            pl.BlockSpec((1, 1, d, f), lambda i, te, nv: (layer, te[i], 0, 0)),
            pl.BlockSpec((1, 1, f, d), lambda i, te, nv: (layer, te[i], 0, 0)),
        ],
        out_specs=pl.BlockSpec((tm, d), lambda i, te, nv: (i, 0)),
    )
    return pl.pallas_call(
        _experts_kernel, grid_spec=grid_spec,
        out_shape=jax.ShapeDtypeStruct((p, d), F32),
        compiler_params=_params("arbitrary"), name="moe_experts",
    )(tile_expert, n_valid, xs, w_gate, w_up, w_down)


def _take_rows(x, idx):
    return x.at[idx].get(mode="promise_in_bounds", unique_indices=True)


def hier_moe(xn, route, counts, w_gate, w_up, w_down, layer, tm=512):
    n, d = xn.shape
    e_idx = route[:, :TOP_K].astype(jnp.int32)
    rank = route[:, 2 * TOP_K:3 * TOP_K].astype(jnp.int32)
    cnt = counts[0, N_GROUPS:N_GROUPS + N_EXPERTS].astype(jnp.int32)
    padded = ((cnt + tm - 1) // tm) * tm
    pad_ends = jnp.cumsum(padded)
    pad_starts = pad_ends - padded
    experts = jnp.arange(N_EXPERTS, dtype=jnp.int32)
    pos = jnp.sum(jnp.where(e_idx[..., None] == experts, pad_starts, 0), axis=-1) + rank
    n_tiles = -(-(n * TOP_K + N_EXPERTS * (tm - 1)) // tm)
    p = n_tiles * tm
    tok = jnp.broadcast_to(jnp.arange(n, dtype=jnp.int32)[:, None], (n, TOP_K))
    n_valid = (pad_ends[-1] // tm).astype(jnp.int32)
    tile_start = jnp.minimum(jnp.arange(n_tiles, dtype=jnp.int32), n_valid - 1) * tm
    tile_e = jnp.sum((tile_start[:, None] >= pad_ends[None, :]).astype(jnp.int32), axis=-1)
    tile_e = jnp.minimum(tile_e, N_EXPERTS - 1)
    _, tok_sorted = lax.sort_key_val(pos.reshape(-1), tok.reshape(-1))
    starts = jnp.cumsum(cnt) - cnt
    rows = jnp.arange(p, dtype=jnp.int32).reshape(n_tiles, tm)
    off = rows - pad_starts[tile_e][:, None]
    src = jnp.where(off < cnt[tile_e][:, None], starts[tile_e][:, None] + off, rows % (n * TOP_K))
    row_tok = tok_sorted.at[src.reshape(-1)].get(mode="promise_in_bounds")
    xs = xn.at[row_tok].get(mode="promise_in_bounds")
    ys = moe_experts(xs, tile_e, n_valid.reshape(1), w_gate, w_up, w_down, layer, tm)
    return tuple(_take_rows(ys, pos[:, kk]) for kk in range(TOP_K))


def _final_norm_kernel(h_ref, rt_ref, a_ref, b_ref, nw_ref, o_ref):
    h = _add_expert_outputs(h_ref[...], rt_ref, (a_ref, b_ref))
    o_ref[...] = h * lax.rsqrt(jnp.mean(h * h, axis=-1, keepdims=True) + EPS) * nw_ref[...]


def final_norm(h, route, a, b, norm_w, add_row0=0, tm=256):
    n, d = h.shape
    tm = min(tm, n)
    assert add_row0 % tm == 0
    row = pl.BlockSpec((tm, d), lambda i: (i, 0))
    add_row = pl.BlockSpec((tm, d), lambda i: (i + add_row0 // tm, 0))
    return pl.pallas_call(
        _final_norm_kernel, grid=(n // tm,),
        in_specs=[row, pl.BlockSpec((tm, LANES), lambda i: (i + add_row0 // tm, 0)), add_row, add_row,
                  pl.BlockSpec((1, d), lambda i: (0, 0))],
        out_specs=row, out_shape=jax.ShapeDtypeStruct((n, d), F32),
        compiler_params=_params("parallel"), name="final_norm",
    )(h, route, a, b, norm_w.reshape(1, d))


def kernel(x_prompt, x_sample, state_a_conv, state_a_ssm, cache_b_k, cache_b_v, cache_c_k, cache_c_v, page_table, norm_mix, norm_ffn, norm_final, rel_bias, w_in0, conv_w, a_log, dt_bias, gdn_norm_w, diff_lambda, diff_subln_w, w_out0, w_in1, b_in1, sinks, w_out1, w_group, b_group, w_router, b_router, w_gate, w_up, w_down):
    bp, tp, d = x_prompt.shape
    sn, sl, _ = x_sample.shape
    n_p, n_s = bp * tp, sn * sl
    parts = ((x_prompt.reshape(n_p, d), bp, tp), (x_sample.reshape(n_s, d), sn, sl))

    off_z = CONV_CH
    off_a = off_z + H_A * DV_A
    off_qb = off_a + 2 * H_A
    w0 = jnp.concatenate([w_in0[:, :off_a], w_in0[:, off_a:off_qb],
                          jnp.zeros((d, LANES - 2 * H_A), w_in0.dtype), w_in0[:, off_qb:]], axis=1).astype(BF16)
    widths0 = (CONV_CH, H_A * DV_A, LANES, B_W, B_W, B_W)
    w_out0_bf = w_out0.astype(BF16)
    w_out0_parts = [w_out0_bf[:H_A * DV_A], w_out0_bf[H_A * DV_A:]]

    row0 = (0, n_p)

    def moe_layer(layer, xns, routes, counts):
        xn_all = jnp.concatenate(xns, axis=0)
        route_all = jnp.concatenate(routes, axis=0)
        return (route_all,) + hier_moe(xn_all, route_all, counts, w_gate, w_up, w_down, layer)

    conv_states = (jnp.zeros((bp, CONV_W - 1, CONV_CH), F32), state_a_conv)
    ssm_states = (jnp.zeros((bp, H_A, DK_A, DV_A), F32), state_a_ssm)
    hs, xns, routes, conv_new, ssm_new, kb_new, vb_new = [], [], [], [], [], [], []
    counts = jnp.zeros((1, LANES), F32)
    for idx, (x2, b, t) in enumerate(parts):
        qkv, z, ab, qb, kb, vb, kb4, vb4 = norm_proj(x2, [], norm_mix[0], w0, None, widths0, emit_h=False,
                                                     head_split=(4, 5))
        r3 = lambda a: a.reshape(b, t, a.shape[-1])
        o_a, s_new = gdn_heads(r3(qkv), r3(z), r3(ab), conv_states[idx], ssm_states[idx], conv_w, a_log, dt_bias,
                               gdn_norm_w, min(GDN_CHUNK, t), 2 if idx == 0 else 8)
        if idx == 0:
            o_b = diff_attn_prompt(r3(qb), r3(kb), r3(vb), rel_bias, diff_lambda, diff_subln_w, 512)
        else:
            o_b = diff_attn_sample(r3(qb), r3(kb), r3(vb), cache_b_k, cache_b_v, page_table, rel_bias, diff_lambda,
                                   diff_subln_w, 16)
        h1, xn, route, counts = out_route(x2, [o_a.reshape(b * t, -1), o_b.reshape(b * t, -1)], w_out0_parts,
                                          norm_ffn[0], w_group[0], b_group[0], w_router[0], b_router[0], counts)
        hs.append(h1), xns.append(xn), routes.append(route)
        conv_new.append(r3(qkv)[:, t - (CONV_W - 1):, :])
        ssm_new.append(s_new)
        kb_new.append(kb4.reshape(b, t, H_B, 2 * DH_B))
        vb_new.append(vb4.reshape(b, t, H_B, 2 * DH_B))
    moe_out = moe_layer(0, xns, routes, counts)

    w1 = w_in1.astype(BF16)
    w_out1_bf = w_out1.astype(BF16)
    hs2, xns, routes, kc_new, vc_new = [], [], [], [], []
    counts = jnp.zeros((1, LANES), F32)
    for idx, (_, b, t) in enumerate(parts):
        h2, q, k, v = norm_proj(hs[idx], moe_out, norm_mix[1], w1, b_in1, (Q_C, KV_C, KV_C),
                                emit_h=True, add_row0=row0[idx])
        r3 = lambda a: a.reshape(b, t, a.shape[-1])
        if idx == 0:
            o_c = swa_prompt(r3(q), r3(k), r3(v), rel_bias, sinks)
            kc_new.append(r3(k)[:, t - WINDOW:].reshape(b, WINDOW, H_KV_C, DH_C))
            vc_new.append(r3(v)[:, t - WINDOW:].reshape(b, WINDOW, H_KV_C, DH_C))
        else:
            o_c = swa_sample(r3(q), r3(k), r3(v), cache_c_k, cache_c_v, rel_bias, sinks, 8)
            kc_new.append(jnp.concatenate([cache_c_k[:, t:], k.reshape(b, t, H_KV_C, DH_C)], axis=1))
            vc_new.append(jnp.concatenate([cache_c_v[:, t:], v.reshape(b, t, H_KV_C, DH_C)], axis=1))
        h3, xn, route, counts = out_route(h2, [o_c.reshape(b * t, -1)], [w_out1_bf], norm_ffn[1],
                                          w_group[1], b_group[1], w_router[1], b_router[1], counts)
        hs2.append(h3), xns.append(xn), routes.append(route)
    moe_out = moe_layer(1, xns, routes, counts)

    y_out = [final_norm(hs2[idx], *moe_out, norm_final, add_row0=row0[idx]).reshape(b, t, d)
             for idx, (_, b, t) in enumerate(parts)]
    return (y_out[0], y_out[1], conv_new[0], conv_new[1], ssm_new[0], ssm_new[1],
            kb_new[0], vb_new[0], kb_new[1], vb_new[1], kc_new[0], vc_new[0], kc_new[1], vc_new[1])
```

```python
import functools
import math

import jax
import jax.numpy as jnp
import numpy as np
from jax import lax
from jax.experimental import pallas as pl
from jax.experimental.pallas import tpu as pltpu

F32 = jnp.float32
BF16 = jnp.bfloat16
EPS = 1e-6
NEG = -1e30
LOG2E = math.log2(math.e)
ONES_ROWS = 16
LANES = 128
VMEM_LIMIT = 56 * 1024 * 1024

D_MODEL = 1024
H_A, DK_A, DV_A, CONV_W = 4, 128, 128, 4
CONV_CH = 2 * H_A * DK_A + H_A * DV_A
GDN_CHUNK = 64
GDN_PRECISE_INV = False
H_B, DH_B = 4, 64
B_W = H_B * 2 * DH_B
PAGE = 128
H_C, H_KV_C, DH_C, WINDOW = 16, 2, 64, 128
G_C = H_C // H_KV_C
Q_C = H_C * DH_C
KV_C = H_KV_C * DH_C
N_BUCKETS, MAX_DIST = 32, 128
N_GROUPS, EXP_PER_GROUP, TOP_K, D_EXPERT = 4, 8, 2, 256
N_EXPERTS = N_GROUPS * EXP_PER_GROUP
LAM_INIT0 = 0.8 - 0.6 * math.exp(-0.3 * 0)


def _params(*sem):
    return pltpu.CompilerParams(dimension_semantics=sem, vmem_limit_bytes=VMEM_LIMIT)


def _mm(a, b, precise=False):
    if precise:
        return jnp.dot(a.astype(F32), b.astype(F32), preferred_element_type=F32, precision=lax.Precision.HIGHEST)
    return jnp.dot(a.astype(BF16), b.astype(BF16), preferred_element_type=F32)


def _mm_nt(a, b, precise=False):
    dn = (((1,), (1,)), ((), ()))
    if precise:
        return lax.dot_general(a.astype(F32), b.astype(F32), dn, preferred_element_type=F32,
                               precision=lax.Precision.HIGHEST)
    return lax.dot_general(a.astype(BF16), b.astype(BF16), dn, preferred_element_type=F32)


def _mm_tn(a, b, precise=False):
    dn = (((0,), (0,)), ((), ()))
    if precise:
        return lax.dot_general(a.astype(F32), b.astype(F32), dn, preferred_element_type=F32,
                               precision=lax.Precision.HIGHEST)
    return lax.dot_general(a.astype(BF16), b.astype(BF16), dn, preferred_element_type=F32)


def _sigmoid(x):
    return 1.0 / (1.0 + jnp.exp(-x))


def _silu(x):
    return x * _sigmoid(x)


def _softplus(x):
    return jnp.maximum(x, 0.0) + jnp.log(1.0 + jnp.exp(-jnp.abs(x)))


def _add_expert_outputs(h, route_ref, y_refs):
    rt = route_ref[...]
    for kk, y_ref in enumerate(y_refs):
        h = h + rt[:, TOP_K + kk:TOP_K + kk + 1] * y_ref[...]
    return h


def _norm_proj_kernel(n_add, widths, has_bias, emit_h, head_split, *refs):
    x_ref = refs[0]
    add_refs = refs[1:1 + n_add]
    nw_ref, w_ref = refs[1 + n_add], refs[2 + n_add]
    pos = 3 + n_add
    b_ref = None
    if has_bias:
        b_ref = refs[pos]
        pos += 1
    outs = refs[pos:]
    h = x_ref[...]
    if n_add:
        h = _add_expert_outputs(h, add_refs[0], add_refs[1:])
    if emit_h:
        outs[0][...] = h
        outs = outs[1:]
    split_refs = outs[len(widths):]
    xn = h * lax.rsqrt(jnp.mean(h * h, axis=-1, keepdims=True) + EPS) * nw_ref[...]
    xb = xn.astype(BF16)
    off = 0
    for j, (o_ref, wd) in enumerate(zip(outs, widths)):
        y = jnp.dot(xb, w_ref[:, off:off + wd], preferred_element_type=F32)
        if has_bias:
            y = y + b_ref[:, off:off + wd]
        o_ref[...] = y
        if j in head_split:
            s_ref = split_refs[head_split.index(j)]
            for hh in range(wd // LANES):
                s_ref[:, hh, :] = y[:, hh * LANES:(hh + 1) * LANES]
        off += wd


def norm_proj(x, addends, norm_w, w_bf16, bias, widths, emit_h, head_split=(), add_row0=0, tm=512):
    n, d = x.shape
    m = w_bf16.shape[1]
    tm = min(tm, n)
    assert sum(widths) == m and n % tm == 0
    row = lambda i: (i, 0)
    fixed = lambda i: (0, 0)
    assert add_row0 % tm == 0
    add_row = lambda i: (i + add_row0 // tm, 0)
    in_specs = [pl.BlockSpec((tm, d), row)] + [pl.BlockSpec((tm, a.shape[1]), add_row) for a in addends]
    in_specs += [pl.BlockSpec((1, d), fixed), pl.BlockSpec((d, m), fixed)]
    args = [x, *addends, norm_w.reshape(1, d), w_bf16]
    if bias is not None:
        in_specs.append(pl.BlockSpec((1, m), fixed))
        args.append(bias.reshape(1, m))
    out_shape, out_specs = [], []
    if emit_h:
        out_shape.append(jax.ShapeDtypeStruct((n, d), F32))
        out_specs.append(pl.BlockSpec((tm, d), row))
    for wd in widths:
        out_shape.append(jax.ShapeDtypeStruct((n, wd), F32))
        out_specs.append(pl.BlockSpec((tm, wd), row))
    for j in head_split:
        out_shape.append(jax.ShapeDtypeStruct((n, widths[j] // LANES, LANES), F32))
        out_specs.append(pl.BlockSpec((tm, widths[j] // LANES, LANES), lambda i: (i, 0, 0)))
    return pl.pallas_call(
        functools.partial(_norm_proj_kernel, len(addends), tuple(widths), bias is not None, emit_h,
                          tuple(head_split)),
        grid=(n // tm,), in_specs=in_specs, out_specs=out_specs, out_shape=out_shape,
        compiler_params=_params("parallel"), name="norm_proj",
    )(*args)


def _gdn_kernel(c_len, nb, qkv_ref, z_ref, ab_ref, cbuf_ref, s0_ref, convw_ref, gpar_ref, nw_ref,
                o_ref, sfin_ref, xp_scr, s_scr):
    c = pl.program_id(1)
    n_c = pl.num_programs(1)
    hist = CONV_W - 1
    base = 8 - hist

    @pl.when(c == 0)
    def _():
        xp_scr[:, base:8, :] = cbuf_ref[...]
        s_scr[...] = s0_ref[...]

    ri = lax.broadcasted_iota(jnp.int32, (c_len, c_len), 0)
    ci = lax.broadcasted_iota(jnp.int32, (c_len, c_len), 1)
    lower = ri >= ci
    strict = ri > ci
    eye = (ri == ci).astype(F32)
    lower_f = lower.astype(F32)

    units = []
    for bi in range(nb):
        xp_scr[bi, 8:8 + c_len, :] = qkv_ref[bi]
        y = xp_scr[bi, base:base + c_len, :] * convw_ref[0:1, :]
        for j in range(1, CONV_W):
            y = y + xp_scr[bi, base + j:base + j + c_len, :] * convw_ref[j:j + 1, :]
        xp_scr[bi, base:8, :] = xp_scr[bi, base + c_len:8 + c_len, :]
        y = _silu(y)

        ab = ab_ref[bi]
        g_t = gpar_ref[0:1, :] * _softplus(ab + gpar_ref[1:2, :])
        beta_t = _sigmoid(ab)
        gcum_t = _mm(lower_f, g_t, precise=True)
        gcum_tt = gcum_t.T

        for h in range(H_A):
            q = y[:, h * DK_A:(h + 1) * DK_A]
            k = y[:, (H_A + h) * DK_A:(H_A + h + 1) * DK_A]
            v = y[:, 2 * H_A * DK_A + h * DV_A:2 * H_A * DK_A + (h + 1) * DV_A]
            q = q * lax.rsqrt(jnp.sum(q * q, axis=-1, keepdims=True) + EPS) * (DK_A ** -0.5)
            k = k * lax.rsqrt(jnp.sum(k * k, axis=-1, keepdims=True) + EPS)
            gc = gcum_t[:, h:h + 1]
            gr = gcum_tt[h:h + 1, :]
            beta = beta_t[:, H_A + h:H_A + h + 1]
            decay = jnp.where(lower, jnp.exp(jnp.where(lower, gc - gr, 0.0)), 0.0)
            kb = k * beta
            units.append(dict(bi=bi, h=h, q=q, k=k, k_bf=k.astype(BF16), gc=gc, decay=decay, kb=kb,
                              rhs=jnp.concatenate([v * beta, kb * jnp.exp(gc)], axis=1)))

    for un in units:
        un["nmat"] = jnp.where(strict, _mm_nt(un["kb"], un["k_bf"]) * un["decay"], 0.0)
        un["qk"] = jnp.where(lower, _mm_nt(un["q"], un["k_bf"]) * un["decay"], 0.0)
    for un in units:
        un["inv"] = eye - un["nmat"]
        un["pw"] = _mm(un["nmat"], un["nmat"], GDN_PRECISE_INV)
    span = 2
    while span < c_len:
        for un in units:
            un["inv"] = un["inv"] + _mm(un["inv"], un["pw"], GDN_PRECISE_INV)
        span *= 2
        if span < c_len:
            for un in units:
                un["pw"] = _mm(un["pw"], un["pw"], GDN_PRECISE_INV)
    for un in units:
        sol = _mm(un["inv"], un["rhs"], GDN_PRECISE_INV)
        un["u"], un["w"] = sol[:, :DV_A], sol[:, DV_A:]
        un["s"] = s_scr[un["bi"], un["h"]]
        un["s_bf"] = un["s"].astype(BF16)
    for un in units:
        un["v_new"] = un["u"] - _mm(un["w"], un["s_bf"])
        un["o_s"] = _mm(un["q"] * jnp.exp(un["gc"]), un["s_bf"])
    for un in units:
        bi, h, gc = un["bi"], un["h"], un["gc"]
        o = un["o_s"] + _mm(un["qk"], un["v_new"])
        g_last = gc[c_len - 1:c_len, :]
        s_scr[bi, h] = un["s"] * jnp.exp(g_last) + _mm_tn(un["k"] * jnp.exp(g_last - gc), un["v_new"])
        o = o * lax.rsqrt(jnp.mean(o * o, axis=-1, keepdims=True) + EPS) * nw_ref[...]
        o_ref[bi, :, h * DV_A:(h + 1) * DV_A] = o * _silu(z_ref[bi, :, h * DV_A:(h + 1) * DV_A])

    @pl.when(c == n_c - 1)
    def _():
        sfin_ref[...] = s_scr[...]


def gdn_heads(qkv, z, ab, conv_buf, s0, conv_w, a_log, dt_bias, norm_w, c_len, nb):
    b, t, _ = qkv.shape
    assert t % c_len == 0 and b % nb == 0
    convw_t = conv_w.T
    gpar = jnp.zeros((2, LANES), F32)
    gpar = gpar.at[0, :H_A].set(-jnp.exp(a_log.astype(F32))).at[1, :H_A].set(dt_bias.astype(F32))
    tok = lambda i, j: (i, j, 0)
    seq3 = lambda i, j: (i, 0, 0)
    fixed = lambda i, j: (0, 0)
    return pl.pallas_call(
        functools.partial(_gdn_kernel, c_len, nb),
        grid=(b // nb, t // c_len),
        in_specs=[
            pl.BlockSpec((nb, c_len, CONV_CH), tok),
            pl.BlockSpec((nb, c_len, H_A * DV_A), tok),
            pl.BlockSpec((nb, c_len, LANES), tok),
            pl.BlockSpec((nb, CONV_W - 1, CONV_CH), seq3),
            pl.BlockSpec((nb, H_A, DK_A, DV_A), lambda i, j: (i, 0, 0, 0)),
            pl.BlockSpec((CONV_W, CONV_CH), fixed),
            pl.BlockSpec((2, LANES), fixed),
            pl.BlockSpec((1, DV_A), fixed),
        ],
        out_specs=[
            pl.BlockSpec((nb, c_len, H_A * DV_A), tok),
            pl.BlockSpec((nb, H_A, DK_A, DV_A), lambda i, j: (i, 0, 0, 0)),
        ],
        out_shape=[
            jax.ShapeDtypeStruct((b, t, H_A * DV_A), F32),
            jax.ShapeDtypeStruct((b, H_A, DK_A, DV_A), F32),
        ],
        scratch_shapes=[
            pltpu.VMEM((nb, 8 + c_len, CONV_CH), F32),
            pltpu.VMEM((nb, H_A, DK_A, DV_A), F32),
        ],
        compiler_params=_params("parallel", "arbitrary"), name="gdn_heads",
    )(qkv, z, ab, conv_buf, s0, convw_t, gpar, norm_w.reshape(1, DV_A))


def _t5_bucket_np(dist):
    d = np.maximum(dist, 0)
    max_exact = N_BUCKETS // 2
    ratio = (np.log(np.maximum(d, 1).astype(np.float32) / np.float32(max_exact))
             / np.float32(math.log(MAX_DIST / max_exact)))
    large = np.minimum(max_exact + (ratio * (N_BUCKETS - max_exact)).astype(np.int32), N_BUCKETS - 1)
    return np.where(d < max_exact, d, large).astype(np.int32)


FAR_BUCKET = int(_t5_bucket_np(np.array([MAX_DIST]))[0])
assert np.all(_t5_bucket_np(np.arange(MAX_DIST, 4 * MAX_DIST)) == FAR_BUCKET)


def _bias_kernel(col_group, bucket_ref, rb_ref, out_ref):
    for col, grp in enumerate(col_group):
        bk = bucket_ref[grp]
        acc = jnp.full(bk.shape, NEG, F32)
        for b in range(N_BUCKETS):
            acc = jnp.where(bk == b, rb_ref[b, col], acc)
        out_ref[col] = acc


def bias_tiles(rel_bias, dist, valid, n_cols, col_group=None):
    bucket = np.where(valid, _t5_bucket_np(dist), -1).astype(np.int32)
    if bucket.ndim == 2:
        bucket, col_group = bucket[None], (0,) * n_cols
    _, r, c = bucket.shape
    return pl.pallas_call(
        functools.partial(_bias_kernel, tuple(col_group)),
        in_specs=[pl.BlockSpec(memory_space=pltpu.VMEM), pl.BlockSpec(memory_space=pltpu.SMEM)],
        out_specs=pl.BlockSpec(memory_space=pltpu.VMEM),
        out_shape=jax.ShapeDtypeStruct((n_cols, r, c), F32),
        name="bias_tiles",
    )(jnp.asarray(bucket), rel_bias.astype(F32))


def _diff_lambda(lam_ref):
    lp = lam_ref[...]
    s1 = jnp.sum(lp[0:1] * lp[1:2], axis=-1, keepdims=True)
    s2 = jnp.sum(lp[2:3] * lp[3:4], axis=-1, keepdims=True)
    return jnp.exp(s1) - jnp.exp(s2) + LAM_INIT0


def _diff_finish(o1, o2, lam, nw):
    o = o1 - lam * o2
    return o * lax.rsqrt(jnp.mean(o * o, axis=-1, keepdims=True) + EPS) * nw * (1.0 - LAM_INIT0)


def _softmax_update(s, v_bf, m_ref, l_ref, acc_ref):
    m_prev = m_ref[...]
    m_new = jnp.maximum(m_prev, jnp.max(s, axis=-1, keepdims=True))
    alpha = jnp.exp(m_prev - m_new)
    p = jnp.exp(s - m_new)
    l_ref[...] = alpha * l_ref[...] + jnp.sum(p, axis=-1, keepdims=True)
    acc_ref[...] = alpha * acc_ref[...] + jnp.dot(p.astype(BF16), v_bf, preferred_element_type=F32)
    m_ref[...] = m_new


def _diff_prompt_kernel(blk, qt_ref, kt_ref, q_ref, k_ref, v_ref, t0_ref, t1_ref, far_ref, lam_ref, nw_ref, o_ref,
                        m_scr, acc_scr):
    qi = qt_ref[pl.program_id(1)]
    ki = kt_ref[pl.program_id(1)]
    nsub = blk // LANES
    hw = 2 * DH_B

    @pl.when(ki == 0)
    def _():
        m_scr[...] = jnp.full(m_scr.shape, NEG, F32)
        acc_scr[...] = jnp.zeros(acc_scr.shape, F32)

    def bias_block(case, hm):
        const = jnp.full((LANES, LANES), far_ref[hm], F32)
        neg = jnp.full((LANES, LANES), NEG, F32)
        rows = []
        for jj in range(nsub):
            tiles = []
            for ii in range(nsub):
                sub = ii - jj + (nsub if case == "prev" else 0)
                tiles.append(neg if sub < 0 else t0_ref[hm] if sub == 0 else t1_ref[hm] if sub == 1 else const)
            rows.append(jnp.concatenate(tiles, axis=1))
        return jnp.concatenate(rows, axis=0)

    def step(case):
        q = (q_ref[0] * (DH_B ** -0.5 * LOG2E)).astype(BF16)
        k = k_ref[0].astype(BF16)
        v_t = v_ref[0].T
        first = lax.broadcasted_iota(jnp.int32, (1, hw), 1) < DH_B
        ones = jnp.ones((ONES_ROWS, blk), F32)
        s_all, v_ext = [], []
        for h in range(H_B):
            sl = slice(h * hw, (h + 1) * hw)
            qh, kh = q[:, sl], k[:, sl]
            v_ext.append(jnp.concatenate([v_t[sl], ones], axis=0).astype(BF16))
            for mp in range(2):
                km = jnp.where(first if mp == 0 else jnp.logical_not(first), kh, jnp.zeros_like(kh))
                s_all.append(lax.dot_general(km, qh, (((1,), (1,)), ((), ())),
                                             preferred_element_type=F32))
        p_all, alpha_all = [], []
        for hm, s in enumerate(s_all):
            m_prev = m_scr[hm]
            if case == "far":
                shift = far_ref[hm]
                m_new = jnp.maximum(m_prev, jnp.max(s, axis=0, keepdims=True) + shift)
                p = jnp.exp2(s + (shift - m_new))
            else:
                s = s + bias_block(case, hm)
                m_new = jnp.maximum(m_prev, jnp.max(s, axis=0, keepdims=True))
                p = jnp.exp2(s - m_new)
            alpha_all.append(jnp.exp2(m_prev - m_new))
            p_all.append(p.astype(BF16))
            m_scr[hm] = m_new
        for hm, (p, alpha) in enumerate(zip(p_all, alpha_all)):
            acc_scr[hm] = alpha * acc_scr[hm] + jnp.dot(v_ext[hm // 2], p, preferred_element_type=F32)

    @pl.when(ki < qi - 1)
    def _():
        step("far")

    @pl.when(ki == qi - 1)
    def _():
        step("prev")

    @pl.when(ki == qi)
    def _():
        step("diag")
        lam = _diff_lambda(lam_ref)
        for h in range(H_B):
            a1, a2 = acc_scr[2 * h], acc_scr[2 * h + 1]
            o = a1[:hw] / a1[hw:hw + 1] - lam * (a2[:hw] / a2[hw:hw + 1])
            o = o * lax.rsqrt(jnp.mean(o * o, axis=0, keepdims=True) + EPS) * nw_ref[...] * (1.0 - LAM_INIT0)
            o_ref[0, :, h * hw:(h + 1) * hw] = o.T


def diff_attn_prompt(q, k, v, rel_bias, diff_lambda, subln_w, blk):
    b, t, _ = q.shape
    assert t % blk == 0 and blk % LANES == 0 and blk >= 2 * LANES
    i = np.arange(LANES)
    d0 = i[None, :] - i[:, None]
    t0 = bias_tiles(rel_bias, d0, d0 >= 0, 2 * H_B) * LOG2E
    t1 = bias_tiles(rel_bias, d0 + LANES, np.ones_like(d0, bool), 2 * H_B) * LOG2E
    far = rel_bias[FAR_BUCKET, :2 * H_B].astype(F32) * LOG2E
    nb = t // blk
    hw = 2 * DH_B
    pairs = [(qi, ki) for qi in range(nb) for ki in range(qi + 1)]
    q_tab = jnp.asarray(np.array([p[0] for p in pairs], np.int32))
    k_tab = jnp.asarray(np.array([p[1] for p in pairs], np.int32))
    whole = lambda shape: pl.BlockSpec(shape, lambda bi, pi, qt, kt: (0,) * len(shape))
    grid_spec = pltpu.PrefetchScalarGridSpec(
        num_scalar_prefetch=2,
        grid=(b, len(pairs)),
        in_specs=[
            pl.BlockSpec((1, blk, B_W), lambda bi, pi, qt, kt: (bi, qt[pi], 0)),
            pl.BlockSpec((1, blk, B_W), lambda bi, pi, qt, kt: (bi, kt[pi], 0)),
            pl.BlockSpec((1, blk, B_W), lambda bi, pi, qt, kt: (bi, kt[pi], 0)),
            whole((2 * H_B, LANES, LANES)),
            whole((2 * H_B, LANES, LANES)),
            pl.BlockSpec(memory_space=pltpu.SMEM),
            whole((4, DH_B)),
            whole((hw, 1)),
        ],
        out_specs=pl.BlockSpec((1, blk, B_W), lambda bi, pi, qt, kt: (bi, qt[pi], 0)),
        scratch_shapes=[
            pltpu.VMEM((2 * H_B, 1, blk), F32),
            pltpu.VMEM((2 * H_B, hw + ONES_ROWS, blk), F32),
        ],
    )
    return pl.pallas_call(
        functools.partial(_diff_prompt_kernel, blk),
        grid_spec=grid_spec,
        out_shape=jax.ShapeDtypeStruct((b, t, B_W), F32),
        compiler_params=_params("parallel", "arbitrary"), name="diff_attn_prompt",
    )(q_tab, k_tab, q, k, v, t0, t1, far, diff_lambda.astype(F32), subln_w.reshape(hw, 1).astype(F32))


def _diff_sample_kernel(n_pg, s_len, pt_ref, q_ref, kn_ref, vn_ref, *refs):
    k_pages = refs[:n_pg]
    v_pages = refs[n_pg:2 * n_pg]
    blast_ref, bnew_ref, far_ref, lam_ref, nw_ref, o_ref, m_scr, l_scr, acc_scr = refs[2 * n_pg:]
    j = pl.program_id(1)
    last = pl.num_programs(1) - 1
    hw = 2 * DH_B

    @pl.when(j == 0)
    def _():
        m_scr[...] = jnp.full(m_scr.shape, NEG, F32)
        l_scr[...] = jnp.zeros(l_scr.shape, F32)
        acc_scr[...] = jnp.zeros(acc_scr.shape, F32)

    q = q_ref[0] * (DH_B ** -0.5)
    first = lax.broadcasted_iota(jnp.int32, (1, hw), 1) < DH_B
    rh = 2 * s_len
    q_heads = []
    for h in range(H_B):
        qh = q[:, h * hw:(h + 1) * hw]
        q_heads.append(jnp.concatenate([jnp.where(first, qh, 0.0), jnp.where(first, 0.0, qh)], axis=0).astype(BF16))

    def head_rows(ref, pages, h):
        return jnp.concatenate([r[0, pl.ds(h, PAGE, stride=H_B), :] for r in pages], axis=0).astype(BF16)

    n_grp = 4 if n_pg % 4 == 0 else 1
    per = n_pg // n_grp
    wg = per * PAGE
    s_units = [[None] * n_grp for _ in range(H_B)]
    for g in range(n_grp):
        for h in range(H_B):
            rows = slice(h * rh, (h + 1) * rh)
            bias = jnp.where(j == last, blast_ref[rows, g * wg:(g + 1) * wg], far_ref[rows, :])
            kgh = head_rows(None, k_pages[g * per:(g + 1) * per], h)
            s_units[h][g] = lax.dot_general(q_heads[h], kgh, (((1,), (1,)), ((), ())),
                                            preferred_element_type=F32) + bias
    m_news, alphas = [], []
    for h in range(H_B):
        m_prev = m_scr[h * rh:(h + 1) * rh]
        m_new = m_prev
        for s in s_units[h]:
            m_new = jnp.maximum(m_new, jnp.max(s, axis=-1, keepdims=True))
        m_news.append(m_new)
        alphas.append(jnp.exp(m_prev - m_new))
    l_new = [alphas[h] * l_scr[h * rh:(h + 1) * rh] for h in range(H_B)]
    acc_new = [alphas[h] * acc_scr[h * rh:(h + 1) * rh] for h in range(H_B)]
    for g in range(n_grp):
        for h in range(H_B):
            p = jnp.exp(s_units[h][g] - m_news[h])
            vgh = head_rows(None, v_pages[g * per:(g + 1) * per], h)
            l_new[h] = l_new[h] + jnp.sum(p, axis=-1, keepdims=True)
            acc_new[h] = acc_new[h] + jnp.dot(p.astype(BF16), vgh, preferred_element_type=F32)
    for h in range(H_B):
        m_scr[h * rh:(h + 1) * rh] = m_news[h]
        l_scr[h * rh:(h + 1) * rh] = l_new[h]
        acc_scr[h * rh:(h + 1) * rh] = acc_new[h]

    @pl.when(j == last)
    def _():
        pad = jnp.zeros((PAGE - s_len, hw), F32)
        for h in range(H_B):
            rows = slice(h * rh, (h + 1) * rh)
            kn = jnp.concatenate([kn_ref[0, :, h * hw:(h + 1) * hw], pad], axis=0).astype(BF16)
            vn = jnp.concatenate([vn_ref[0, :, h * hw:(h + 1) * hw], pad], axis=0).astype(BF16)
            s2 = lax.dot_general(q_heads[h], kn, (((1,), (1,)), ((), ())), preferred_element_type=F32)
            _softmax_update(s2 + bnew_ref[rows], vn, m_scr.at[rows], l_scr.at[rows], acc_scr.at[rows])
        accn = acc_scr[...] / l_scr[...]
        lam = _diff_lambda(lam_ref)
        for h in range(H_B):
            o1 = accn[(2 * h) * s_len:(2 * h + 1) * s_len]
            o2 = accn[(2 * h + 1) * s_len:(2 * h + 2) * s_len]
            o_ref[0, :, h * hw:(h + 1) * hw] = _diff_finish(o1, o2, lam, nw_ref[...])


def diff_attn_sample(q, k, v, cache_k, cache_v, page_table, rel_bias, diff_lambda, subln_w, n_pg):
    s_n, s_len, _ = q.shape
    pages_per_seq = page_table.shape[1]
    assert pages_per_seq % n_pg == 0
    past = pages_per_seq * PAGE
    n_phys = cache_k.shape[0]
    hw = 2 * DH_B
    ck = cache_k.reshape(n_phys, PAGE * H_B, hw)
    cv = cache_v.reshape(n_phys, PAGE * H_B, hw)
    n_rows = 2 * H_B * s_len
    blk_keys = n_pg * PAGE
    assert PAGE >= s_len
    t = np.arange(s_len)
    d_last = (past + t)[:, None] - (past - blk_keys + np.arange(blk_keys))[None, :]
    blast = bias_tiles(rel_bias, d_last, np.ones_like(d_last, bool), 2 * H_B).reshape(n_rows, blk_keys)
    far = bias_tiles(rel_bias, np.full((s_len, LANES), MAX_DIST), np.ones((s_len, LANES), bool), 2 * H_B)
    far = far.reshape(n_rows, LANES)[:, :1]
    cnew = np.arange(PAGE)
    d_new = t[:, None] - cnew[None, :]
    bnew = bias_tiles(rel_bias, d_new, (d_new >= 0) & (cnew[None, :] < s_len), 2 * H_B).reshape(n_rows, PAGE)
    assert past - blk_keys + s_len - 1 >= 0 and np.all(_t5_bucket_np(np.array([blk_keys + 1])) == FAR_BUCKET)

    def page_spec(i):
        return pl.BlockSpec((1, PAGE * H_B, hw), lambda s, j, pt: (pt[s * pages_per_seq + j * n_pg + i], 0, 0))

    seq = pl.BlockSpec((1, s_len, B_W), lambda s, j, pt: (s, 0, 0))
    whole = lambda shape: pl.BlockSpec(shape, lambda s, j, pt: (0,) * len(shape))
    grid_spec = pltpu.PrefetchScalarGridSpec(
        num_scalar_prefetch=1,
        grid=(s_n, pages_per_seq // n_pg),
        in_specs=[seq, seq, seq] + [page_spec(i) for i in range(n_pg)] * 2 + [
            whole((n_rows, blk_keys)), whole((n_rows, PAGE)), whole((n_rows, 1)),
            whole((4, DH_B)), whole((1, hw))],
        out_specs=seq,
        scratch_shapes=[
            pltpu.VMEM((n_rows, 1), F32),
            pltpu.VMEM((n_rows, 1), F32),
            pltpu.VMEM((n_rows, hw), F32),
        ],
    )
    return pl.pallas_call(
        functools.partial(_diff_sample_kernel, n_pg, s_len),
        grid_spec=grid_spec,
        out_shape=jax.ShapeDtypeStruct((s_n, s_len, B_W), F32),
        compiler_params=_params("parallel", "arbitrary"), name="diff_attn_sample",
    )(page_table.reshape(-1).astype(jnp.int32), q, k, v, *([ck] * n_pg), *([cv] * n_pg),
      blast, bnew, far, diff_lambda.astype(F32), subln_w.reshape(1, 2 * DH_B).astype(F32))


def _kv_variants(x, kv):
    lo = lax.broadcasted_iota(jnp.int32, (1, LANES), 1) < DH_C
    rolled = pltpu.roll(x, DH_C, 1)
    a_src, b_src = (x, rolled) if kv == 0 else (rolled, x)
    zero = jnp.zeros_like(x)
    return jnp.where(lo, a_src, zero).astype(BF16), jnp.where(lo, zero, b_src).astype(BF16)


def _sink_attend(units):
    scores = [[lax.dot_general(q_bf, kx, (((1,), (1,)), ((), ())), preferred_element_type=F32) + bias
               for kx, bias in zip(k_ab, bias_ab)] for q_bf, k_ab, _, bias_ab, _ in units]
    probs = []
    for s_ab, (_, _, _, _, sink_ab) in zip(scores, units):
        p_ab = []
        for s, sink in zip(s_ab, sink_ab):
            m = jnp.maximum(jnp.max(s, axis=-1, keepdims=True), sink)
            e = jnp.exp(s - m)
            p_ab.append((e / (jnp.sum(e, axis=-1, keepdims=True) + jnp.exp(sink - m))).astype(BF16))
        probs.append(p_ab)
    return [jnp.dot(p_ab[0], v_ab[0], preferred_element_type=F32) + jnp.dot(p_ab[1], v_ab[1], preferred_element_type=F32)
            for p_ab, (_, _, v_ab, _, _) in zip(probs, units)]


def _swa_prompt_kernel(q_ref, kp_ref, kc_ref, vp_ref, vc_ref, bias_ref, sink_ref, o_ref):
    n = pl.program_id(1)
    keys = jnp.concatenate([kp_ref[0], kc_ref[0]], axis=0)
    vals = jnp.concatenate([vp_ref[0], vc_ref[0]], axis=0)
    col = lax.broadcasted_iota(jnp.int32, (1, 2 * WINDOW), 1)
    first_blk = jnp.where((col < WINDOW) & (n == 0), NEG, 0.0)
    n_slot = G_C // 2
    units, slices = [], []
    for kv in range(H_KV_C):
        k_ab = _kv_variants(keys, kv)
        v_ab = _kv_variants(vals, kv)
        for ps in range(n_slot):
            h0 = kv * G_C + 2 * ps
            sl = slice(kv * G_C * DH_C + ps * LANES, kv * G_C * DH_C + (ps + 1) * LANES)
            q = (q_ref[0, :, sl] * (DH_C ** -0.5)).astype(BF16)
            bias_ab = (bias_ref[h0] + first_blk, bias_ref[h0 + 1] + first_blk)
            units.append((q, k_ab, v_ab, bias_ab, (sink_ref[h0], sink_ref[h0 + 1])))
            slices.append(sl)
    for sl, o in zip(slices, _sink_attend(units)):
        o_ref[0, :, sl] = o


def swa_prompt(q, k, v, rel_bias, sinks):
    b, t, _ = q.shape
    nb = t // WINDOW
    i = np.arange(WINDOW)
    j = np.arange(2 * WINDOW)
    dist = WINDOW + i[:, None] - j[None, :]
    bias = bias_tiles(rel_bias, dist, (dist >= 0) & (dist <= WINDOW), H_C)
    cur = lambda bi, n: (bi, n, 0)
    prev = lambda bi, n: (bi, jnp.maximum(n - 1, 0), 0)
    return pl.pallas_call(
        _swa_prompt_kernel,
        grid=(b, nb),
        in_specs=[
            pl.BlockSpec((1, WINDOW, Q_C), cur),
            pl.BlockSpec((1, WINDOW, KV_C), prev), pl.BlockSpec((1, WINDOW, KV_C), cur),
            pl.BlockSpec((1, WINDOW, KV_C), prev), pl.BlockSpec((1, WINDOW, KV_C), cur),
            pl.BlockSpec((H_C, WINDOW, 2 * WINDOW), lambda bi, n: (0, 0, 0)),
            pl.BlockSpec(memory_space=pltpu.SMEM),
        ],
        out_specs=pl.BlockSpec((1, WINDOW, Q_C), cur),
        out_shape=jax.ShapeDtypeStruct((b, t, Q_C), F32),
        compiler_params=_params("parallel", "parallel"), name="swa_prompt",
    )(q, k, k, v, v, bias, sinks.astype(F32))


def _swa_sample_kernel(gs, s_len, q_ref, kn_ref, vn_ref, ck_ref, cv_ref, bias_ref, sink_ref, o_ref):
    n_slot = G_C // 2
    pad = jnp.zeros((WINDOW - s_len, LANES), F32)
    units, where = [], []
    for si in range(gs):
        keys = jnp.concatenate([ck_ref[si], kn_ref[si], pad], axis=0)
        vals = jnp.concatenate([cv_ref[si], vn_ref[si], pad], axis=0)
        for kv in range(H_KV_C):
            k_ab = _kv_variants(keys, kv)
            v_ab = _kv_variants(vals, kv)
            base = kv * G_C * DH_C
            q = jnp.concatenate([q_ref[si, :, base + ps * LANES:base + (ps + 1) * LANES] for ps in range(n_slot)],
                                axis=0)
            q = (q * (DH_C ** -0.5)).astype(BF16)
            units.append((q, k_ab, v_ab, (bias_ref[kv, 0], bias_ref[kv, 1]), (sink_ref[kv, 0], sink_ref[kv, 1])))
            where.append((si, base))
    for (si, base), o in zip(where, _sink_attend(units)):
        for ps in range(n_slot):
            o_ref[si, :, base + ps * LANES:base + (ps + 1) * LANES] = o[ps * s_len:(ps + 1) * s_len]


def swa_sample(q, k, v, cache_k, cache_v, rel_bias, sinks, gs):
    s_n, s_len, _ = q.shape
    assert s_n % gs == 0 and cache_k.shape[1] == WINDOW
    ck = cache_k.reshape(s_n, WINDOW, KV_C)
    cv = cache_v.reshape(s_n, WINDOW, KV_C)
    tq = np.arange(s_len)
    j = np.arange(2 * WINDOW)
    dist = WINDOW + tq[:, None] - j[None, :]
    valid = (dist >= 0) & (dist <= WINDOW) & (j[None, :] < WINDOW + s_len)
    n_slot = G_C // 2
    bias = bias_tiles(rel_bias, dist, valid, H_C)
    bias = bias.reshape(H_KV_C, n_slot, 2, s_len, 2 * WINDOW).transpose(0, 2, 1, 3, 4)
    bias = bias.reshape(H_KV_C, 2, n_slot * s_len, 2 * WINDOW)
    sk = sinks.astype(F32).reshape(H_KV_C, n_slot, 2).transpose(0, 2, 1)
    sk = jnp.repeat(sk, s_len, axis=-1).reshape(H_KV_C, 2, n_slot * s_len, 1)
    grp = lambda i: (i, 0, 0)
    whole = lambda shape: pl.BlockSpec(shape, lambda i: (0,) * len(shape))
    return pl.pallas_call(
        functools.partial(_swa_sample_kernel, gs, s_len),
        grid=(s_n // gs,),
        in_specs=[
            pl.BlockSpec((gs, s_len, Q_C), grp),
            pl.BlockSpec((gs, s_len, KV_C), grp), pl.BlockSpec((gs, s_len, KV_C), grp),
            pl.BlockSpec((gs, WINDOW, KV_C), grp), pl.BlockSpec((gs, WINDOW, KV_C), grp),
            whole(bias.shape), whole(sk.shape),
        ],
        out_specs=pl.BlockSpec((gs, s_len, Q_C), grp),
        out_shape=jax.ShapeDtypeStruct((s_n, s_len, Q_C), F32),
        compiler_params=_params("parallel"), name="swa_sample",
    )(q, k, v, ck, cv, bias, sk)


ROUTE_BIG = 1 << 20


def _out_route_kernel(n_in, *refs):
    h_ref = refs[0]
    o_refs = refs[1:1 + n_in]
    w_refs = refs[1 + n_in:1 + 2 * n_in]
    nw_ref, wr_ref, br_ref, cin_ref, hn_ref, xn_ref, rt_ref, cout_ref, cnt_scr, tri_scr = refs[1 + 2 * n_in:]
    step = pl.program_id(0)

    @pl.when(step == 0)
    def _():
        cnt_scr[...] = cin_ref[...]
        ri = lax.broadcasted_iota(jnp.int32, tri_scr.shape, 0)
        ci = lax.broadcasted_iota(jnp.int32, tri_scr.shape, 1)
        tri_scr[...] = (ri > ci).astype(BF16)

    h = h_ref[...]
    for o_ref, w_ref in zip(o_refs, w_refs):
        h = h + jnp.dot(o_ref[...].astype(BF16), w_ref[...], preferred_element_type=F32)
    hn_ref[...] = h
    xn = h * lax.rsqrt(jnp.mean(h * h, axis=-1, keepdims=True) + EPS) * nw_ref[...]
    xn_ref[...] = xn
    x_hi = xn.astype(BF16)
    x_lo = (xn - x_hi.astype(F32)).astype(BF16)
    logit = (jnp.dot(x_hi, wr_ref[0], preferred_element_type=F32) + jnp.dot(x_lo, wr_ref[0], preferred_element_type=F32)
             + jnp.dot(x_hi, wr_ref[1], preferred_element_type=F32) + br_ref[...])
    lane = lax.broadcasted_iota(jnp.int32, logit.shape, 1)
    gmask = lane < N_GROUPS
    gl = jnp.where(gmask, logit, NEG)
    gmax = jnp.max(gl, axis=-1, keepdims=True)
    gsel = jnp.min(jnp.where(gl == gmax, lane, ROUTE_BIG), axis=-1, keepdims=True)
    gw = 1.0 / jnp.sum(jnp.where(gmask, jnp.exp(gl - gmax), 0.0), axis=-1, keepdims=True)
    eid = lane - N_GROUPS
    emask = (eid >= 0) & (eid < N_EXPERTS) & (jnp.right_shift(eid, 3) == gsel)
    el = jnp.where(emask, logit, NEG)
    v1 = jnp.max(el, axis=-1, keepdims=True)
    i1 = jnp.min(jnp.where((el == v1) & emask, lane, ROUTE_BIG), axis=-1, keepdims=True)
    emask2 = emask & (lane != i1)
    el2 = jnp.where(emask2, logit, NEG)
    v2 = jnp.max(el2, axis=-1, keepdims=True)
    i2 = jnp.min(jnp.where((el2 == v2) & emask2, lane, ROUTE_BIG), axis=-1, keepdims=True)
    e = jnp.exp(v2 - v1)
    w1 = gw / (1.0 + e)
    w2 = gw * e / (1.0 + e)
    onehot = ((lane == i1) | (lane == i2)).astype(BF16)
    before = jnp.dot(tri_scr[...], onehot, preferred_element_type=F32) + cnt_scr[...]
    r1 = jnp.sum(jnp.where(lane == i1, before, 0.0), axis=-1, keepdims=True)
    r2 = jnp.sum(jnp.where(lane == i2, before, 0.0), axis=-1, keepdims=True)
    cnt_scr[...] = cnt_scr[...] + jnp.sum(onehot.astype(F32), axis=0, keepdims=True)
    rt = jnp.where(lane == 0, (i1 - N_GROUPS).astype(F32),
                   jnp.where(lane == 1, (i2 - N_GROUPS).astype(F32),
                             jnp.where(lane == 2, w1, jnp.where(lane == 3, w2,
                                                                jnp.where(lane == 4, r1,
                                                                          jnp.where(lane == 5, r2, 0.0))))))
    rt_ref[...] = rt

    @pl.when(step == pl.num_programs(0) - 1)
    def _():
        cout_ref[...] = cnt_scr[...]


def out_route(h, outs, weights_bf16, norm_w, w_group, b_group, w_router, b_router, counts_in, tm=512):
    n, d = h.shape
    tm = min(tm, n)
    assert n % tm == 0 and EXP_PER_GROUP == 8
    wr = jnp.zeros((d, LANES), F32).at[:, :N_GROUPS].set(w_group).at[:, N_GROUPS:N_GROUPS + N_EXPERTS].set(w_router)
    br = jnp.zeros((1, LANES), F32).at[0, :N_GROUPS].set(b_group).at[0, N_GROUPS:N_GROUPS + N_EXPERTS].set(b_router)
    wr_hi = wr.astype(BF16)
    wr = jnp.stack([wr_hi, (wr - wr_hi.astype(F32)).astype(BF16)])
    row = lambda i: (i, 0)
    fixed = lambda i: (0, 0)
    in_specs = [pl.BlockSpec((tm, d), row)]
    in_specs += [pl.BlockSpec((tm, o.shape[1]), row) for o in outs]
    in_specs += [pl.BlockSpec(w.shape, fixed) for w in weights_bf16]
    in_specs += [pl.BlockSpec((1, d), fixed), pl.BlockSpec((2, d, LANES), lambda i: (0, 0, 0)),
                 pl.BlockSpec((1, LANES), fixed), pl.BlockSpec((1, LANES), fixed)]
    return pl.pallas_call(
        functools.partial(_out_route_kernel, len(outs)),
        grid=(n // tm,), in_specs=in_specs,
        out_specs=[pl.BlockSpec((tm, d), row), pl.BlockSpec((tm, d), row), pl.BlockSpec((tm, LANES), row),
                   pl.BlockSpec((1, LANES), fixed)],
        out_shape=[jax.ShapeDtypeStruct((n, d), F32), jax.ShapeDtypeStruct((n, d), F32),
                   jax.ShapeDtypeStruct((n, LANES), F32), jax.ShapeDtypeStruct((1, LANES), F32)],
        scratch_shapes=[pltpu.VMEM((1, LANES), F32), pltpu.VMEM((tm, tm), BF16)],
        compiler_params=_params("arbitrary"), name="out_route",
    )(h, *outs, *weights_bf16, norm_w.reshape(1, d), wr, br, counts_in)


def _experts_kernel(te_ref, nv_ref, x_ref, wg_ref, wu_ref, wd_ref, y_ref):
    i = pl.program_id(0)

    @pl.when(i < nv_ref[0])
    def _():
        x = x_ref[...].astype(BF16)
        g = jnp.dot(x, wg_ref[0, 0].astype(BF16), preferred_element_type=F32)
        u = jnp.dot(x, wu_ref[0, 0].astype(BF16), preferred_element_type=F32)
        hh = (_silu(g) * u).astype(BF16)
        y_ref[...] = jnp.dot(hh, wd_ref[0, 0].astype(BF16), preferred_element_type=F32)

    @pl.when(i >= nv_ref[0])
    def _():
        y_ref[...] = jnp.zeros(y_ref.shape, F32)


def moe_experts(xs, tile_expert, n_valid, w_gate, w_up, w_down, layer, tm):
    p, d = xs.shape
    f = w_gate.shape[-1]
    n_tiles = p // tm
    grid_spec = pltpu.PrefetchScalarGridSpec(
        num_scalar_prefetch=2,
        grid=(n_tiles,),
        in_specs=[
            pl.BlockSpec((tm, d), lambda i, te, nv: (i, 0)),
            pl.BlockSpec((1, 1, d, f), lambda i, te, nv: (layer, te[i], 0, 0)),
            pl.BlockSpec((1, 1, d, f), lambda i, te, nv: (layer, te[i], 0, 0)),
            pl.BlockSpec((1, 1, f, d), lambda i, te, nv: (layer, te[i], 0, 0)),
        ],
        out_specs=pl.BlockSpec((tm, d), lambda i, te, nv: (i, 0)),
    )
    return pl.pallas_call(
        _experts_kernel, grid_spec=grid_spec,
        out_shape=jax.ShapeDtypeStruct((p, d), F32),
        compiler_params=_params("arbitrary"), name="moe_experts",
    )(tile_expert, n_valid, xs, w_gate, w_up, w_down)


def _take_rows(x, idx):
    return x.at[idx].get(mode="promise_in_bounds", unique_indices=True)


def hier_moe(xn, route, counts, w_gate, w_up, w_down, layer, tm=512):
    n, d = xn.shape
    e_idx = route[:, :TOP_K].astype(jnp.int32)
    rank = route[:, 2 * TOP_K:3 * TOP_K].astype(jnp.int32)
    cnt = counts[0, N_GROUPS:N_GROUPS + N_EXPERTS].astype(jnp.int32)
    padded = ((cnt + tm - 1) // tm) * tm
    pad_ends = jnp.cumsum(padded)
    pad_starts = pad_ends - padded
    experts = jnp.arange(N_EXPERTS, dtype=jnp.int32)
    pos = jnp.sum(jnp.where(e_idx[..., None] == experts, pad_starts, 0), axis=-1) + rank
    n_tiles = -(-(n * TOP_K + N_EXPERTS * (tm - 1)) // tm)
    p = n_tiles * tm
    tok = jnp.broadcast_to(jnp.arange(n, dtype=jnp.int32)[:, None], (n, TOP_K))
    n_valid = (pad_ends[-1] // tm).astype(jnp.int32)
    tile_start = jnp.minimum(jnp.arange(n_tiles, dtype=jnp.int32), n_valid - 1) * tm
    tile_e = jnp.sum((tile_start[:, None] >= pad_ends[None, :]).astype(jnp.int32), axis=-1)
    tile_e = jnp.minimum(tile_e, N_EXPERTS - 1)
    _, tok_sorted = lax.sort_key_val(pos.reshape(-1), tok.reshape(-1))
    starts = jnp.cumsum(cnt) - cnt
    rows = jnp.arange(p, dtype=jnp.int32).reshape(n_tiles, tm)
    off = rows - pad_starts[tile_e][:, None]
    src = jnp.where(off < cnt[tile_e][:, None], starts[tile_e][:, None] + off, rows % (n * TOP_K))
    row_tok = tok_sorted.at[src.reshape(-1)].get(mode="promise_in_bounds")
    xs = xn.at[row_tok].get(mode="promise_in_bounds")
    ys = moe_experts(xs, tile_e, n_valid.reshape(1), w_gate, w_up, w_down, layer, tm)
    return tuple(_take_rows(ys, pos[:, kk]) for kk in range(TOP_K))


def _final_norm_kernel(h_ref, rt_ref, a_ref, b_ref, nw_ref, o_ref):
    h = _add_expert_outputs(h_ref[...], rt_ref, (a_ref, b_ref))
    o_ref[...] = h * lax.rsqrt(jnp.mean(h * h, axis=-1, keepdims=True) + EPS) * nw_ref[...]


def final_norm(h, route, a, b, norm_w, add_row0=0, tm=256):
    n, d = h.shape
    tm = min(tm, n)
    assert add_row0 % tm == 0
    row = pl.BlockSpec((tm, d), lambda i: (i, 0))
    add_row = pl.BlockSpec((tm, d), lambda i: (i + add_row0 // tm, 0))
    return pl.pallas_call(
        _final_norm_kernel, grid=(n // tm,),
        in_specs=[row, pl.BlockSpec((tm, LANES), lambda i: (i + add_row0 // tm, 0)), add_row, add_row,
                  pl.BlockSpec((1, d), lambda i: (0, 0))],
        out_specs=row, out_shape=jax.ShapeDtypeStruct((n, d), F32),
        compiler_params=_params("parallel"), name="final_norm",
    )(h, route, a, b, norm_w.reshape(1, d))


def kernel(x_prompt, x_sample, state_a_conv, state_a_ssm, cache_b_k, cache_b_v, cache_c_k, cache_c_v, page_table, norm_mix, norm_ffn, norm_final, rel_bias, w_in0, conv_w, a_log, dt_bias, gdn_norm_w, diff_lambda, diff_subln_w, w_out0, w_in1, b_in1, sinks, w_out1, w_group, b_group, w_router, b_router, w_gate, w_up, w_down):
    bp, tp, d = x_prompt.shape
    sn, sl, _ = x_sample.shape
    n_p, n_s = bp * tp, sn * sl
    parts = ((x_prompt.reshape(n_p, d), bp, tp), (x_sample.reshape(n_s, d), sn, sl))

    off_z = CONV_CH
    off_a = off_z + H_A * DV_A
    off_qb = off_a + 2 * H_A
    w0 = jnp.concatenate([w_in0[:, :off_a], w_in0[:, off_a:off_qb],
                          jnp.zeros((d, LANES - 2 * H_A), w_in0.dtype), w_in0[:, off_qb:]], axis=1).astype(BF16)
    widths0 = (CONV_CH, H_A * DV_A, LANES, B_W, B_W, B_W)
    w_out0_bf = w_out0.astype(BF16)
    w_out0_parts = [w_out0_bf[:H_A * DV_A], w_out0_bf[H_A * DV_A:]]

    row0 = (0, n_p)

    def moe_layer(layer, xns, routes, counts):
        xn_all = jnp.concatenate(xns, axis=0)
        route_all = jnp.concatenate(routes, axis=0)
        return (route_all,) + hier_moe(xn_all, route_all, counts, w_gate, w_up, w_down, layer)

    conv_states = (jnp.zeros((bp, CONV_W - 1, CONV_CH), F32), state_a_conv)
    ssm_states = (jnp.zeros((bp, H_A, DK_A, DV_A), F32), state_a_ssm)
    hs, xns, routes, conv_new, ssm_new, kb_new, vb_new = [], [], [], [], [], [], []
    counts = jnp.zeros((1, LANES), F32)
    for idx, (x2, b, t) in enumerate(parts):
        qkv, z, ab, qb, kb, vb, kb4, vb4 = norm_proj(x2, [], norm_mix[0], w0, None, widths0, emit_h=False,
                                                     head_split=(4, 5))
        r3 = lambda a: a.reshape(b, t, a.shape[-1])
        o_a, s_new = gdn_heads(r3(qkv), r3(z), r3(ab), conv_states[idx], ssm_states[idx], conv_w, a_log, dt_bias,
                               gdn_norm_w, min(GDN_CHUNK, t), 2 if idx == 0 else 8)
        if idx == 0:
            o_b = diff_attn_prompt(r3(qb), r3(kb), r3(vb), rel_bias, diff_lambda, diff_subln_w, 512)
        else:
            o_b = diff_attn_sample(r3(qb), r3(kb), r3(vb), cache_b_k, cache_b_v, page_table, rel_bias, diff_lambda,
                                   diff_subln_w, 16)
        h1, xn, route, counts = out_route(x2, [o_a.reshape(b * t, -1), o_b.reshape(b * t, -1)], w_out0_parts,
                                          norm_ffn[0], w_group[0], b_group[0], w_router[0], b_router[0], counts)
        hs.append(h1), xns.append(xn), routes.append(route)
        conv_new.append(r3(qkv)[:, t - (CONV_W - 1):, :])
        ssm_new.append(s_new)
        kb_new.append(kb4.reshape(b, t, H_B, 2 * DH_B))
        vb_new.append(vb4.reshape(b, t, H_B, 2 * DH_B))
    moe_out = moe_layer(0, xns, routes, counts)

    w1 = w_in1.astype(BF16)
    w_out1_bf = w_out1.astype(BF16)
    hs2, xns, routes, kc_new, vc_new = [], [], [], [], []
    counts = jnp.zeros((1, LANES), F32)
    for idx, (_, b, t) in enumerate(parts):
        h2, q, k, v = norm_proj(hs[idx], moe_out, norm_mix[1], w1, b_in1, (Q_C, KV_C, KV_C),
                                emit_h=True, add_row0=row0[idx])
        r3 = lambda a: a.reshape(b, t, a.shape[-1])
        if idx == 0:
            o_c = swa_prompt(r3(q), r3(k), r3(v), rel_bias, sinks)
            kc_new.append(r3(k)[:, t - WINDOW:].reshape(b, WINDOW, H_KV_C, DH_C))
            vc_new.append(r3(v)[:, t - WINDOW:].reshape(b, WINDOW, H_KV_C, DH_C))
        else:
            o_c = swa_sample(r3(q), r3(k), r3(v), cache_c_k, cache_c_v, rel_bias, sinks, 8)
            kc_new.append(jnp.concatenate([cache_c_k[:, t:], k.reshape(b, t, H_KV_C, DH_C)], axis=1))
            vc_new.append(jnp.concatenate([cache_c_v[:, t:], v.reshape(b, t, H_KV_C, DH_C)], axis=1))
        h3, xn, route, counts = out_route(h2, [o_c.reshape(b * t, -1)], [w_out1_bf], norm_ffn[1],
                                          w_group[1], b_group[1], w_router[1], b_router[1], counts)
        hs2.append(h3), xns.append(xn), routes.append(route)
    moe_out = moe_layer(1, xns, routes, counts)

    y_out = [final_norm(hs2[idx], *moe_out, norm_final, add_row0=row0[idx]).reshape(b, t, d)
             for idx, (_, b, t) in enumerate(parts)]
    return (y_out[0], y_out[1], conv_new[0], conv_new[1], ssm_new[0], ssm_new[1],
            kb_new[0], vb_new[0], kb_new[1], vb_new[1], kc_new[0], vc_new[0], kc_new[1], vc_new[1])
```

```python
import functools
import math

import jax
import jax.numpy as jnp
import numpy as np
from jax import lax
from jax.experimental import pallas as pl
from jax.experimental.pallas import tpu as pltpu

F32 = jnp.float32
BF16 = jnp.bfloat16
EPS = 1e-6
NEG = -1e30
LOG2E = math.log2(math.e)
ONES_ROWS = 16
LANES = 128
VMEM_LIMIT = 56 * 1024 * 1024

D_MODEL = 1024
H_A, DK_A, DV_A, CONV_W = 4, 128, 128, 4
CONV_CH = 2 * H_A * DK_A + H_A * DV_A
GDN_CHUNK = 64
GDN_PRECISE_INV = False
H_B, DH_B = 4, 64
B_W = H_B * 2 * DH_B
PAGE = 128
H_C, H_KV_C, DH_C, WINDOW = 16, 2, 64, 128
G_C = H_C // H_KV_C
Q_C = H_C * DH_C
KV_C = H_KV_C * DH_C
N_BUCKETS, MAX_DIST = 32, 128
N_GROUPS, EXP_PER_GROUP, TOP_K, D_EXPERT = 4, 8, 2, 256
N_EXPERTS = N_GROUPS * EXP_PER_GROUP
LAM_INIT0 = 0.8 - 0.6 * math.exp(-0.3 * 0)


def _params(*sem):
    return pltpu.CompilerParams(dimension_semantics=sem, vmem_limit_bytes=VMEM_LIMIT)


def _mm(a, b, precise=False):
    if precise:
        return jnp.dot(a.astype(F32), b.astype(F32), preferred_element_type=F32, precision=lax.Precision.HIGHEST)
    return jnp.dot(a.astype(BF16), b.astype(BF16), preferred_element_type=F32)


def _mm_nt(a, b, precise=False):
    dn = (((1,), (1,)), ((), ()))
    if precise:
        return lax.dot_general(a.astype(F32), b.astype(F32), dn, preferred_element_type=F32,
                               precision=lax.Precision.HIGHEST)
    return lax.dot_general(a.astype(BF16), b.astype(BF16), dn, preferred_element_type=F32)


def _mm_tn(a, b, precise=False):
    dn = (((0,), (0,)), ((), ()))
    if precise:
        return lax.dot_general(a.astype(F32), b.astype(F32), dn, preferred_element_type=F32,
                               precision=lax.Precision.HIGHEST)
    return lax.dot_general(a.astype(BF16), b.astype(BF16), dn, preferred_element_type=F32)


def _sigmoid(x):
    return 1.0 / (1.0 + jnp.exp(-x))


def _silu(x):
    return x * _sigmoid(x)


def _softplus(x):
    return jnp.maximum(x, 0.0) + jnp.log(1.0 + jnp.exp(-jnp.abs(x)))


def _add_expert_outputs(h, route_ref, y_refs):
    rt = route_ref[...]
    for kk, y_ref in enumerate(y_refs):
        h = h + rt[:, TOP_K + kk:TOP_K + kk + 1] * y_ref[...]
    return h


def _norm_proj_kernel(n_add, widths, has_bias, emit_h, head_split, *refs):
    x_ref = refs[0]
    add_refs = refs[1:1 + n_add]
    nw_ref, w_ref = refs[1 + n_add], refs[2 + n_add]
    pos = 3 + n_add
    b_ref = None
    if has_bias:
        b_ref = refs[pos]
        pos += 1
    outs = refs[pos:]
    h = x_ref[...]
    if n_add:
        h = _add_expert_outputs(h, add_refs[0], add_refs[1:])
    if emit_h:
        outs[0][...] = h
        outs = outs[1:]
    split_refs = outs[len(widths):]
    xn = h * lax.rsqrt(jnp.mean(h * h, axis=-1, keepdims=True) + EPS) * nw_ref[...]
    xb = xn.astype(BF16)
    off = 0
    for j, (o_ref, wd) in enumerate(zip(outs, widths)):
        y = jnp.dot(xb, w_ref[:, off:off + wd], preferred_element_type=F32)
        if has_bias:
            y = y + b_ref[:, off:off + wd]
        o_ref[...] = y
        if j in head_split:
            s_ref = split_refs[head_split.index(j)]
            for hh in range(wd // LANES):
                s_ref[:, hh, :] = y[:, hh * LANES:(hh + 1) * LANES]
        off += wd


def norm_proj(x, addends, norm_w, w_bf16, bias, widths, emit_h, head_split=(), add_row0=0, tm=512):
    n, d = x.shape
    m = w_bf16.shape[1]
    tm = min(tm, n)
    assert sum(widths) == m and n % tm == 0
    row = lambda i: (i, 0)
    fixed = lambda i: (0, 0)
    assert add_row0 % tm == 0
    add_row = lambda i: (i + add_row0 // tm, 0)
    in_specs = [pl.BlockSpec((tm, d), row)] + [pl.BlockSpec((tm, a.shape[1]), add_row) for a in addends]
    in_specs += [pl.BlockSpec((1, d), fixed), pl.BlockSpec((d, m), fixed)]
    args = [x, *addends, norm_w.reshape(1, d), w_bf16]
    if bias is not None:
        in_specs.append(pl.BlockSpec((1, m), fixed))
        args.append(bias.reshape(1, m))
    out_shape, out_specs = [], []
    if emit_h:
        out_shape.append(jax.ShapeDtypeStruct((n, d), F32))
        out_specs.append(pl.BlockSpec((tm, d), row))
    for wd in widths:
        out_shape.append(jax.ShapeDtypeStruct((n, wd), F32))
        out_specs.append(pl.BlockSpec((tm, wd), row))
    for j in head_split:
        out_shape.append(jax.ShapeDtypeStruct((n, widths[j] // LANES, LANES), F32))
        out_specs.append(pl.BlockSpec((tm, widths[j] // LANES, LANES), lambda i: (i, 0, 0)))
    return pl.pallas_call(
        functools.partial(_norm_proj_kernel, len(addends), tuple(widths), bias is not None, emit_h,
                          tuple(head_split)),
        grid=(n // tm,), in_specs=in_specs, out_specs=out_specs, out_shape=out_shape,
        compiler_params=_params("parallel"), name="norm_proj",
    )(*args)


def _gdn_kernel(c_len, nb, qkv_ref, z_ref, ab_ref, cbuf_ref, s0_ref, convw_ref, gpar_ref, nw_ref,
                o_ref, sfin_ref, xp_scr, s_scr):
    c = pl.program_id(1)
    n_c = pl.num_programs(1)
    hist = CONV_W - 1
    base = 8 - hist

    @pl.when(c == 0)
    def _():
        xp_scr[:, base:8, :] = cbuf_ref[...]
        s_scr[...] = s0_ref[...]

    ri = lax.broadcasted_iota(jnp.int32, (c_len, c_len), 0)
    ci = lax.broadcasted_iota(jnp.int32, (c_len, c_len), 1)
    lower = ri >= ci
    strict = ri > ci
    eye = (ri == ci).astype(F32)
    lower_f = lower.astype(F32)

    units = []
    for bi in range(nb):
        xp_scr[bi, 8:8 + c_len, :] = qkv_ref[bi]
        y = xp_scr[bi, base:base + c_len, :] * convw_ref[0:1, :]
        for j in range(1, CONV_W):
            y = y + xp_scr[bi, base + j:base + j + c_len, :] * convw_ref[j:j + 1, :]
        xp_scr[bi, base:8, :] = xp_scr[bi, base + c_len:8 + c_len, :]
        y = _silu(y)

        ab = ab_ref[bi]
        g_t = gpar_ref[0:1, :] * _softplus(ab + gpar_ref[1:2, :])
        beta_t = _sigmoid(ab)
        gcum_t = _mm(lower_f, g_t, precise=True)
        gcum_tt = gcum_t.T

        for h in range(H_A):
            q = y[:, h * DK_A:(h + 1) * DK_A]
            k = y[:, (H_A + h) * DK_A:(H_A + h + 1) * DK_A]
            v = y[:, 2 * H_A * DK_A + h * DV_A:2 * H_A * DK_A + (h + 1) * DV_A]
            q = q * lax.rsqrt(jnp.sum(q * q, axis=-1, keepdims=True) + EPS) * (DK_A ** -0.5)
            k = k * lax.rsqrt(jnp.sum(k * k, axis=-1, keepdims=True) + EPS)
            gc = gcum_t[:, h:h + 1]
            gr = gcum_tt[h:h + 1, :]
            beta = beta_t[:, H_A + h:H_A + h + 1]
            decay = jnp.where(lower, jnp.exp(jnp.where(lower, gc - gr, 0.0)), 0.0)
            kb = k * beta
            units.append(dict(bi=bi, h=h, q=q, k=k, k_bf=k.astype(BF16), gc=gc, decay=decay, kb=kb,
                              rhs=jnp.concatenate([v * beta, kb * jnp.exp(gc)], axis=1)))

    for un in units:
        un["nmat"] = jnp.where(strict, _mm_nt(un["kb"], un["k_bf"]) * un["decay"], 0.0)
        un["qk"] = jnp.where(lower, _mm_nt(un["q"], un["k_bf"]) * un["decay"], 0.0)
    for un in units:
        un["inv"] = eye - un["nmat"]
        un["pw"] = _mm(un["nmat"], un["nmat"], GDN_PRECISE_INV)
    span = 2
    while span < c_len:
        for un in units:
            un["inv"] = un["inv"] + _mm(un["inv"], un["pw"], GDN_PRECISE_INV)
        span *= 2
        if span < c_len:
            for un in units:
                un["pw"] = _mm(un["pw"], un["pw"], GDN_PRECISE_INV)
    for un in units:
        sol = _mm(un["inv"], un["rhs"], GDN_PRECISE_INV)
        un["u"], un["w"] = sol[:, :DV_A], sol[:, DV_A:]
        un["s"] = s_scr[un["bi"], un["h"]]
        un["s_bf"] = un["s"].astype(BF16)
    for un in units:
        un["v_new"] = un["u"] - _mm(un["w"], un["s_bf"])
        un["o_s"] = _mm(un["q"] * jnp.exp(un["gc"]), un["s_bf"])
    for un in units:
        bi, h, gc = un["bi"], un["h"], un["gc"]
        o = un["o_s"] + _mm(un["qk"], un["v_new"])
        g_last = gc[c_len - 1:c_len, :]
        s_scr[bi, h] = un["s"] * jnp.exp(g_last) + _mm_tn(un["k"] * jnp.exp(g_last - gc), un["v_new"])
        o = o * lax.rsqrt(jnp.mean(o * o, axis=-1, keepdims=True) + EPS) * nw_ref[...]
        o_ref[bi, :, h * DV_A:(h + 1) * DV_A] = o * _silu(z_ref[bi, :, h * DV_A:(h + 1) * DV_A])

    @pl.when(c == n_c - 1)
    def _():
        sfin_ref[...] = s_scr[...]


def gdn_heads(qkv, z, ab, conv_buf, s0, conv_w, a_log, dt_bias, norm_w, c_len, nb):
    b, t, _ = qkv.shape
    assert t % c_len == 0 and b % nb == 0
    convw_t = conv_w.T
    gpar = jnp.zeros((2, LANES), F32)
    gpar = gpar.at[0, :H_A].set(-jnp.exp(a_log.astype(F32))).at[1, :H_A].set(dt_bias.astype(F32))
    tok = lambda i, j: (i, j, 0)
    seq3 = lambda i, j: (i, 0, 0)
    fixed = lambda i, j: (0, 0)
    return pl.pallas_call(
        functools.partial(_gdn_kernel, c_len, nb),
        grid=(b // nb, t // c_len),
        in_specs=[
            pl.BlockSpec((nb, c_len, CONV_CH), tok),
            pl.BlockSpec((nb, c_len, H_A * DV_A), tok),
            pl.BlockSpec((nb, c_len, LANES), tok),
            pl.BlockSpec((nb, CONV_W - 1, CONV_CH), seq3),
            pl.BlockSpec((nb, H_A, DK_A, DV_A), lambda i, j: (i, 0, 0, 0)),
            pl.BlockSpec((CONV_W, CONV_CH), fixed),
            pl.BlockSpec((2, LANES), fixed),
            pl.BlockSpec((1, DV_A), fixed),
        ],
        out_specs=[
            pl.BlockSpec((nb, c_len, H_A * DV_A), tok),
            pl.BlockSpec((nb, H_A, DK_A, DV_A), lambda i, j: (i, 0, 0, 0)),
        ],
        out_shape=[
            jax.ShapeDtypeStruct((b, t, H_A * DV_A), F32),
            jax.ShapeDtypeStruct((b, H_A, DK_A, DV_A), F32),
        ],
        scratch_shapes=[
            pltpu.VMEM((nb, 8 + c_len, CONV_CH), F32),
            pltpu.VMEM((nb, H_A, DK_A, DV_A), F32),
        ],
        compiler_params=_params("parallel", "arbitrary"), name="gdn_heads",
    )(qkv, z, ab, conv_buf, s0, convw_t, gpar, norm_w.reshape(1, DV_A))


def _t5_bucket_np(dist):
    d = np.maximum(dist, 0)
    max_exact = N_BUCKETS // 2
    ratio = (np.log(np.maximum(d, 1).astype(np.float32) / np.float32(max_exact))
             / np.float32(math.log(MAX_DIST / max_exact)))
    large = np.minimum(max_exact + (ratio * (N_BUCKETS - max_exact)).astype(np.int32), N_BUCKETS - 1)
    return np.where(d < max_exact, d, large).astype(np.int32)


FAR_BUCKET = int(_t5_bucket_np(np.array([MAX_DIST]))[0])
assert np.all(_t5_bucket_np(np.arange(MAX_DIST, 4 * MAX_DIST)) == FAR_BUCKET)


def _bias_kernel(col_group, bucket_ref, rb_ref, out_ref):
    for col, grp in enumerate(col_group):
        bk = bucket_ref[grp]
        acc = jnp.full(bk.shape, NEG, F32)
        for b in range(N_BUCKETS):
            acc = jnp.where(bk == b, rb_ref[b, col], acc)
        out_ref[col] = acc


def bias_tiles(rel_bias, dist, valid, n_cols, col_group=None):
    bucket = np.where(valid, _t5_bucket_np(dist), -1).astype(np.int32)
    if bucket.ndim == 2:
        bucket, col_group = bucket[None], (0,) * n_cols
    _, r, c = bucket.shape
    return pl.pallas_call(
        functools.partial(_bias_kernel, tuple(col_group)),
        in_specs=[pl.BlockSpec(memory_space=pltpu.VMEM), pl.BlockSpec(memory_space=pltpu.SMEM)],
        out_specs=pl.BlockSpec(memory_space=pltpu.VMEM),
        out_shape=jax.ShapeDtypeStruct((n_cols, r, c), F32),
        name="bias_tiles",
    )(jnp.asarray(bucket), rel_bias.astype(F32))


def _diff_lambda(lam_ref):
    lp = lam_ref[...]
    s1 = jnp.sum(lp[0:1] * lp[1:2], axis=-1, keepdims=True)
    s2 = jnp.sum(lp[2:3] * lp[3:4], axis=-1, keepdims=True)
    return jnp.exp(s1) - jnp.exp(s2) + LAM_INIT0


def _diff_finish(o1, o2, lam, nw):
    o = o1 - lam * o2
    return o * lax.rsqrt(jnp.mean(o * o, axis=-1, keepdims=True) + EPS) * nw * (1.0 - LAM_INIT0)


def _softmax_update(s, v_bf, m_ref, l_ref, acc_ref):
    m_prev = m_ref[...]
    m_new = jnp.maximum(m_prev, jnp.max(s, axis=-1, keepdims=True))
    alpha = jnp.exp(m_prev - m_new)
    p = jnp.exp(s - m_new)
    l_ref[...] = alpha * l_ref[...] + jnp.sum(p, axis=-1, keepdims=True)
    acc_ref[...] = alpha * acc_ref[...] + jnp.dot(p.astype(BF16), v_bf, preferred_element_type=F32)
    m_ref[...] = m_new


def _diff_prompt_kernel(blk, qt_ref, kt_ref, q_ref, k_ref, v_ref, t0_ref, t1_ref, far_ref, lam_ref, nw_ref, o_ref,
                        m_scr, acc_scr):
    qi = qt_ref[pl.program_id(1)]
    ki = kt_ref[pl.program_id(1)]
    nsub = blk // LANES
    hw = 2 * DH_B

    @pl.when(ki == 0)
    def _():
        m_scr[...] = jnp.full(m_scr.shape, NEG, F32)
        acc_scr[...] = jnp.zeros(acc_scr.shape, F32)

    def bias_block(case, hm):
        const = jnp.full((LANES, LANES), far_ref[hm], F32)
        neg = jnp.full((LANES, LANES), NEG, F32)
        rows = []
        for jj in range(nsub):
            tiles = []
            for ii in range(nsub):
                sub = ii - jj + (nsub if case == "prev" else 0)
                tiles.append(neg if sub < 0 else t0_ref[hm] if sub == 0 else t1_ref[hm] if sub == 1 else const)
            rows.append(jnp.concatenate(tiles, axis=1))
        return jnp.concatenate(rows, axis=0)

    def step(case):
        q = (q_ref[0] * (DH_B ** -0.5 * LOG2E)).astype(BF16)
        k = k_ref[0].astype(BF16)
        v_t = v_ref[0].T
        first = lax.broadcasted_iota(jnp.int32, (1, hw), 1) < DH_B
        ones = jnp.ones((ONES_ROWS, blk), F32)
        s_all, v_ext = [], []
        for h in range(H_B):
            sl = slice(h * hw, (h + 1) * hw)
            qh, kh = q[:, sl], k[:, sl]
            v_ext.append(jnp.concatenate([v_t[sl], ones], axis=0).astype(BF16))
            for mp in range(2):
                km = jnp.where(first if mp == 0 else jnp.logical_not(first), kh, jnp.zeros_like(kh))
                s_all.append(lax.dot_general(km, qh, (((1,), (1,)), ((), ())),
                                             preferred_element_type=F32))
        p_all, alpha_all = [], []
        for hm, s in enumerate(s_all):
            m_prev = m_scr[hm]
            if case == "far":
                shift = far_ref[hm]
                m_new = jnp.maximum(m_prev, jnp.max(s, axis=0, keepdims=True) + shift)
                p = jnp.exp2(s + (shift - m_new))
            else:
                s = s + bias_block(case, hm)
                m_new = jnp.maximum(m_prev, jnp.max(s, axis=0, keepdims=True))
                p = jnp.exp2(s - m_new)
            alpha_all.append(jnp.exp2(m_prev - m_new))
            p_all.append(p.astype(BF16))
            m_scr[hm] = m_new
        for hm, (p, alpha) in enumerate(zip(p_all, alpha_all)):
            acc_scr[hm] = alpha * acc_scr[hm] + jnp.dot(v_ext[hm // 2], p, preferred_element_type=F32)

    @pl.when(ki < qi - 1)
    def _():
        step("far")

    @pl.when(ki == qi - 1)
    def _():
        step("prev")

    @pl.when(ki == qi)
    def _():
        step("diag")
        lam = _diff_lambda(lam_ref)
        for h in range(H_B):
            a1, a2 = acc_scr[2 * h], acc_scr[2 * h + 1]
            o = a1[:hw] / a1[hw:hw + 1] - lam * (a2[:hw] / a2[hw:hw + 1])
            o = o * lax.rsqrt(jnp.mean(o * o, axis=0, keepdims=True) + EPS) * nw_ref[...] * (1.0 - LAM_INIT0)
            o_ref[0, :, h * hw:(h + 1) * hw] = o.T


def diff_attn_prompt(q, k, v, rel_bias, diff_lambda, subln_w, blk):
    b, t, _ = q.shape
    assert t % blk == 0 and blk % LANES == 0 and blk >= 2 * LANES
    i = np.arange(LANES)
    d0 = i[None, :] - i[:, None]
    t0 = bias_tiles(rel_bias, d0, d0 >= 0, 2 * H_B) * LOG2E
    t1 = bias_tiles(rel_bias, d0 + LANES, np.ones_like(d0, bool), 2 * H_B) * LOG2E
    far = rel_bias[FAR_BUCKET, :2 * H_B].astype(F32) * LOG2E
    nb = t // blk
    hw = 2 * DH_B
    pairs = [(qi, ki) for qi in range(nb) for ki in range(qi + 1)]
    q_tab = jnp.asarray(np.array([p[0] for p in pairs], np.int32))
    k_tab = jnp.asarray(np.array([p[1] for p in pairs], np.int32))
    whole = lambda shape: pl.BlockSpec(shape, lambda bi, pi, qt, kt: (0,) * len(shape))
    grid_spec = pltpu.PrefetchScalarGridSpec(
        num_scalar_prefetch=2,
        grid=(b, len(pairs)),
        in_specs=[
            pl.BlockSpec((1, blk, B_W), lambda bi, pi, qt, kt: (bi, qt[pi], 0)),
            pl.BlockSpec((1, blk, B_W), lambda bi, pi, qt, kt: (bi, kt[pi], 0)),
            pl.BlockSpec((1, blk, B_W), lambda bi, pi, qt, kt: (bi, kt[pi], 0)),
            whole((2 * H_B, LANES, LANES)),
            whole((2 * H_B, LANES, LANES)),
            pl.BlockSpec(memory_space=pltpu.SMEM),
            whole((4, DH_B)),
            whole((hw, 1)),
        ],
        out_specs=pl.BlockSpec((1, blk, B_W), lambda bi, pi, qt, kt: (bi, qt[pi], 0)),
        scratch_shapes=[
            pltpu.VMEM((2 * H_B, 1, blk), F32),
            pltpu.VMEM((2 * H_B, hw + ONES_ROWS, blk), F32),
        ],
    )
    return pl.pallas_call(
        functools.partial(_diff_prompt_kernel, blk),
        grid_spec=grid_spec,
        out_shape=jax.ShapeDtypeStruct((b, t, B_W), F32),
        compiler_params=_params("parallel", "arbitrary"), name="diff_attn_prompt",
    )(q_tab, k_tab, q, k, v, t0, t1, far, diff_lambda.astype(F32), subln_w.reshape(hw, 1).astype(F32))


def _diff_sample_kernel(n_pg, s_len, pt_ref, q_ref, kn_ref, vn_ref, *refs):
    k_pages = refs[:n_pg]
    v_pages = refs[n_pg:2 * n_pg]
    blast_ref, bnew_ref, far_ref, lam_ref, nw_ref, o_ref, m_scr, l_scr, acc_scr = refs[2 * n_pg:]
    j = pl.program_id(1)
    last = pl.num_programs(1) - 1
    hw = 2 * DH_B

    @pl.when(j == 0)
    def _():
        m_scr[...] = jnp.full(m_scr.shape, NEG, F32)
        l_scr[...] = jnp.zeros(l_scr.shape, F32)
        acc_scr[...] = jnp.zeros(acc_scr.shape, F32)

    q = q_ref[0] * (DH_B ** -0.5)
    first = lax.broadcasted_iota(jnp.int32, (1, hw), 1) < DH_B
    rh = 2 * s_len
    q_heads = []
    for h in range(H_B):
        qh = q[:, h * hw:(h + 1) * hw]
        q_heads.append(jnp.concatenate([jnp.where(first, qh, 0.0), jnp.where(first, 0.0, qh)], axis=0).astype(BF16))

    def head_rows(ref, pages, h):
        return jnp.concatenate([r[0, pl.ds(h, PAGE, stride=H_B), :] for r in pages], axis=0).astype(BF16)

    n_grp = 4 if n_pg % 4 == 0 else 1
    per = n_pg // n_grp
    wg = per * PAGE
    s_units = [[None] * n_grp for _ in range(H_B)]
    for g in range(n_grp):
        for h in range(H_B):
            rows = slice(h * rh, (h + 1) * rh)
            bias = jnp.where(j == last, blast_ref[rows, g * wg:(g + 1) * wg], far_ref[rows, :])
            kgh = head_rows(None, k_pages[g * per:(g + 1) * per], h)
            s_units[h][g] = lax.dot_general(q_heads[h], kgh, (((1,), (1,)), ((), ())),
                                            preferred_element_type=F32) + bias
    m_news, alphas = [], []
    for h in range(H_B):
        m_prev = m_scr[h * rh:(h + 1) * rh]
        m_new = m_prev
        for s in s_units[h]:
            m_new = jnp.maximum(m_new, jnp.max(s, axis=-1, keepdims=True))
        m_news.append(m_new)
        alphas.append(jnp.exp(m_prev - m_new))
    l_new = [alphas[h] * l_scr[h * rh:(h + 1) * rh] for h in range(H_B)]
    acc_new = [alphas[h] * acc_scr[h * rh:(h + 1) * rh] for h in range(H_B)]
    for g in range(n_grp):
        for h in range(H_B):
            p = jnp.exp(s_units[h][g] - m_news[h])
            vgh = head_rows(None, v_pages[g * per:(g + 1) * per], h)
            l_new[h] = l_new[h] + jnp.sum(p, axis=-1, keepdims=True)
            acc_new[h] = acc_new[h] + jnp.dot(p.astype(BF16), vgh, preferred_element_type=F32)
    for h in range(H_B):
        m_scr[h * rh:(h + 1) * rh] = m_news[h]
        l_scr[h * rh:(h + 1) * rh] = l_new[h]
        acc_scr[h * rh:(h + 1) * rh] = acc_new[h]

    @pl.when(j == last)
    def _():
        pad = jnp.zeros((PAGE - s_len, hw), F32)
        for h in range(H_B):
            rows = slice(h * rh, (h + 1) * rh)
            kn = jnp.concatenate([kn_ref[0, :, h * hw:(h + 1) * hw], pad], axis=0).astype(BF16)
            vn = jnp.concatenate([vn_ref[0, :, h * hw:(h + 1) * hw], pad], axis=0).astype(BF16)
            s2 = lax.dot_general(q_heads[h], kn, (((1,), (1,)), ((), ())), preferred_element_type=F32)
            _softmax_update(s2 + bnew_ref[rows], vn, m_scr.at[rows], l_scr.at[rows], acc_scr.at[rows])
        accn = acc_scr[...] / l_scr[...]
        lam = _diff_lambda(lam_ref)
        for h in range(H_B):
            o1 = accn[(2 * h) * s_len:(2 * h + 1) * s_len]
            o2 = accn[(2 * h + 1) * s_len:(2 * h + 2) * s_len]
            o_ref[0, :, h * hw:(h + 1) * hw] = _diff_finish(o1, o2, lam, nw_ref[...])


def diff_attn_sample(q, k, v, cache_k, cache_v, page_table, rel_bias, diff_lambda, subln_w, n_pg):
    s_n, s_len, _ = q.shape
    pages_per_seq = page_table.shape[1]
    assert pages_per_seq % n_pg == 0
    past = pages_per_seq * PAGE
    n_phys = cache_k.shape[0]
    hw = 2 * DH_B
    ck = cache_k.reshape(n_phys, PAGE * H_B, hw)
    cv = cache_v.reshape(n_phys, PAGE * H_B, hw)
    n_rows = 2 * H_B * s_len
    blk_keys = n_pg * PAGE
    assert PAGE >= s_len
    t = np.arange(s_len)
    d_last = (past + t)[:, None] - (past - blk_keys + np.arange(blk_keys))[None, :]
    blast = bias_tiles(rel_bias, d_last, np.ones_like(d_last, bool), 2 * H_B).reshape(n_rows, blk_keys)
    far = bias_tiles(rel_bias, np.full((s_len, LANES), MAX_DIST), np.ones((s_len, LANES), bool), 2 * H_B)
    far = far.reshape(n_rows, LANES)[:, :1]
    cnew = np.arange(PAGE)
    d_new = t[:, None] - cnew[None, :]
    bnew = bias_tiles(rel_bias, d_new, (d_new >= 0) & (cnew[None, :] < s_len), 2 * H_B).reshape(n_rows, PAGE)
    assert past - blk_keys + s_len - 1 >= 0 and np.all(_t5_bucket_np(np.array([blk_keys + 1])) == FAR_BUCKET)

    def page_spec(i):
        return pl.BlockSpec((1, PAGE * H_B, hw), lambda s, j, pt: (pt[s * pages_per_seq + j * n_pg + i], 0, 0))

    seq = pl.BlockSpec((1, s_len, B_W), lambda s, j, pt: (s, 0, 0))
    whole = lambda shape: pl.BlockSpec(shape, lambda s, j, pt: (0,) * len(shape))
    grid_spec = pltpu.PrefetchScalarGridSpec(
        num_scalar_prefetch=1,
        grid=(s_n, pages_per_seq // n_pg),
        in_specs=[seq, seq, seq] + [page_spec(i) for i in range(n_pg)] * 2 + [
            whole((n_rows, blk_keys)), whole((n_rows, PAGE)), whole((n_rows, 1)),
            whole((4, DH_B)), whole((1, hw))],
        out_specs=seq,
        scratch_shapes=[
            pltpu.VMEM((n_rows, 1), F32),
            pltpu.VMEM((n_rows, 1), F32),
            pltpu.VMEM((n_rows, hw), F32),
        ],
    )
    return pl.pallas_call(
        functools.partial(_diff_sample_kernel, n_pg, s_len),
        grid_spec=grid_spec,
        out_shape=jax.ShapeDtypeStruct((s_n, s_len, B_W), F32),
        compiler_params=_params("parallel", "arbitrary"), name="diff_attn_sample",
    )(page_table.reshape(-1).astype(jnp.int32), q, k, v, *([ck] * n_pg), *([cv] * n_pg),
      blast, bnew, far, diff_lambda.astype(F32), subln_w.reshape(1, 2 * DH_B).astype(F32))


def _kv_variants(x, kv):
    lo = lax.broadcasted_iota(jnp.int32, (1, LANES), 1) < DH_C
    rolled = pltpu.roll(x, DH_C, 1)
    a_src, b_src = (x, rolled) if kv == 0 else (rolled, x)
    zero = jnp.zeros_like(x)
    return jnp.where(lo, a_src, zero).astype(BF16), jnp.where(lo, zero, b_src).astype(BF16)


def _sink_attend(units):
    scores = [[lax.dot_general(q_bf, kx, (((1,), (1,)), ((), ())), preferred_element_type=F32) + bias
               for kx, bias in zip(k_ab, bias_ab)] for q_bf, k_ab, _, bias_ab, _ in units]
    probs = []
    for s_ab, (_, _, _, _, sink_ab) in zip(scores, units):
        p_ab = []
        for s, sink in zip(s_ab, sink_ab):
            m = jnp.maximum(jnp.max(s, axis=-1, keepdims=True), sink)
            e = jnp.exp(s - m)
            p_ab.append((e / (jnp.sum(e, axis=-1, keepdims=True) + jnp.exp(sink - m))).astype(BF16))
        probs.append(p_ab)
    return [jnp.dot(p_ab[0], v_ab[0], preferred_element_type=F32) + jnp.dot(p_ab[1], v_ab[1], preferred_element_type=F32)
            for p_ab, (_, _, v_ab, _, _) in zip(probs, units)]


def _swa_prompt_kernel(q_ref, kp_ref, kc_ref, vp_ref, vc_ref, bias_ref, sink_ref, o_ref):
    n = pl.program_id(1)
    keys = jnp.concatenate([kp_ref[0], kc_ref[0]], axis=0)
    vals = jnp.concatenate([vp_ref[0], vc_ref[0]], axis=0)
    col = lax.broadcasted_iota(jnp.int32, (1, 2 * WINDOW), 1)
    first_blk = jnp.where((col < WINDOW) & (n == 0), NEG, 0.0)
    n_slot = G_C // 2
    units, slices = [], []
    for kv in range(H_KV_C):
        k_ab = _kv_variants(keys, kv)
        v_ab = _kv_variants(vals, kv)
        for ps in range(n_slot):
            h0 = kv * G_C + 2 * ps
            sl = slice(kv * G_C * DH_C + ps * LANES, kv * G_C * DH_C + (ps + 1) * LANES)
            q = (q_ref[0, :, sl] * (DH_C ** -0.5)).astype(BF16)
            bias_ab = (bias_ref[h0] + first_blk, bias_ref[h0 + 1] + first_blk)
            units.append((q, k_ab, v_ab, bias_ab, (sink_ref[h0], sink_ref[h0 + 1])))
            slices.append(sl)
    for sl, o in zip(slices, _sink_attend(units)):
        o_ref[0, :, sl] = o


def swa_prompt(q, k, v, rel_bias, sinks):
    b, t, _ = q.shape
    nb = t // WINDOW
    i = np.arange(WINDOW)
    j = np.arange(2 * WINDOW)
    dist = WINDOW + i[:, None] - j[None, :]
    bias = bias_tiles(rel_bias, dist, (dist >= 0) & (dist <= WINDOW), H_C)
    cur = lambda bi, n: (bi, n, 0)
    prev = lambda bi, n: (bi, jnp.maximum(n - 1, 0), 0)
    return pl.pallas_call(
        _swa_prompt_kernel,
        grid=(b, nb),
        in_specs=[
            pl.BlockSpec((1, WINDOW, Q_C), cur),
            pl.BlockSpec((1, WINDOW, KV_C), prev), pl.BlockSpec((1, WINDOW, KV_C), cur),
            pl.BlockSpec((1, WINDOW, KV_C), prev), pl.BlockSpec((1, WINDOW, KV_C), cur),
            pl.BlockSpec((H_C, WINDOW, 2 * WINDOW), lambda bi, n: (0, 0, 0)),
            pl.BlockSpec(memory_space=pltpu.SMEM),
        ],
        out_specs=pl.BlockSpec((1, WINDOW, Q_C), cur),
        out_shape=jax.ShapeDtypeStruct((b, t, Q_C), F32),
        compiler_params=_params("parallel", "parallel"), name="swa_prompt",
    )(q, k, k, v, v, bias, sinks.astype(F32))


def _swa_sample_kernel(gs, s_len, q_ref, kn_ref, vn_ref, ck_ref, cv_ref, bias_ref, sink_ref, o_ref):
    n_slot = G_C // 2
    pad = jnp.zeros((WINDOW - s_len, LANES), F32)
    units, where = [], []
    for si in range(gs):
        keys = jnp.concatenate([ck_ref[si], kn_ref[si], pad], axis=0)
        vals = jnp.concatenate([cv_ref[si], vn_ref[si], pad], axis=0)
        for kv in range(H_KV_C):
            k_ab = _kv_variants(keys, kv)
            v_ab = _kv_variants(vals, kv)
            base = kv * G_C * DH_C
            q = jnp.concatenate([q_ref[si, :, base + ps * LANES:base + (ps + 1) * LANES] for ps in range(n_slot)],
                                axis=0)
            q = (q * (DH_C ** -0.5)).astype(BF16)
            units.append((q, k_ab, v_ab, (bias_ref[kv, 0], bias_ref[kv, 1]), (sink_ref[kv, 0], sink_ref[kv, 1])))
            where.append((si, base))
    for (si, base), o in zip(where, _sink_attend(units)):
        for ps in range(n_slot):
            o_ref[si, :, base + ps * LANES:base + (ps + 1) * LANES] = o[ps * s_len:(ps + 1) * s_len]


def swa_sample(q, k, v, cache_k, cache_v, rel_bias, sinks, gs):
    s_n, s_len, _ = q.shape
    assert s_n % gs == 0 and cache_k.shape[1] == WINDOW
    ck = cache_k.reshape(s_n, WINDOW, KV_C)
    cv = cache_v.reshape(s_n, WINDOW, KV_C)
    tq = np.arange(s_len)
    j = np.arange(2 * WINDOW)
    dist = WINDOW + tq[:, None] - j[None, :]
    valid = (dist >= 0) & (dist <= WINDOW) & (j[None, :] < WINDOW + s_len)
    n_slot = G_C // 2
    bias = bias_tiles(rel_bias, dist, valid, H_C)
    bias = bias.reshape(H_KV_C, n_slot, 2, s_len, 2 * WINDOW).transpose(0, 2, 1, 3, 4)
    bias = bias.reshape(H_KV_C, 2, n_slot * s_len, 2 * WINDOW)
    sk = sinks.astype(F32).reshape(H_KV_C, n_slot, 2).transpose(0, 2, 1)
    sk = jnp.repeat(sk, s_len, axis=-1).reshape(H_KV_C, 2, n_slot * s_len, 1)
    grp = lambda i: (i, 0, 0)
    whole = lambda shape: pl.BlockSpec(shape, lambda i: (0,) * len(shape))
    return pl.pallas_call(
        functools.partial(_swa_sample_kernel, gs, s_len),
        grid=(s_n // gs,),
        in_specs=[
            pl.BlockSpec((gs, s_len, Q_C), grp),
            pl.BlockSpec((gs, s_len, KV_C), grp), pl.BlockSpec((gs, s_len, KV_C), grp),
            pl.BlockSpec((gs, WINDOW, KV_C), grp), pl.BlockSpec((gs, WINDOW, KV_C), grp),
            whole(bias.shape), whole(sk.shape),
        ],
        out_specs=pl.BlockSpec((gs, s_len, Q_C), grp),
        out_shape=jax.ShapeDtypeStruct((s_n, s_len, Q_C), F32),
        compiler_params=_params("parallel"), name="swa_sample",
    )(q, k, v, ck, cv, bias, sk)


ROUTE_BIG = 1 << 20


def _out_route_kernel(n_in, part_tiles, *refs):
    n_parts = len(part_tiles)
    per = 1 + n_in
    w_refs = refs[n_parts * per:n_parts * per + n_in]
    rest = refs[n_parts * per + n_in:]
    nw_ref, wr_ref, br_ref = rest[:3]
    hn_refs = rest[3:3 + n_parts]
    xn_ref, rt_ref, cout_ref, cnt_scr, tri_scr = rest[3 + n_parts:]
    step = pl.program_id(0)

    @pl.when(step == 0)
    def _():
        cnt_scr[...] = jnp.zeros(cnt_scr.shape, F32)
        ri = lax.broadcasted_iota(jnp.int32, tri_scr.shape, 0)
        ci = lax.broadcasted_iota(jnp.int32, tri_scr.shape, 1)
        tri_scr[...] = (ri > ci).astype(BF16)

    first = 0
    for p, nt in enumerate(part_tiles):
        part_refs = refs[p * per:(p + 1) * per]

        @pl.when((step >= first) & (step < first + nt))
        def _(part_refs=part_refs, hn_ref=hn_refs[p]):
            _out_route_tile(part_refs[0], part_refs[1:], w_refs, nw_ref, wr_ref, br_ref, hn_ref, xn_ref, rt_ref,
                            cnt_scr, tri_scr)
        first += nt

    @pl.when(step == pl.num_programs(0) - 1)
    def _():
        cout_ref[...] = cnt_scr[...]


def _out_route_tile(h_ref, o_refs, w_refs, nw_ref, wr_ref, br_ref, hn_ref, xn_ref, rt_ref, cnt_scr, tri_scr):
    h = h_ref[...]
    for o_ref, w_ref in zip(o_refs, w_refs):
        h = h + jnp.dot(o_ref[...].astype(BF16), w_ref[...], preferred_element_type=F32)
    hn_ref[...] = h
    xn = h * lax.rsqrt(jnp.mean(h * h, axis=-1, keepdims=True) + EPS) * nw_ref[...]
    xn_ref[...] = xn
    x_hi = xn.astype(BF16)
    x_lo = (xn - x_hi.astype(F32)).astype(BF16)
    logit = (jnp.dot(x_hi, wr_ref[0], preferred_element_type=F32) + jnp.dot(x_lo, wr_ref[0], preferred_element_type=F32)
             + jnp.dot(x_hi, wr_ref[1], preferred_element_type=F32) + br_ref[...])
    lane = lax.broadcasted_iota(jnp.int32, logit.shape, 1)
    gmask = lane < N_GROUPS
    gl = jnp.where(gmask, logit, NEG)
    gmax = jnp.max(gl, axis=-1, keepdims=True)
    gsel = jnp.min(jnp.where(gl == gmax, lane, ROUTE_BIG), axis=-1, keepdims=True)
    gw = 1.0 / jnp.sum(jnp.where(gmask, jnp.exp(gl - gmax), 0.0), axis=-1, keepdims=True)
    eid = lane - N_GROUPS
    emask = (eid >= 0) & (eid < N_EXPERTS) & (jnp.right_shift(eid, 3) == gsel)
    el = jnp.where(emask, logit, NEG)
    v1 = jnp.max(el, axis=-1, keepdims=True)
    i1 = jnp.min(jnp.where((el == v1) & emask, lane, ROUTE_BIG), axis=-1, keepdims=True)
    emask2 = emask & (lane != i1)
    el2 = jnp.where(emask2, logit, NEG)
    v2 = jnp.max(el2, axis=-1, keepdims=True)
    i2 = jnp.min(jnp.where((el2 == v2) & emask2, lane, ROUTE_BIG), axis=-1, keepdims=True)
    e = jnp.exp(v2 - v1)
    w1 = gw / (1.0 + e)
    w2 = gw * e / (1.0 + e)
    onehot = ((lane == i1) | (lane == i2)).astype(BF16)
    before = jnp.dot(tri_scr[...], onehot, preferred_element_type=F32) + cnt_scr[...]
    r1 = jnp.sum(jnp.where(lane == i1, before, 0.0), axis=-1, keepdims=True)
    r2 = jnp.sum(jnp.where(lane == i2, before, 0.0), axis=-1, keepdims=True)
    cnt_scr[...] = cnt_scr[...] + jnp.sum(onehot.astype(F32), axis=0, keepdims=True)
    rt = jnp.where(lane == 0, (i1 - N_GROUPS).astype(F32),
                   jnp.where(lane == 1, (i2 - N_GROUPS).astype(F32),
                             jnp.where(lane == 2, w1, jnp.where(lane == 3, w2,
                                                                jnp.where(lane == 4, r1,
                                                                          jnp.where(lane == 5, r2, 0.0))))))
    rt_ref[...] = rt


def out_route(parts, weights_bf16, norm_w, w_group, b_group, w_router, b_router, tm=512):
    d = parts[0][0].shape[1]
    n_in = len(weights_bf16)
    tm = min([tm] + [h.shape[0] for h, _ in parts])
    assert all(h.shape[0] % tm == 0 for h, _ in parts) and EXP_PER_GROUP == 8
    part_tiles = tuple(h.shape[0] // tm for h, _ in parts)
    n_all = sum(h.shape[0] for h, _ in parts)
    wr = jnp.zeros((d, LANES), F32).at[:, :N_GROUPS].set(w_group).at[:, N_GROUPS:N_GROUPS + N_EXPERTS].set(w_router)
    br = jnp.zeros((1, LANES), F32).at[0, :N_GROUPS].set(b_group).at[0, N_GROUPS:N_GROUPS + N_EXPERTS].set(b_router)
    wr_hi = wr.astype(BF16)
    wr = jnp.stack([wr_hi, (wr - wr_hi.astype(F32)).astype(BF16)])
    row = lambda i: (i, 0)
    fixed = lambda i: (0, 0)

    def part_row(first, nt):
        return lambda i: (jnp.clip(i - first, 0, nt - 1), 0)

    in_specs, args, hn_specs, hn_shapes, first = [], [], [], [], 0
    for (h, outs), nt in zip(parts, part_tiles):
        assert len(outs) == n_in
        in_specs += [pl.BlockSpec((tm, a.shape[1]), part_row(first, nt)) for a in (h, *outs)]
        args += [h, *outs]
        hn_specs.append(pl.BlockSpec((tm, d), part_row(first, nt)))
        hn_shapes.append(jax.ShapeDtypeStruct(h.shape, F32))
        first += nt
    in_specs += [pl.BlockSpec(w.shape, fixed) for w in weights_bf16]
    in_specs += [pl.BlockSpec((1, d), fixed), pl.BlockSpec((2, d, LANES), lambda i: (0, 0, 0)),
                 pl.BlockSpec((1, LANES), fixed)]
    res = pl.pallas_call(
        functools.partial(_out_route_kernel, n_in, part_tiles),
        grid=(sum(part_tiles),), in_specs=in_specs,
        out_specs=hn_specs + [pl.BlockSpec((tm, d), row), pl.BlockSpec((tm, LANES), row),
                              pl.BlockSpec((1, LANES), fixed)],
        out_shape=hn_shapes + [jax.ShapeDtypeStruct((n_all, d), F32), jax.ShapeDtypeStruct((n_all, LANES), F32),
                               jax.ShapeDtypeStruct((1, LANES), F32)],
        scratch_shapes=[pltpu.VMEM((1, LANES), F32), pltpu.VMEM((tm, tm), BF16)],
        compiler_params=_params("arbitrary"), name="out_route",
    )(*args, *weights_bf16, norm_w.reshape(1, d), wr, br)
    return res[:len(parts)], res[len(parts)], res[len(parts) + 1], res[len(parts) + 2]


def _experts_kernel(te_ref, nv_ref, x_ref, wg_ref, wu_ref, wd_ref, y_ref, wg_bf, wu_bf, wd_bf):
    i = pl.program_id(0)

    @pl.when((i == 0) | (te_ref[i] != te_ref[jnp.maximum(i - 1, 0)]))
    def _():
        wg_bf[...] = wg_ref[0, 0].astype(BF16)
        wu_bf[...] = wu_ref[0, 0].astype(BF16)
        wd_bf[...] = wd_ref[0, 0].astype(BF16)

    @pl.when(i < nv_ref[0])
    def _():
        x = x_ref[...].astype(BF16)
        g = jnp.dot(x, wg_bf[...], preferred_element_type=F32)
        u = jnp.dot(x, wu_bf[...], preferred_element_type=F32)
        hh = (_silu(g) * u).astype(BF16)
        y_ref[...] = jnp.dot(hh, wd_bf[...], preferred_element_type=F32)

    @pl.when(i >= nv_ref[0])
    def _():
        y_ref[...] = jnp.zeros(y_ref.shape, F32)


def moe_experts(xs, tile_expert, n_valid, w_gate, w_up, w_down, layer, tm):
    p, d = xs.shape
    f = w_gate.shape[-1]
    n_tiles = p // tm
    grid_spec = pltpu.PrefetchScalarGridSpec(
        num_scalar_prefetch=2,
        grid=(n_tiles,),
        in_specs=[
            pl.BlockSpec((tm, d), lambda i, te, nv: (i, 0)),
            pl.BlockSpec((1, 1, d, f), lambda i, te, nv: (layer, te[i], 0, 0)),
            pl.BlockSpec((1, 1, d, f), lambda i, te, nv: (layer, te[i], 0, 0)),
            pl.BlockSpec((1, 1, f, d), lambda i, te, nv: (layer, te[i], 0, 0)),
        ],
        out_specs=pl.BlockSpec((tm, d), lambda i, te, nv: (i, 0)),
        scratch_shapes=[pltpu.VMEM((d, f), BF16), pltpu.VMEM((d, f), BF16), pltpu.VMEM((f, d), BF16)],
    )
    return pl.pallas_call(
        _experts_kernel, grid_spec=grid_spec,
        out_shape=jax.ShapeDtypeStruct((p, d), F32),
        compiler_params=_params("arbitrary"), name="moe_experts",
    )(tile_expert, n_valid, xs, w_gate, w_up, w_down)


def _take_rows(x, idx):
    return x.at[idx].get(mode="promise_in_bounds", unique_indices=True)


def hier_moe(xn, route, counts, w_gate, w_up, w_down, layer, tm=512):
    n, d = xn.shape
    e_idx = route[:, :TOP_K].astype(jnp.int32)
    rank = route[:, 2 * TOP_K:3 * TOP_K].astype(jnp.int32)
    cnt = counts[0, N_GROUPS:N_GROUPS + N_EXPERTS].astype(jnp.int32)
    padded = ((cnt + tm - 1) // tm) * tm
    pad_ends = jnp.cumsum(padded)
    pad_starts = pad_ends - padded
    experts = jnp.arange(N_EXPERTS, dtype=jnp.int32)
    pos = jnp.sum(jnp.where(e_idx[..., None] == experts, pad_starts, 0), axis=-1) + rank
    n_tiles = -(-(n * TOP_K + N_EXPERTS * (tm - 1)) // tm)
    p = n_tiles * tm
    tok = jnp.broadcast_to(jnp.arange(n, dtype=jnp.int32)[:, None], (n, TOP_K))
    n_valid = (pad_ends[-1] // tm).astype(jnp.int32)
    tile_start = jnp.minimum(jnp.arange(n_tiles, dtype=jnp.int32), n_valid - 1) * tm
    tile_e = jnp.sum((tile_start[:, None] >= pad_ends[None, :]).astype(jnp.int32), axis=-1)
    tile_e = jnp.minimum(tile_e, N_EXPERTS - 1)
    _, tok_sorted = lax.sort_key_val(pos.reshape(-1), tok.reshape(-1))
    starts = jnp.cumsum(cnt) - cnt
    rows = jnp.arange(p, dtype=jnp.int32).reshape(n_tiles, tm)
    off = rows - pad_starts[tile_e][:, None]
    src = jnp.where(off < cnt[tile_e][:, None], starts[tile_e][:, None] + off, rows % (n * TOP_K))
    row_tok = tok_sorted.at[src.reshape(-1)].get(mode="promise_in_bounds")
    xs = xn.at[row_tok].get(mode="promise_in_bounds")
    ys = moe_experts(xs, tile_e, n_valid.reshape(1), w_gate, w_up, w_down, layer, tm)
    return tuple(_take_rows(ys, pos[:, kk]) for kk in range(TOP_K))


def _final_norm_kernel(h_ref, rt_ref, a_ref, b_ref, nw_ref, o_ref):
    h = _add_expert_outputs(h_ref[...], rt_ref, (a_ref, b_ref))
    o_ref[...] = h * lax.rsqrt(jnp.mean(h * h, axis=-1, keepdims=True) + EPS) * nw_ref[...]


def final_norm(h, route, a, b, norm_w, add_row0=0, tm=256):
    n, d = h.shape
    tm = min(tm, n)
    assert add_row0 % tm == 0
    row = pl.BlockSpec((tm, d), lambda i: (i, 0))
    add_row = pl.BlockSpec((tm, d), lambda i: (i + add_row0 // tm, 0))
    return pl.pallas_call(
        _final_norm_kernel, grid=(n // tm,),
        in_specs=[row, pl.BlockSpec((tm, LANES), lambda i: (i + add_row0 // tm, 0)), add_row, add_row,
                  pl.BlockSpec((1, d), lambda i: (0, 0))],
        out_specs=row, out_shape=jax.ShapeDtypeStruct((n, d), F32),
        compiler_params=_params("parallel"), name="final_norm",
    )(h, route, a, b, norm_w.reshape(1, d))


def kernel(x_prompt, x_sample, state_a_conv, state_a_ssm, cache_b_k, cache_b_v, cache_c_k, cache_c_v, page_table, norm_mix, norm_ffn, norm_final, rel_bias, w_in0, conv_w, a_log, dt_bias, gdn_norm_w, diff_lambda, diff_subln_w, w_out0, w_in1, b_in1, sinks, w_out1, w_group, b_group, w_router, b_router, w_gate, w_up, w_down):
    bp, tp, d = x_prompt.shape
    sn, sl, _ = x_sample.shape
    n_p, n_s = bp * tp, sn * sl
    parts = ((x_prompt.reshape(n_p, d), bp, tp), (x_sample.reshape(n_s, d), sn, sl))

    off_z = CONV_CH
    off_a = off_z + H_A * DV_A
    off_qb = off_a + 2 * H_A
    w0 = jnp.concatenate([w_in0[:, :off_a], w_in0[:, off_a:off_qb],
                          jnp.zeros((d, LANES - 2 * H_A), w_in0.dtype), w_in0[:, off_qb:]], axis=1).astype(BF16)
    widths0 = (CONV_CH, H_A * DV_A, LANES, B_W, B_W, B_W)
    w_out0_bf = w_out0.astype(BF16)
    w_out0_parts = [w_out0_bf[:H_A * DV_A], w_out0_bf[H_A * DV_A:]]

    row0 = (0, n_p)

    def moe_layer(layer, xn_all, route_all, counts):
        return (route_all,) + hier_moe(xn_all, route_all, counts, w_gate, w_up, w_down, layer)

    conv_states = (jnp.zeros((bp, CONV_W - 1, CONV_CH), F32), state_a_conv)
    ssm_states = (jnp.zeros((bp, H_A, DK_A, DV_A), F32), state_a_ssm)
    mixed, conv_new, ssm_new, kb_new, vb_new = [], [], [], [], []
    for idx, (x2, b, t) in enumerate(parts):
        qkv, z, ab, qb, kb, vb, kb4, vb4 = norm_proj(x2, [], norm_mix[0], w0, None, widths0, emit_h=False,
                                                     head_split=(4, 5))
        r3 = lambda a: a.reshape(b, t, a.shape[-1])
        o_a, s_new = gdn_heads(r3(qkv), r3(z), r3(ab), conv_states[idx], ssm_states[idx], conv_w, a_log, dt_bias,
                               gdn_norm_w, min(GDN_CHUNK, t), 2 if idx == 0 else 8)
        if idx == 0:
            o_b = diff_attn_prompt(r3(qb), r3(kb), r3(vb), rel_bias, diff_lambda, diff_subln_w, 512)
        else:
            o_b = diff_attn_sample(r3(qb), r3(kb), r3(vb), cache_b_k, cache_b_v, page_table, rel_bias, diff_lambda,
                                   diff_subln_w, min(32, page_table.shape[1]))
        mixed.append((x2, [o_a.reshape(b * t, -1), o_b.reshape(b * t, -1)]))
        conv_new.append(r3(qkv)[:, t - (CONV_W - 1):, :])
        ssm_new.append(s_new)
        kb_new.append(kb4.reshape(b, t, H_B, 2 * DH_B))
        vb_new.append(vb4.reshape(b, t, H_B, 2 * DH_B))
    hs, xn_all, route_all, counts = out_route(mixed, w_out0_parts, norm_ffn[0], w_group[0], b_group[0],
                                              w_router[0], b_router[0])
    moe_out = moe_layer(0, xn_all, route_all, counts)

    w1 = w_in1.astype(BF16)
    w_out1_bf = w_out1.astype(BF16)
    mixed, kc_new, vc_new = [], [], []
    for idx, (_, b, t) in enumerate(parts):
        h2, q, k, v = norm_proj(hs[idx], moe_out, norm_mix[1], w1, b_in1, (Q_C, KV_C, KV_C),
                                emit_h=True, add_row0=row0[idx])
        r3 = lambda a: a.reshape(b, t, a.shape[-1])
        if idx == 0:
            o_c = swa_prompt(r3(q), r3(k), r3(v), rel_bias, sinks)
            kc_new.append(r3(k)[:, t - WINDOW:].reshape(b, WINDOW, H_KV_C, DH_C))
            vc_new.append(r3(v)[:, t - WINDOW:].reshape(b, WINDOW, H_KV_C, DH_C))
        else:
            o_c = swa_sample(r3(q), r3(k), r3(v), cache_c_k, cache_c_v, rel_bias, sinks, 8)
            kc_new.append(jnp.concatenate([cache_c_k[:, t:], k.reshape(b, t, H_KV_C, DH_C)], axis=1))
            vc_new.append(jnp.concatenate([cache_c_v[:, t:], v.reshape(b, t, H_KV_C, DH_C)], axis=1))
        mixed.append((h2, [o_c.reshape(b * t, -1)]))
    hs2, xn_all, route_all, counts = out_route(mixed, [w_out1_bf], norm_ffn[1], w_group[1], b_group[1],
                                               w_router[1], b_router[1])
    moe_out = moe_layer(1, xn_all, route_all, counts)

    y_out = [final_norm(hs2[idx], *moe_out, norm_final, add_row0=row0[idx]).reshape(b, t, d)
             for idx, (_, b, t) in enumerate(parts)]
    return (y_out[0], y_out[1], conv_new[0], conv_new[1], ssm_new[0], ssm_new[1],
            kb_new[0], vb_new[0], kb_new[1], vb_new[1], kc_new[0], vc_new[0], kc_new[1], vc_new[1])
```

```python
import functools
import math

import jax
import jax.numpy as jnp
import numpy as np
from jax import lax
from jax.experimental import pallas as pl
from jax.experimental.pallas import tpu as pltpu

F32 = jnp.float32
BF16 = jnp.bfloat16
EPS = 1e-6
NEG = -1e30
LOG2E = math.log2(math.e)
PAGE_RING = 3
ONES_ROWS = 16
LANES = 128
VMEM_LIMIT = 56 * 1024 * 1024

D_MODEL = 1024
H_A, DK_A, DV_A, CONV_W = 4, 128, 128, 4
CONV_CH = 2 * H_A * DK_A + H_A * DV_A
GDN_CHUNK = 64
GDN_PRECISE_INV = False
H_B, DH_B = 4, 64
B_W = H_B * 2 * DH_B
PAGE = 128
H_C, H_KV_C, DH_C, WINDOW = 16, 2, 64, 128
G_C = H_C // H_KV_C
Q_C = H_C * DH_C
KV_C = H_KV_C * DH_C
N_BUCKETS, MAX_DIST = 32, 128
N_GROUPS, EXP_PER_GROUP, TOP_K, D_EXPERT = 4, 8, 2, 256
N_EXPERTS = N_GROUPS * EXP_PER_GROUP
LAM_INIT0 = 0.8 - 0.6 * math.exp(-0.3 * 0)


def _params(*sem):
    return pltpu.CompilerParams(dimension_semantics=sem, vmem_limit_bytes=VMEM_LIMIT)


def _mm(a, b, precise=False):
    if precise:
        return jnp.dot(a.astype(F32), b.astype(F32), preferred_element_type=F32, precision=lax.Precision.HIGHEST)
    return jnp.dot(a.astype(BF16), b.astype(BF16), preferred_element_type=F32)


def _mm_nt(a, b, precise=False):
    dn = (((1,), (1,)), ((), ()))
    if precise:
        return lax.dot_general(a.astype(F32), b.astype(F32), dn, preferred_element_type=F32,
                               precision=lax.Precision.HIGHEST)
    return lax.dot_general(a.astype(BF16), b.astype(BF16), dn, preferred_element_type=F32)


def _mm_tn(a, b, precise=False):
    dn = (((0,), (0,)), ((), ()))
    if precise:
        return lax.dot_general(a.astype(F32), b.astype(F32), dn, preferred_element_type=F32,
                               precision=lax.Precision.HIGHEST)
    return lax.dot_general(a.astype(BF16), b.astype(BF16), dn, preferred_element_type=F32)


def _sigmoid(x):
    return 1.0 / (1.0 + jnp.exp(-x))


def _silu(x):
    return x * _sigmoid(x)


def _softplus(x):
    return jnp.maximum(x, 0.0) + jnp.log(1.0 + jnp.exp(-jnp.abs(x)))


def _add_expert_outputs(h, route_ref, y_refs):
    rt = route_ref[...]
    for kk, y_ref in enumerate(y_refs):
        h = h + rt[:, TOP_K + kk:TOP_K + kk + 1] * y_ref[...]
    return h


def _norm_proj_kernel(n_add, widths, has_bias, emit_h, head_split, *refs):
    x_ref = refs[0]
    add_refs = refs[1:1 + n_add]
    nw_ref, w_ref = refs[1 + n_add], refs[2 + n_add]
    pos = 3 + n_add
    b_ref = None
    if has_bias:
        b_ref = refs[pos]
        pos += 1
    outs = refs[pos:]
    h = x_ref[...]
    if n_add:
        h = _add_expert_outputs(h, add_refs[0], add_refs[1:])
    if emit_h:
        outs[0][...] = h
        outs = outs[1:]
    split_refs = outs[len(widths):]
    xn = h * lax.rsqrt(jnp.mean(h * h, axis=-1, keepdims=True) + EPS) * nw_ref[...]
    xb = xn.astype(BF16)
    off = 0
    for j, (o_ref, wd) in enumerate(zip(outs, widths)):
        y = jnp.dot(xb, w_ref[:, off:off + wd], preferred_element_type=F32)
        if has_bias:
            y = y + b_ref[:, off:off + wd]
        o_ref[...] = y
        if j in head_split:
            s_ref = split_refs[head_split.index(j)]
            for hh in range(wd // LANES):
                s_ref[:, hh, :] = y[:, hh * LANES:(hh + 1) * LANES]
        off += wd


def norm_proj(x, addends, norm_w, w_bf16, bias, widths, emit_h, head_split=(), add_row0=0, tm=512):
    n, d = x.shape
    m = w_bf16.shape[1]
    tm = min(tm, n)
    assert sum(widths) == m and n % tm == 0
    row = lambda i: (i, 0)
    fixed = lambda i: (0, 0)
    assert add_row0 % tm == 0
    add_row = lambda i: (i + add_row0 // tm, 0)
    in_specs = [pl.BlockSpec((tm, d), row)] + [pl.BlockSpec((tm, a.shape[1]), add_row) for a in addends]
    in_specs += [pl.BlockSpec((1, d), fixed), pl.BlockSpec((d, m), fixed)]
    args = [x, *addends, norm_w.reshape(1, d), w_bf16]
    if bias is not None:
        in_specs.append(pl.BlockSpec((1, m), fixed))
        args.append(bias.reshape(1, m))
    out_shape, out_specs = [], []
    if emit_h:
        out_shape.append(jax.ShapeDtypeStruct((n, d), F32))
        out_specs.append(pl.BlockSpec((tm, d), row))
    for wd in widths:
        out_shape.append(jax.ShapeDtypeStruct((n, wd), F32))
        out_specs.append(pl.BlockSpec((tm, wd), row))
    for j in head_split:
        out_shape.append(jax.ShapeDtypeStruct((n, widths[j] // LANES, LANES), F32))
        out_specs.append(pl.BlockSpec((tm, widths[j] // LANES, LANES), lambda i: (i, 0, 0)))
    return pl.pallas_call(
        functools.partial(_norm_proj_kernel, len(addends), tuple(widths), bias is not None, emit_h,
                          tuple(head_split)),
        grid=(n // tm,), in_specs=in_specs, out_specs=out_specs, out_shape=out_shape,
        compiler_params=_params("parallel"), name="norm_proj",
    )(*args)


def _gdn_kernel(c_len, nb, qkv_ref, z_ref, ab_ref, cbuf_ref, s0_ref, convw_ref, gpar_ref, nw_ref,
                o_ref, sfin_ref, xp_scr, s_scr):
    c = pl.program_id(1)
    n_c = pl.num_programs(1)
    hist = CONV_W - 1
    base = 8 - hist

    @pl.when(c == 0)
    def _():
        xp_scr[:, base:8, :] = cbuf_ref[...]
        s_scr[...] = s0_ref[...]

    ri = lax.broadcasted_iota(jnp.int32, (c_len, c_len), 0)
    ci = lax.broadcasted_iota(jnp.int32, (c_len, c_len), 1)
    lower = ri >= ci
    strict = ri > ci
    eye = (ri == ci).astype(F32)
    lower_f = lower.astype(F32)

    units = []
    for bi in range(nb):
        xp_scr[bi, 8:8 + c_len, :] = qkv_ref[bi]
        y = xp_scr[bi, base:base + c_len, :] * convw_ref[0:1, :]
        for j in range(1, CONV_W):
            y = y + xp_scr[bi, base + j:base + j + c_len, :] * convw_ref[j:j + 1, :]
        xp_scr[bi, base:8, :] = xp_scr[bi, base + c_len:8 + c_len, :]
        y = _silu(y)

        ab = ab_ref[bi]
        g_t = gpar_ref[0:1, :] * _softplus(ab + gpar_ref[1:2, :])
        beta_t = _sigmoid(ab)
        gcum_t = _mm(lower_f, g_t, precise=True)
        gcum_tt = gcum_t.T

        for h in range(H_A):
            q = y[:, h * DK_A:(h + 1) * DK_A]
            k = y[:, (H_A + h) * DK_A:(H_A + h + 1) * DK_A]
            v = y[:, 2 * H_A * DK_A + h * DV_A:2 * H_A * DK_A + (h + 1) * DV_A]
            q = q * lax.rsqrt(jnp.sum(q * q, axis=-1, keepdims=True) + EPS) * (DK_A ** -0.5)
            k = k * lax.rsqrt(jnp.sum(k * k, axis=-1, keepdims=True) + EPS)
            gc = gcum_t[:, h:h + 1]
            gr = gcum_tt[h:h + 1, :]
            beta = beta_t[:, H_A + h:H_A + h + 1]
            decay = jnp.where(lower, jnp.exp(jnp.where(lower, gc - gr, 0.0)), 0.0)
            kb = k * beta
            units.append(dict(bi=bi, h=h, q=q, k=k, k_bf=k.astype(BF16), gc=gc, decay=decay, kb=kb,
                              rhs=jnp.concatenate([v * beta, kb * jnp.exp(gc)], axis=1)))

    for un in units:
        un["nmat"] = jnp.where(strict, _mm_nt(un["kb"], un["k_bf"]) * un["decay"], 0.0)
        un["qk"] = jnp.where(lower, _mm_nt(un["q"], un["k_bf"]) * un["decay"], 0.0)
    for un in units:
        un["inv"] = eye - un["nmat"]
        un["pw"] = _mm(un["nmat"], un["nmat"], GDN_PRECISE_INV)
    span = 2
    while span < c_len:
        for un in units:
            un["inv"] = un["inv"] + _mm(un["inv"], un["pw"], GDN_PRECISE_INV)
        span *= 2
        if span < c_len:
            for un in units:
                un["pw"] = _mm(un["pw"], un["pw"], GDN_PRECISE_INV)
    for un in units:
        sol = _mm(un["inv"], un["rhs"], GDN_PRECISE_INV)
        un["u"], un["w"] = sol[:, :DV_A], sol[:, DV_A:]
        un["s"] = s_scr[un["bi"], un["h"]]
        un["s_bf"] = un["s"].astype(BF16)
    for un in units:
        un["v_new"] = un["u"] - _mm(un["w"], un["s_bf"])
        un["o_s"] = _mm(un["q"] * jnp.exp(un["gc"]), un["s_bf"])
    for un in units:
        bi, h, gc = un["bi"], un["h"], un["gc"]
        o = un["o_s"] + _mm(un["qk"], un["v_new"])
        g_last = gc[c_len - 1:c_len, :]
        s_scr[bi, h] = un["s"] * jnp.exp(g_last) + _mm_tn(un["k"] * jnp.exp(g_last - gc), un["v_new"])
        o = o * lax.rsqrt(jnp.mean(o * o, axis=-1, keepdims=True) + EPS) * nw_ref[...]
        o_ref[bi, :, h * DV_A:(h + 1) * DV_A] = o * _silu(z_ref[bi, :, h * DV_A:(h + 1) * DV_A])

    @pl.when(c == n_c - 1)
    def _():
        sfin_ref[...] = s_scr[...]


def gdn_heads(qkv, z, ab, conv_buf, s0, conv_w, a_log, dt_bias, norm_w, c_len, nb):
    b, t, _ = qkv.shape
    assert t % c_len == 0 and b % nb == 0
    convw_t = conv_w.T
    gpar = jnp.zeros((2, LANES), F32)
    gpar = gpar.at[0, :H_A].set(-jnp.exp(a_log.astype(F32))).at[1, :H_A].set(dt_bias.astype(F32))
    tok = lambda i, j: (i, j, 0)
    seq3 = lambda i, j: (i, 0, 0)
    fixed = lambda i, j: (0, 0)
    return pl.pallas_call(
        functools.partial(_gdn_kernel, c_len, nb),
        grid=(b // nb, t // c_len),
        in_specs=[
            pl.BlockSpec((nb, c_len, CONV_CH), tok),
            pl.BlockSpec((nb, c_len, H_A * DV_A), tok),
            pl.BlockSpec((nb, c_len, LANES), tok),
            pl.BlockSpec((nb, CONV_W - 1, CONV_CH), seq3),
            pl.BlockSpec((nb, H_A, DK_A, DV_A), lambda i, j: (i, 0, 0, 0)),
            pl.BlockSpec((CONV_W, CONV_CH), fixed),
            pl.BlockSpec((2, LANES), fixed),
            pl.BlockSpec((1, DV_A), fixed),
        ],
        out_specs=[
            pl.BlockSpec((nb, c_len, H_A * DV_A), tok),
            pl.BlockSpec((nb, H_A, DK_A, DV_A), lambda i, j: (i, 0, 0, 0)),
        ],
        out_shape=[
            jax.ShapeDtypeStruct((b, t, H_A * DV_A), F32),
            jax.ShapeDtypeStruct((b, H_A, DK_A, DV_A), F32),
        ],
        scratch_shapes=[
            pltpu.VMEM((nb, 8 + c_len, CONV_CH), F32),
            pltpu.VMEM((nb, H_A, DK_A, DV_A), F32),
        ],
        compiler_params=_params("parallel", "arbitrary"), name="gdn_heads",
    )(qkv, z, ab, conv_buf, s0, convw_t, gpar, norm_w.reshape(1, DV_A))


def _t5_bucket_np(dist):
    d = np.maximum(dist, 0)
    max_exact = N_BUCKETS // 2
    ratio = (np.log(np.maximum(d, 1).astype(np.float32) / np.float32(max_exact))
             / np.float32(math.log(MAX_DIST / max_exact)))
    large = np.minimum(max_exact + (ratio * (N_BUCKETS - max_exact)).astype(np.int32), N_BUCKETS - 1)
    return np.where(d < max_exact, d, large).astype(np.int32)


FAR_BUCKET = int(_t5_bucket_np(np.array([MAX_DIST]))[0])
assert np.all(_t5_bucket_np(np.arange(MAX_DIST, 4 * MAX_DIST)) == FAR_BUCKET)


def _bias_kernel(col_group, bucket_ref, rb_ref, out_ref):
    for col, grp in enumerate(col_group):
        bk = bucket_ref[grp]
        acc = jnp.full(bk.shape, NEG, F32)
        for b in range(N_BUCKETS):
            acc = jnp.where(bk == b, rb_ref[b, col], acc)
        out_ref[col] = acc


def bias_tiles(rel_bias, dist, valid, n_cols, col_group=None):
    bucket = np.where(valid, _t5_bucket_np(dist), -1).astype(np.int32)
    if bucket.ndim == 2:
        bucket, col_group = bucket[None], (0,) * n_cols
    _, r, c = bucket.shape
    return pl.pallas_call(
        functools.partial(_bias_kernel, tuple(col_group)),
        in_specs=[pl.BlockSpec(memory_space=pltpu.VMEM), pl.BlockSpec(memory_space=pltpu.SMEM)],
        out_specs=pl.BlockSpec(memory_space=pltpu.VMEM),
        out_shape=jax.ShapeDtypeStruct((n_cols, r, c), F32),
        name="bias_tiles",
    )(jnp.asarray(bucket), rel_bias.astype(F32))


def _diff_lambda(lam_ref):
    lp = lam_ref[...]
    s1 = jnp.sum(lp[0:1] * lp[1:2], axis=-1, keepdims=True)
    s2 = jnp.sum(lp[2:3] * lp[3:4], axis=-1, keepdims=True)
    return jnp.exp(s1) - jnp.exp(s2) + LAM_INIT0


def _diff_finish(o1, o2, lam, nw):
    o = o1 - lam * o2
    return o * lax.rsqrt(jnp.mean(o * o, axis=-1, keepdims=True) + EPS) * nw * (1.0 - LAM_INIT0)


def _softmax_update(s, v_bf, m_ref, l_ref, acc_ref):
    m_prev = m_ref[...]
    m_new = jnp.maximum(m_prev, jnp.max(s, axis=-1, keepdims=True))
    alpha = jnp.exp(m_prev - m_new)
    p = jnp.exp(s - m_new)
    l_ref[...] = alpha * l_ref[...] + jnp.sum(p, axis=-1, keepdims=True)
    acc_ref[...] = alpha * acc_ref[...] + jnp.dot(p.astype(BF16), v_bf, preferred_element_type=F32)
    m_ref[...] = m_new


def _diff_prompt_kernel(blk, qt_ref, kt_ref, q_ref, k_ref, v_ref, t0_ref, t1_ref, far_ref, lam_ref, nw_ref, o_ref,
                        m_scr, acc_scr):
    qi = qt_ref[pl.program_id(1)]
    ki = kt_ref[pl.program_id(1)]
    nsub = blk // LANES
    hw = 2 * DH_B

    @pl.when(ki == 0)
    def _():
        m_scr[...] = jnp.full(m_scr.shape, NEG, F32)
        acc_scr[...] = jnp.zeros(acc_scr.shape, F32)

    def bias_block(case, hm):
        const = jnp.full((LANES, LANES), far_ref[hm], F32)
        neg = jnp.full((LANES, LANES), NEG, F32)
        rows = []
        for jj in range(nsub):
            tiles = []
            for ii in range(nsub):
                sub = ii - jj + (nsub if case == "prev" else 0)
                tiles.append(neg if sub < 0 else t0_ref[hm] if sub == 0 else t1_ref[hm] if sub == 1 else const)
            rows.append(jnp.concatenate(tiles, axis=1))
        return jnp.concatenate(rows, axis=0)

    def step(case):
        q = (q_ref[0] * (DH_B ** -0.5 * LOG2E)).astype(BF16)
        k = k_ref[0].astype(BF16)
        v_t = v_ref[0].T
        first = lax.broadcasted_iota(jnp.int32, (1, hw), 1) < DH_B
        ones = jnp.ones((ONES_ROWS, blk), F32)
        s_all, v_ext = [], []
        for h in range(H_B):
            sl = slice(h * hw, (h + 1) * hw)
            qh, kh = q[:, sl], k[:, sl]
            v_ext.append(jnp.concatenate([v_t[sl], ones], axis=0).astype(BF16))
            for mp in range(2):
                km = jnp.where(first if mp == 0 else jnp.logical_not(first), kh, jnp.zeros_like(kh))
                s_all.append(lax.dot_general(km, qh, (((1,), (1,)), ((), ())),
                                             preferred_element_type=F32))
        p_all, alpha_all = [], []
        for hm, s in enumerate(s_all):
            m_prev = m_scr[hm]
            if case == "far":
                shift = far_ref[hm]
                m_new = jnp.maximum(m_prev, jnp.max(s, axis=0, keepdims=True) + shift)
                p = jnp.exp2(s + (shift - m_new))
            else:
                s = s + bias_block(case, hm)
                m_new = jnp.maximum(m_prev, jnp.max(s, axis=0, keepdims=True))
                p = jnp.exp2(s - m_new)
            alpha_all.append(jnp.exp2(m_prev - m_new))
            p_all.append(p.astype(BF16))
            m_scr[hm] = m_new
        for hm, (p, alpha) in enumerate(zip(p_all, alpha_all)):
            acc_scr[hm] = alpha * acc_scr[hm] + jnp.dot(v_ext[hm // 2], p, preferred_element_type=F32)

    @pl.when(ki < qi - 1)
    def _():
        step("far")

    @pl.when(ki == qi - 1)
    def _():
        step("prev")

    @pl.when(ki == qi)
    def _():
        step("diag")
        lam = _diff_lambda(lam_ref)
        for h in range(H_B):
            a1, a2 = acc_scr[2 * h], acc_scr[2 * h + 1]
            o = a1[:hw] / a1[hw:hw + 1] - lam * (a2[:hw] / a2[hw:hw + 1])
            o = o * lax.rsqrt(jnp.mean(o * o, axis=0, keepdims=True) + EPS) * nw_ref[...] * (1.0 - LAM_INIT0)
            o_ref[0, :, h * hw:(h + 1) * hw] = o.T


def diff_attn_prompt(q, k, v, rel_bias, diff_lambda, subln_w, blk):
    b, t, _ = q.shape
    assert t % blk == 0 and blk % LANES == 0 and blk >= 2 * LANES
    i = np.arange(LANES)
    d0 = i[None, :] - i[:, None]
    t0 = bias_tiles(rel_bias, d0, d0 >= 0, 2 * H_B) * LOG2E
    t1 = bias_tiles(rel_bias, d0 + LANES, np.ones_like(d0, bool), 2 * H_B) * LOG2E
    far = rel_bias[FAR_BUCKET, :2 * H_B].astype(F32) * LOG2E
    nb = t // blk
    hw = 2 * DH_B
    pairs = [(qi, ki) for qi in range(nb) for ki in range(qi + 1)]
    q_tab = jnp.asarray(np.array([p[0] for p in pairs], np.int32))
    k_tab = jnp.asarray(np.array([p[1] for p in pairs], np.int32))
    whole = lambda shape: pl.BlockSpec(shape, lambda bi, pi, qt, kt: (0,) * len(shape))
    grid_spec = pltpu.PrefetchScalarGridSpec(
        num_scalar_prefetch=2,
        grid=(b, len(pairs)),
        in_specs=[
            pl.BlockSpec((1, blk, B_W), lambda bi, pi, qt, kt: (bi, qt[pi], 0)),
            pl.BlockSpec((1, blk, B_W), lambda bi, pi, qt, kt: (bi, kt[pi], 0)),
            pl.BlockSpec((1, blk, B_W), lambda bi, pi, qt, kt: (bi, kt[pi], 0)),
            whole((2 * H_B, LANES, LANES)),
            whole((2 * H_B, LANES, LANES)),
            pl.BlockSpec(memory_space=pltpu.SMEM),
            whole((4, DH_B)),
            whole((hw, 1)),
        ],
        out_specs=pl.BlockSpec((1, blk, B_W), lambda bi, pi, qt, kt: (bi, qt[pi], 0)),
        scratch_shapes=[
            pltpu.VMEM((2 * H_B, 1, blk), F32),
            pltpu.VMEM((2 * H_B, hw + ONES_ROWS, blk), F32),
        ],
    )
    return pl.pallas_call(
        functools.partial(_diff_prompt_kernel, blk),
        grid_spec=grid_spec,
        out_shape=jax.ShapeDtypeStruct((b, t, B_W), F32),
        compiler_params=_params("parallel", "arbitrary"), name="diff_attn_prompt",
    )(q_tab, k_tab, q, k, v, t0, t1, far, diff_lambda.astype(F32), subln_w.reshape(hw, 1).astype(F32))


def _diff_sample_kernel(n_pg, s_len, pages_per_seq, pt_ref, q_ref, kn_ref, vn_ref, ck_hbm, cv_hbm, blast_ref,
                        bnew_ref, far_ref, lam_ref, nw_ref, o_ref, m_scr, l_scr, acc_scr, kbuf, vbuf, sem):
    j = pl.program_id(1)
    n_j = pl.num_programs(1)
    last = n_j - 1
    hw = 2 * DH_B
    step = pl.program_id(0) * n_j + j
    n_steps = pl.num_programs(0) * n_j

    def page_copies(g, slot):
        first_page = (g // n_j) * pages_per_seq + (g % n_j) * n_pg
        out = []
        for i in range(n_pg):
            page = pt_ref[first_page + i]
            out.append(pltpu.make_async_copy(ck_hbm.at[page], kbuf.at[slot, i], sem.at[slot, i]))
            out.append(pltpu.make_async_copy(cv_hbm.at[page], vbuf.at[slot, i], sem.at[slot, n_pg + i]))
        return out

    @pl.when(step == 0)
    def _():
        for ahead in range(PAGE_RING - 1):
            @pl.when(ahead < n_steps)
            def _(ahead=ahead):
                for c in page_copies(ahead, ahead):
                    c.start()

    slot = step % PAGE_RING
    for c in page_copies(step, slot):
        c.wait()
    nxt = step + PAGE_RING - 1

    @pl.when(nxt < n_steps)
    def _():
        for c in page_copies(nxt, nxt % PAGE_RING):
            c.start()

    k_pages = [kbuf.at[slot, i] for i in range(n_pg)]
    v_pages = [vbuf.at[slot, i] for i in range(n_pg)]

    @pl.when(j == 0)
    def _():
        m_scr[...] = jnp.full(m_scr.shape, NEG, F32)
        l_scr[...] = jnp.zeros(l_scr.shape, F32)
        acc_scr[...] = jnp.zeros(acc_scr.shape, F32)

    q = q_ref[0] * (DH_B ** -0.5)
    first = lax.broadcasted_iota(jnp.int32, (1, hw), 1) < DH_B
    rh = 2 * s_len
    q_heads = []
    for h in range(H_B):
        qh = q[:, h * hw:(h + 1) * hw]
        q_heads.append(jnp.concatenate([jnp.where(first, qh, 0.0), jnp.where(first, 0.0, qh)], axis=0).astype(BF16))

    def head_rows(ref, pages, h):
        return jnp.concatenate([r[pl.ds(h, PAGE, stride=H_B), :] for r in pages], axis=0).astype(BF16)

    n_grp = 4 if n_pg % 4 == 0 else 1
    per = n_pg // n_grp
    wg = per * PAGE
    s_units = [[None] * n_grp for _ in range(H_B)]
    for g in range(n_grp):
        for h in range(H_B):
            rows = slice(h * rh, (h + 1) * rh)
            bias = jnp.where(j == last, blast_ref[rows, g * wg:(g + 1) * wg], far_ref[rows, :])
            kgh = head_rows(None, k_pages[g * per:(g + 1) * per], h)
            s_units[h][g] = lax.dot_general(q_heads[h], kgh, (((1,), (1,)), ((), ())),
                                            preferred_element_type=F32) + bias
    m_news, alphas = [], []
    for h in range(H_B):
        m_prev = m_scr[h * rh:(h + 1) * rh]
        m_new = m_prev
        for s in s_units[h]:
            m_new = jnp.maximum(m_new, jnp.max(s, axis=-1, keepdims=True))
        m_news.append(m_new)
        alphas.append(jnp.exp(m_prev - m_new))
    l_new = [alphas[h] * l_scr[h * rh:(h + 1) * rh] for h in range(H_B)]
    acc_new = [alphas[h] * acc_scr[h * rh:(h + 1) * rh] for h in range(H_B)]
    for g in range(n_grp):
        for h in range(H_B):
            p = jnp.exp(s_units[h][g] - m_news[h])
            vgh = head_rows(None, v_pages[g * per:(g + 1) * per], h)
            l_new[h] = l_new[h] + jnp.sum(p, axis=-1, keepdims=True)
            acc_new[h] = acc_new[h] + jnp.dot(p.astype(BF16), vgh, preferred_element_type=F32)
    for h in range(H_B):
        m_scr[h * rh:(h + 1) * rh] = m_news[h]
        l_scr[h * rh:(h + 1) * rh] = l_new[h]
        acc_scr[h * rh:(h + 1) * rh] = acc_new[h]

    @pl.when(j == last)
    def _():
        pad = jnp.zeros((PAGE - s_len, hw), F32)
        for h in range(H_B):
            rows = slice(h * rh, (h + 1) * rh)
            kn = jnp.concatenate([kn_ref[0, :, h * hw:(h + 1) * hw], pad], axis=0).astype(BF16)
            vn = jnp.concatenate([vn_ref[0, :, h * hw:(h + 1) * hw], pad], axis=0).astype(BF16)
            s2 = lax.dot_general(q_heads[h], kn, (((1,), (1,)), ((), ())), preferred_element_type=F32)
            _softmax_update(s2 + bnew_ref[rows], vn, m_scr.at[rows], l_scr.at[rows], acc_scr.at[rows])
        accn = acc_scr[...] / l_scr[...]
        lam = _diff_lambda(lam_ref)
        for h in range(H_B):
            o1 = accn[(2 * h) * s_len:(2 * h + 1) * s_len]
            o2 = accn[(2 * h + 1) * s_len:(2 * h + 2) * s_len]
            o_ref[0, :, h * hw:(h + 1) * hw] = _diff_finish(o1, o2, lam, nw_ref[...])


def diff_attn_sample(q, k, v, cache_k, cache_v, page_table, rel_bias, diff_lambda, subln_w, n_pg):
    s_n, s_len, _ = q.shape
    pages_per_seq = page_table.shape[1]
    assert pages_per_seq % n_pg == 0
    past = pages_per_seq * PAGE
    n_phys = cache_k.shape[0]
    hw = 2 * DH_B
    ck = cache_k.reshape(n_phys, PAGE * H_B, hw)
    cv = cache_v.reshape(n_phys, PAGE * H_B, hw)
    n_rows = 2 * H_B * s_len
    blk_keys = n_pg * PAGE
    assert PAGE >= s_len
    t = np.arange(s_len)
    d_last = (past + t)[:, None] - (past - blk_keys + np.arange(blk_keys))[None, :]
    blast = bias_tiles(rel_bias, d_last, np.ones_like(d_last, bool), 2 * H_B).reshape(n_rows, blk_keys)
    far = bias_tiles(rel_bias, np.full((s_len, LANES), MAX_DIST), np.ones((s_len, LANES), bool), 2 * H_B)
    far = far.reshape(n_rows, LANES)[:, :1]
    cnew = np.arange(PAGE)
    d_new = t[:, None] - cnew[None, :]
    bnew = bias_tiles(rel_bias, d_new, (d_new >= 0) & (cnew[None, :] < s_len), 2 * H_B).reshape(n_rows, PAGE)
    assert past - blk_keys + s_len - 1 >= 0 and np.all(_t5_bucket_np(np.array([blk_keys + 1])) == FAR_BUCKET)

    seq = pl.BlockSpec((1, s_len, B_W), lambda s, j, pt: (s, 0, 0))
    whole = lambda shape: pl.BlockSpec(shape, lambda s, j, pt: (0,) * len(shape))
    hbm = pl.BlockSpec(memory_space=pl.ANY)
    grid_spec = pltpu.PrefetchScalarGridSpec(
        num_scalar_prefetch=1,
        grid=(s_n, pages_per_seq // n_pg),
        in_specs=[seq, seq, seq, hbm, hbm,
                  whole((n_rows, blk_keys)), whole((n_rows, PAGE)), whole((n_rows, 1)),
                  whole((4, DH_B)), whole((1, hw))],
        out_specs=seq,
        scratch_shapes=[
            pltpu.VMEM((n_rows, 1), F32),
            pltpu.VMEM((n_rows, 1), F32),
            pltpu.VMEM((n_rows, hw), F32),
            pltpu.VMEM((PAGE_RING, n_pg, PAGE * H_B, hw), F32),
            pltpu.VMEM((PAGE_RING, n_pg, PAGE * H_B, hw), F32),
            pltpu.SemaphoreType.DMA((PAGE_RING, 2 * n_pg)),
        ],
    )
    return pl.pallas_call(
        functools.partial(_diff_sample_kernel, n_pg, s_len, pages_per_seq),
        grid_spec=grid_spec,
        out_shape=jax.ShapeDtypeStruct((s_n, s_len, B_W), F32),
        compiler_params=_params("arbitrary", "arbitrary"), name="diff_attn_sample",
    )(page_table.reshape(-1).astype(jnp.int32), q, k, v, ck, cv,
      blast, bnew, far, diff_lambda.astype(F32), subln_w.reshape(1, 2 * DH_B).astype(F32))


def _kv_variants(x, kv):
    lo = lax.broadcasted_iota(jnp.int32, (1, LANES), 1) < DH_C
    rolled = pltpu.roll(x, DH_C, 1)
    a_src, b_src = (x, rolled) if kv == 0 else (rolled, x)
    zero = jnp.zeros_like(x)
    return jnp.where(lo, a_src, zero).astype(BF16), jnp.where(lo, zero, b_src).astype(BF16)


def _sink_attend(units):
    scores = [[lax.dot_general(q_bf, kx, (((1,), (1,)), ((), ())), preferred_element_type=F32) + bias
               for kx, bias in zip(k_ab, bias_ab)] for q_bf, k_ab, _, bias_ab, _ in units]
    probs = []
    for s_ab, (_, _, _, _, sink_ab) in zip(scores, units):
        p_ab = []
        for s, sink in zip(s_ab, sink_ab):
            m = jnp.maximum(jnp.max(s, axis=-1, keepdims=True), sink)
            e = jnp.exp(s - m)
            p_ab.append((e / (jnp.sum(e, axis=-1, keepdims=True) + jnp.exp(sink - m))).astype(BF16))
        probs.append(p_ab)
    return [jnp.dot(p_ab[0], v_ab[0], preferred_element_type=F32) + jnp.dot(p_ab[1], v_ab[1], preferred_element_type=F32)
            for p_ab, (_, _, v_ab, _, _) in zip(probs, units)]


def _swa_prompt_kernel(q_ref, kp_ref, kc_ref, vp_ref, vc_ref, bias_ref, sink_ref, o_ref):
    n = pl.program_id(1)
    keys = jnp.concatenate([kp_ref[0], kc_ref[0]], axis=0)
    vals = jnp.concatenate([vp_ref[0], vc_ref[0]], axis=0)
    col = lax.broadcasted_iota(jnp.int32, (1, 2 * WINDOW), 1)
    first_blk = jnp.where((col < WINDOW) & (n == 0), NEG, 0.0)
    n_slot = G_C // 2
    units, slices = [], []
    for kv in range(H_KV_C):
        k_ab = _kv_variants(keys, kv)
        v_ab = _kv_variants(vals, kv)
        for ps in range(n_slot):
            h0 = kv * G_C + 2 * ps
            sl = slice(kv * G_C * DH_C + ps * LANES, kv * G_C * DH_C + (ps + 1) * LANES)
            q = (q_ref[0, :, sl] * (DH_C ** -0.5)).astype(BF16)
            bias_ab = (bias_ref[h0] + first_blk, bias_ref[h0 + 1] + first_blk)
            units.append((q, k_ab, v_ab, bias_ab, (sink_ref[h0], sink_ref[h0 + 1])))
            slices.append(sl)
    for sl, o in zip(slices, _sink_attend(units)):
        o_ref[0, :, sl] = o


def swa_prompt(q, k, v, rel_bias, sinks):
    b, t, _ = q.shape
    nb = t // WINDOW
    i = np.arange(WINDOW)
    j = np.arange(2 * WINDOW)
    dist = WINDOW + i[:, None] - j[None, :]
    bias = bias_tiles(rel_bias, dist, (dist >= 0) & (dist <= WINDOW), H_C)
    cur = lambda bi, n: (bi, n, 0)
    prev = lambda bi, n: (bi, jnp.maximum(n - 1, 0), 0)
    return pl.pallas_call(
        _swa_prompt_kernel,
        grid=(b, nb),
        in_specs=[
            pl.BlockSpec((1, WINDOW, Q_C), cur),
            pl.BlockSpec((1, WINDOW, KV_C), prev), pl.BlockSpec((1, WINDOW, KV_C), cur),
            pl.BlockSpec((1, WINDOW, KV_C), prev), pl.BlockSpec((1, WINDOW, KV_C), cur),
            pl.BlockSpec((H_C, WINDOW, 2 * WINDOW), lambda bi, n: (0, 0, 0)),
            pl.BlockSpec(memory_space=pltpu.SMEM),
        ],
        out_specs=pl.BlockSpec((1, WINDOW, Q_C), cur),
        out_shape=jax.ShapeDtypeStruct((b, t, Q_C), F32),
        compiler_params=_params("parallel", "parallel"), name="swa_prompt",
    )(q, k, k, v, v, bias, sinks.astype(F32))


def _swa_sample_kernel(gs, s_len, q_ref, kn_ref, vn_ref, ck_ref, cv_ref, bias_ref, sink_ref, o_ref):
    n_slot = G_C // 2
    pad = jnp.zeros((WINDOW - s_len, LANES), F32)
    units, where = [], []
    for si in range(gs):
        keys = jnp.concatenate([ck_ref[si], kn_ref[si], pad], axis=0)
        vals = jnp.concatenate([cv_ref[si], vn_ref[si], pad], axis=0)
        for kv in range(H_KV_C):
            k_ab = _kv_variants(keys, kv)
            v_ab = _kv_variants(vals, kv)
            base = kv * G_C * DH_C
            q = jnp.concatenate([q_ref[si, :, base + ps * LANES:base + (ps + 1) * LANES] for ps in range(n_slot)],
                                axis=0)
            q = (q * (DH_C ** -0.5)).astype(BF16)
            units.append((q, k_ab, v_ab, (bias_ref[kv, 0], bias_ref[kv, 1]), (sink_ref[kv, 0], sink_ref[kv, 1])))
            where.append((si, base))
    for (si, base), o in zip(where, _sink_attend(units)):
        for ps in range(n_slot):
            o_ref[si, :, base + ps * LANES:base + (ps + 1) * LANES] = o[ps * s_len:(ps + 1) * s_len]


def swa_sample(q, k, v, cache_k, cache_v, rel_bias, sinks, gs):
    s_n, s_len, _ = q.shape
    assert s_n % gs == 0 and cache_k.shape[1] == WINDOW
    ck = cache_k.reshape(s_n, WINDOW, KV_C)
    cv = cache_v.reshape(s_n, WINDOW, KV_C)
    tq = np.arange(s_len)
    j = np.arange(2 * WINDOW)
    dist = WINDOW + tq[:, None] - j[None, :]
    valid = (dist >= 0) & (dist <= WINDOW) & (j[None, :] < WINDOW + s_len)
    n_slot = G_C // 2
    bias = bias_tiles(rel_bias, dist, valid, H_C)
    bias = bias.reshape(H_KV_C, n_slot, 2, s_len, 2 * WINDOW).transpose(0, 2, 1, 3, 4)
    bias = bias.reshape(H_KV_C, 2, n_slot * s_len, 2 * WINDOW)
    sk = sinks.astype(F32).reshape(H_KV_C, n_slot, 2).transpose(0, 2, 1)
    sk = jnp.repeat(sk, s_len, axis=-1).reshape(H_KV_C, 2, n_slot * s_len, 1)
    grp = lambda i: (i, 0, 0)
    whole = lambda shape: pl.BlockSpec(shape, lambda i: (0,) * len(shape))
    return pl.pallas_call(
        functools.partial(_swa_sample_kernel, gs, s_len),
        grid=(s_n // gs,),
        in_specs=[
            pl.BlockSpec((gs, s_len, Q_C), grp),
            pl.BlockSpec((gs, s_len, KV_C), grp), pl.BlockSpec((gs, s_len, KV_C), grp),
            pl.BlockSpec((gs, WINDOW, KV_C), grp), pl.BlockSpec((gs, WINDOW, KV_C), grp),
            whole(bias.shape), whole(sk.shape),
        ],
        out_specs=pl.BlockSpec((gs, s_len, Q_C), grp),
        out_shape=jax.ShapeDtypeStruct((s_n, s_len, Q_C), F32),
        compiler_params=_params("parallel"), name="swa_sample",
    )(q, k, v, ck, cv, bias, sk)


ROUTE_BIG = 1 << 20


def _out_route_kernel(n_in, part_tiles, *refs):
    n_parts = len(part_tiles)
    per = 1 + n_in
    w_refs = refs[n_parts * per:n_parts * per + n_in]
    rest = refs[n_parts * per + n_in:]
    nw_ref, wr_ref, br_ref = rest[:3]
    hn_refs = rest[3:3 + n_parts]
    xn_ref, rt_ref, cout_ref, cnt_scr, tri_scr = rest[3 + n_parts:]
    step = pl.program_id(0)

    @pl.when(step == 0)
    def _():
        cnt_scr[...] = jnp.zeros(cnt_scr.shape, F32)
        ri = lax.broadcasted_iota(jnp.int32, tri_scr.shape, 0)
        ci = lax.broadcasted_iota(jnp.int32, tri_scr.shape, 1)
        tri_scr[...] = (ri > ci).astype(BF16)

    first = 0
    for p, nt in enumerate(part_tiles):
        part_refs = refs[p * per:(p + 1) * per]

        @pl.when((step >= first) & (step < first + nt))
        def _(part_refs=part_refs, hn_ref=hn_refs[p]):
            _out_route_tile(part_refs[0], part_refs[1:], w_refs, nw_ref, wr_ref, br_ref, hn_ref, xn_ref, rt_ref,
                            cnt_scr, tri_scr)
        first += nt

    @pl.when(step == pl.num_programs(0) - 1)
    def _():
        cout_ref[...] = cnt_scr[...]


def _out_route_tile(h_ref, o_refs, w_refs, nw_ref, wr_ref, br_ref, hn_ref, xn_ref, rt_ref, cnt_scr, tri_scr):
    h = h_ref[...]
    for o_ref, w_ref in zip(o_refs, w_refs):
        h = h + jnp.dot(o_ref[...].astype(BF16), w_ref[...], preferred_element_type=F32)
    hn_ref[...] = h
    xn = h * lax.rsqrt(jnp.mean(h * h, axis=-1, keepdims=True) + EPS) * nw_ref[...]
    xn_ref[...] = xn
    x_hi = xn.astype(BF16)
    x_lo = (xn - x_hi.astype(F32)).astype(BF16)
    logit = (jnp.dot(x_hi, wr_ref[0], preferred_element_type=F32) + jnp.dot(x_lo, wr_ref[0], preferred_element_type=F32)
             + jnp.dot(x_hi, wr_ref[1], preferred_element_type=F32) + br_ref[...])
    lane = lax.broadcasted_iota(jnp.int32, logit.shape, 1)
    gmask = lane < N_GROUPS
    gl = jnp.where(gmask, logit, NEG)
    gmax = jnp.max(gl, axis=-1, keepdims=True)
    gsel = jnp.min(jnp.where(gl == gmax, lane, ROUTE_BIG), axis=-1, keepdims=True)
    gw = 1.0 / jnp.sum(jnp.where(gmask, jnp.exp(gl - gmax), 0.0), axis=-1, keepdims=True)
    eid = lane - N_GROUPS
    emask = (eid >= 0) & (eid < N_EXPERTS) & (jnp.right_shift(eid, 3) == gsel)
    el = jnp.where(emask, logit, NEG)
    v1 = jnp.max(el, axis=-1, keepdims=True)
    i1 = jnp.min(jnp.where((el == v1) & emask, lane, ROUTE_BIG), axis=-1, keepdims=True)
    emask2 = emask & (lane != i1)
    el2 = jnp.where(emask2, logit, NEG)
    v2 = jnp.max(el2, axis=-1, keepdims=True)
    i2 = jnp.min(jnp.where((el2 == v2) & emask2, lane, ROUTE_BIG), axis=-1, keepdims=True)
    e = jnp.exp(v2 - v1)
    w1 = gw / (1.0 + e)
    w2 = gw * e / (1.0 + e)
    onehot = ((lane == i1) | (lane == i2)).astype(BF16)
    before = jnp.dot(tri_scr[...], onehot, preferred_element_type=F32) + cnt_scr[...]
    r1 = jnp.sum(jnp.where(lane == i1, before, 0.0), axis=-1, keepdims=True)
    r2 = jnp.sum(jnp.where(lane == i2, before, 0.0), axis=-1, keepdims=True)
    cnt_scr[...] = cnt_scr[...] + jnp.sum(onehot.astype(F32), axis=0, keepdims=True)
    rt = jnp.where(lane == 0, (i1 - N_GROUPS).astype(F32),
                   jnp.where(lane == 1, (i2 - N_GROUPS).astype(F32),
                             jnp.where(lane == 2, w1, jnp.where(lane == 3, w2,
                                                                jnp.where(lane == 4, r1,
                                                                          jnp.where(lane == 5, r2, 0.0))))))
    rt_ref[...] = rt


def out_route(parts, weights_bf16, norm_w, w_group, b_group, w_router, b_router, tm=512):
    d = parts[0][0].shape[1]
    n_in = len(weights_bf16)
    tm = min([tm] + [h.shape[0] for h, _ in parts])
    assert all(h.shape[0] % tm == 0 for h, _ in parts) and EXP_PER_GROUP == 8
    part_tiles = tuple(h.shape[0] // tm for h, _ in parts)
    n_all = sum(h.shape[0] for h, _ in parts)
    wr = jnp.zeros((d, LANES), F32).at[:, :N_GROUPS].set(w_group).at[:, N_GROUPS:N_GROUPS + N_EXPERTS].set(w_router)
    br = jnp.zeros((1, LANES), F32).at[0, :N_GROUPS].set(b_group).at[0, N_GROUPS:N_GROUPS + N_EXPERTS].set(b_router)
    wr_hi = wr.astype(BF16)
    wr = jnp.stack([wr_hi, (wr - wr_hi.astype(F32)).astype(BF16)])
    row = lambda i: (i, 0)
    fixed = lambda i: (0, 0)

    def part_row(first, nt):
        return lambda i: (jnp.clip(i - first, 0, nt - 1), 0)

    in_specs, args, hn_specs, hn_shapes, first = [], [], [], [], 0
    for (h, outs), nt in zip(parts, part_tiles):
        assert len(outs) == n_in
        in_specs += [pl.BlockSpec((tm, a.shape[1]), part_row(first, nt)) for a in (h, *outs)]
        args += [h, *outs]
        hn_specs.append(pl.BlockSpec((tm, d), part_row(first, nt)))
        hn_shapes.append(jax.ShapeDtypeStruct(h.shape, F32))
        first += nt
    in_specs += [pl.BlockSpec(w.shape, fixed) for w in weights_bf16]
    in_specs += [pl.BlockSpec((1, d), fixed), pl.BlockSpec((2, d, LANES), lambda i: (0, 0, 0)),
                 pl.BlockSpec((1, LANES), fixed)]
    res = pl.pallas_call(
        functools.partial(_out_route_kernel, n_in, part_tiles),
        grid=(sum(part_tiles),), in_specs=in_specs,
        out_specs=hn_specs + [pl.BlockSpec((tm, d), row), pl.BlockSpec((tm, LANES), row),
                              pl.BlockSpec((1, LANES), fixed)],
        out_shape=hn_shapes + [jax.ShapeDtypeStruct((n_all, d), F32), jax.ShapeDtypeStruct((n_all, LANES), F32),
                               jax.ShapeDtypeStruct((1, LANES), F32)],
        scratch_shapes=[pltpu.VMEM((1, LANES), F32), pltpu.VMEM((tm, tm), BF16)],
        compiler_params=_params("arbitrary"), name="out_route",
    )(*args, *weights_bf16, norm_w.reshape(1, d), wr, br)
    return res[:len(parts)], res[len(parts)], res[len(parts) + 1], res[len(parts) + 2]


def _experts_kernel(te_ref, nv_ref, x_ref, wg_ref, wu_ref, wd_ref, y_ref, wg_bf, wu_bf, wd_bf):
    i = pl.program_id(0)

    @pl.when((i == 0) | (te_ref[i] != te_ref[jnp.maximum(i - 1, 0)]))
    def _():
        wg_bf[...] = wg_ref[0, 0].astype(BF16)
        wu_bf[...] = wu_ref[0, 0].astype(BF16)
        wd_bf[...] = wd_ref[0, 0].astype(BF16)

    @pl.when(i < nv_ref[0])
    def _():
        x = x_ref[...].astype(BF16)
        g = jnp.dot(x, wg_bf[...], preferred_element_type=F32)
        u = jnp.dot(x, wu_bf[...], preferred_element_type=F32)
        hh = (_silu(g) * u).astype(BF16)
        y_ref[...] = jnp.dot(hh, wd_bf[...], preferred_element_type=F32)

    @pl.when(i >= nv_ref[0])
    def _():
        y_ref[...] = jnp.zeros(y_ref.shape, F32)


def moe_experts(xs, tile_expert, n_valid, w_gate, w_up, w_down, layer, tm):
    p, d = xs.shape
    f = w_gate.shape[-1]
    n_tiles = p // tm
    grid_spec = pltpu.PrefetchScalarGridSpec(
        num_scalar_prefetch=2,
        grid=(n_tiles,),
        in_specs=[
            pl.BlockSpec((tm, d), lambda i, te, nv: (i, 0)),
            pl.BlockSpec((1, 1, d, f), lambda i, te, nv: (layer, te[i], 0, 0)),
            pl.BlockSpec((1, 1, d, f), lambda i, te, nv: (layer, te[i], 0, 0)),
            pl.BlockSpec((1, 1, f, d), lambda i, te, nv: (layer, te[i], 0, 0)),
        ],
        out_specs=pl.BlockSpec((tm, d), lambda i, te, nv: (i, 0)),
        scratch_shapes=[pltpu.VMEM((d, f), BF16), pltpu.VMEM((d, f), BF16), pltpu.VMEM((f, d), BF16)],
    )
    return pl.pallas_call(
        _experts_kernel, grid_spec=grid_spec,
        out_shape=jax.ShapeDtypeStruct((p, d), F32),
        compiler_params=_params("arbitrary"), name="moe_experts",
    )(tile_expert, n_valid, xs, w_gate, w_up, w_down)


def _take_rows(x, idx):
    return x.at[idx].get(mode="promise_in_bounds", unique_indices=True)


def hier_moe(xn, route, counts, w_gate, w_up, w_down, layer, tm=512):
    n, d = xn.shape
    e_idx = route[:, :TOP_K].astype(jnp.int32)
    rank = route[:, 2 * TOP_K:3 * TOP_K].astype(jnp.int32)
    cnt = counts[0, N_GROUPS:N_GROUPS + N_EXPERTS].astype(jnp.int32)
    padded = ((cnt + tm - 1) // tm) * tm
    pad_ends = jnp.cumsum(padded)
    pad_starts = pad_ends - padded
    experts = jnp.arange(N_EXPERTS, dtype=jnp.int32)
    pos = jnp.sum(jnp.where(e_idx[..., None] == experts, pad_starts, 0), axis=-1) + rank
    n_tiles = -(-(n * TOP_K + N_EXPERTS * (tm - 1)) // tm)
    p = n_tiles * tm
    tok = jnp.broadcast_to(jnp.arange(n, dtype=jnp.int32)[:, None], (n, TOP_K))
    n_valid = (pad_ends[-1] // tm).astype(jnp.int32)
    tile_start = jnp.minimum(jnp.arange(n_tiles, dtype=jnp.int32), n_valid - 1) * tm
    tile_e = jnp.sum((tile_start[:, None] >= pad_ends[None, :]).astype(jnp.int32), axis=-1)
    tile_e = jnp.minimum(tile_e, N_EXPERTS - 1)
    _, tok_sorted = lax.sort_key_val(pos.reshape(-1), tok.reshape(-1))
    starts = jnp.cumsum(cnt) - cnt
    rows = jnp.arange(p, dtype=jnp.int32).reshape(n_tiles, tm)
    off = rows - pad_starts[tile_e][:, None]
    src = jnp.where(off < cnt[tile_e][:, None], starts[tile_e][:, None] + off, rows % (n * TOP_K))
    row_tok = tok_sorted.at[src.reshape(-1)].get(mode="promise_in_bounds")
    xs = xn.at[row_tok].get(mode="promise_in_bounds")
    ys = moe_experts(xs, tile_e, n_valid.reshape(1), w_gate, w_up, w_down, layer, tm)
    return tuple(_take_rows(ys, pos[:, kk]) for kk in range(TOP_K))


def _final_norm_kernel(h_ref, rt_ref, a_ref, b_ref, nw_ref, o_ref):
    h = _add_expert_outputs(h_ref[...], rt_ref, (a_ref, b_ref))
    o_ref[...] = h * lax.rsqrt(jnp.mean(h * h, axis=-1, keepdims=True) + EPS) * nw_ref[...]


def final_norm(h, route, a, b, norm_w, add_row0=0, tm=256):
    n, d = h.shape
    tm = min(tm, n)
    assert add_row0 % tm == 0
    row = pl.BlockSpec((tm, d), lambda i: (i, 0))
    add_row = pl.BlockSpec((tm, d), lambda i: (i + add_row0 // tm, 0))
    return pl.pallas_call(
        _final_norm_kernel, grid=(n // tm,),
        in_specs=[row, pl.BlockSpec((tm, LANES), lambda i: (i + add_row0 // tm, 0)), add_row, add_row,
                  pl.BlockSpec((1, d), lambda i: (0, 0))],
        out_specs=row, out_shape=jax.ShapeDtypeStruct((n, d), F32),
        compiler_params=_params("parallel"), name="final_norm",
    )(h, route, a, b, norm_w.reshape(1, d))


def kernel(x_prompt, x_sample, state_a_conv, state_a_ssm, cache_b_k, cache_b_v, cache_c_k, cache_c_v, page_table, norm_mix, norm_ffn, norm_final, rel_bias, w_in0, conv_w, a_log, dt_bias, gdn_norm_w, diff_lambda, diff_subln_w, w_out0, w_in1, b_in1, sinks, w_out1, w_group, b_group, w_router, b_router, w_gate, w_up, w_down):
    bp, tp, d = x_prompt.shape
    sn, sl, _ = x_sample.shape
    n_p, n_s = bp * tp, sn * sl
    parts = ((x_prompt.reshape(n_p, d), bp, tp), (x_sample.reshape(n_s, d), sn, sl))

    off_z = CONV_CH
    off_a = off_z + H_A * DV_A
    off_qb = off_a + 2 * H_A
    w0 = jnp.concatenate([w_in0[:, :off_a], w_in0[:, off_a:off_qb],
                          jnp.zeros((d, LANES - 2 * H_A), w_in0.dtype), w_in0[:, off_qb:]], axis=1).astype(BF16)
    widths0 = (CONV_CH, H_A * DV_A, LANES, B_W, B_W, B_W)
    w_out0_bf = w_out0.astype(BF16)
    w_out0_parts = [w_out0_bf[:H_A * DV_A], w_out0_bf[H_A * DV_A:]]

    row0 = (0, n_p)

    def moe_layer(layer, xn_all, route_all, counts):
        return (route_all,) + hier_moe(xn_all, route_all, counts, w_gate, w_up, w_down, layer)

    conv_states = (jnp.zeros((bp, CONV_W - 1, CONV_CH), F32), state_a_conv)
    ssm_states = (jnp.zeros((bp, H_A, DK_A, DV_A), F32), state_a_ssm)
    mixed, conv_new, ssm_new, kb_new, vb_new = [], [], [], [], []
    for idx, (x2, b, t) in enumerate(parts):
        qkv, z, ab, qb, kb, vb, kb4, vb4 = norm_proj(x2, [], norm_mix[0], w0, None, widths0, emit_h=False,
                                                     head_split=(4, 5))
        r3 = lambda a: a.reshape(b, t, a.shape[-1])
        o_a, s_new = gdn_heads(r3(qkv), r3(z), r3(ab), conv_states[idx], ssm_states[idx], conv_w, a_log, dt_bias,
                               gdn_norm_w, min(GDN_CHUNK, t), 2 if idx == 0 else 8)
        if idx == 0:
            o_b = diff_attn_prompt(r3(qb), r3(kb), r3(vb), rel_bias, diff_lambda, diff_subln_w, 512)
        else:
            o_b = diff_attn_sample(r3(qb), r3(kb), r3(vb), cache_b_k, cache_b_v, page_table, rel_bias, diff_lambda,
                                   diff_subln_w, min(16, page_table.shape[1]))
        mixed.append((x2, [o_a.reshape(b * t, -1), o_b.reshape(b * t, -1)]))
        conv_new.append(r3(qkv)[:, t - (CONV_W - 1):, :])
        ssm_new.append(s_new)
        kb_new.append(kb4.reshape(b, t, H_B, 2 * DH_B))
        vb_new.append(vb4.reshape(b, t, H_B, 2 * DH_B))
    hs, xn_all, route_all, counts = out_route(mixed, w_out0_parts, norm_ffn[0], w_group[0], b_group[0],
                                              w_router[0], b_router[0])
    moe_out = moe_layer(0, xn_all, route_all, counts)

    w1 = w_in1.astype(BF16)
    w_out1_bf = w_out1.astype(BF16)
    mixed, kc_new, vc_new = [], [], []
    for idx, (_, b, t) in enumerate(parts):
        h2, q, k, v = norm_proj(hs[idx], moe_out, norm_mix[1], w1, b_in1, (Q_C, KV_C, KV_C),
                                emit_h=True, add_row0=row0[idx])
        r3 = lambda a: a.reshape(b, t, a.shape[-1])
        if idx == 0:
            o_c = swa_prompt(r3(q), r3(k), r3(v), rel_bias, sinks)
            kc_new.append(r3(k)[:, t - WINDOW:].reshape(b, WINDOW, H_KV_C, DH_C))
            vc_new.append(r3(v)[:, t - WINDOW:].reshape(b, WINDOW, H_KV_C, DH_C))
        else:
            o_c = swa_sample(r3(q), r3(k), r3(v), cache_c_k, cache_c_v, rel_bias, sinks, 8)
            kc_new.append(jnp.concatenate([cache_c_k[:, t:], k.reshape(b, t, H_KV_C, DH_C)], axis=1))
            vc_new.append(jnp.concatenate([cache_c_v[:, t:], v.reshape(b, t, H_KV_C, DH_C)], axis=1))
        mixed.append((h2, [o_c.reshape(b * t, -1)]))
    hs2, xn_all, route_all, counts = out_route(mixed, [w_out1_bf], norm_ffn[1], w_group[1], b_group[1],
                                               w_router[1], b_router[1])
    moe_out = moe_layer(1, xn_all, route_all, counts)

    y_out = [final_norm(hs2[idx], *moe_out, norm_final, add_row0=row0[idx]).reshape(b, t, d)
             for idx, (_, b, t) in enumerate(parts)]
    return (y_out[0], y_out[1], conv_new[0], conv_new[1], ssm_new[0], ssm_new[1],
            kb_new[0], vb_new[0], kb_new[1], vb_new[1], kc_new[0], vc_new[0], kc_new[1], vc_new[1])
```
